```python
import jax
import jax.numpy as jnp
from jax import lax
import numpy as np


D_MODEL = 2048
BATCH = 1
SEQ = 8192
DEPTH = 4

ROPE_THETA = 10000.0
ROPE_DIM = 64
NORM_EPS = 1e-5

SSM_HEADS = 32
SSM_HEAD_DIM = 64
SSM_INNER = SSM_HEADS * SSM_HEAD_DIM
SSM_GROUPS = 4
SSM_HEADS_PER_GROUP = SSM_HEADS // SSM_GROUPS
SSM_STATE = 128
SSM_CONV = 4
SSM_CHUNK = 128
SSM_CONV_DIM = SSM_INNER + 2 * SSM_GROUPS * SSM_STATE

SWA_Q_HEADS = 16
SWA_KV_HEADS = 2
SWA_HEAD_DIM = ROPE_DIM
SWA_WINDOW = 128
ATTN_BLOCK = 128

EVEN_IN = SSM_INNER + SSM_CONV_DIM + SSM_HEADS + (SWA_Q_HEADS + 2 * SWA_KV_HEADS) * SWA_HEAD_DIM
EVEN_MIX = SSM_INNER + SWA_Q_HEADS * SWA_HEAD_DIM

MLA_HEADS = 16
MLA_NOPE = 128
MLA_ROPE = ROPE_DIM
MLA_V = 128
MLA_RANK = 512
MLA_SCALE = (MLA_NOPE + MLA_ROPE) ** -0.5
IDX_HEADS = 16
IDX_DIM = ROPE_DIM
IDX_TOPK_MAX = 256
ODD_IN = MLA_HEADS * (MLA_NOPE + MLA_ROPE) + MLA_RANK + MLA_ROPE + IDX_HEADS * IDX_DIM + IDX_DIM + IDX_HEADS
ODD_MIX = MLA_HEADS * MLA_V

MOE_GROUPS = 4
MOE_EXPERTS_PER_GROUP = 8
MOE_EXPERTS = MOE_GROUPS * MOE_EXPERTS_PER_GROUP
MOE_TOP_K = 2
MOE_FF = 512
MOE_BLOCK = 128

PLE_DIM = 256

DEEPNORM_ALPHA = (2 * DEPTH) ** 0.25
DEEPNORM_BETA = (8 * DEPTH) ** -0.25
N_EVEN = (DEPTH + 1) // 2
N_ODD = DEPTH // 2

kernel_name = 'hybrid_ssd_swa_dsa_hmoe_deepnorm'

F32 = jnp.float32


def _split(a, sizes):
    return jnp.split(a, [int(v) for v in np.cumsum(sizes)[:-1]], axis=-1)


def layer_norm(x, g, b):
    xf = x.astype(F32)
    xc = xf - xf.mean(-1, keepdims=True)
    var = jnp.mean(xc * xc, -1, keepdims=True)
    return (xc * lax.rsqrt(var + NORM_EPS) * g.astype(F32) + b.astype(F32)).astype(x.dtype)


def rms_norm(x, g):
    xf = x.astype(F32)
    return (xf * lax.rsqrt(jnp.mean(xf * xf, -1, keepdims=True) + NORM_EPS) * g.astype(F32)).astype(x.dtype)


def rope_tables(positions):
    inv = ROPE_THETA ** (-jnp.arange(0, ROPE_DIM, 2, dtype=F32) / ROPE_DIM)
    ang = positions.astype(F32)[..., None] * inv
    return jnp.cos(ang), jnp.sin(ang)


def apply_rope(t, cos, sin):
    tf = t.astype(F32)
    t1, t2 = jnp.split(tf, 2, axis=-1)
    c = cos[:, :, None, :]
    s_ = sin[:, :, None, :]
    return jnp.concatenate([t1 * c - t2 * s_, t2 * c + t1 * s_], axis=-1).astype(t.dtype)


def causal_dwconv(u, w, bias):
    c = u.shape[-1]
    out = lax.conv_general_dilated(u, w[:, None, :].astype(u.dtype), window_strides=(1,),
                                   padding=[(SSM_CONV - 1, 0)],
                                   dimension_numbers=('NWC', 'WIO', 'NWC'),
                                   feature_group_count=c)
    return out + bias.astype(u.dtype)


def ssd_chunked(xdt, a, bm, cm):
    b, s, g, r, p = xdt.shape
    n = bm.shape[-1]
    nc = s // SSM_CHUNK
    X = xdt.reshape(b, nc, SSM_CHUNK, g, r, p)
    A = a.reshape(b, nc, SSM_CHUNK, g, r)
    B = bm.reshape(b, nc, SSM_CHUNK, g, n)
    C = cm.reshape(b, nc, SSM_CHUNK, g, n)
    a_cs = jnp.cumsum(A, axis=2)
    seg = a_cs[:, :, :, None] - a_cs[:, :, None, :]
    causal = jnp.tril(jnp.ones((SSM_CHUNK, SSM_CHUNK), dtype=bool))[:, :, None, None]
    decay = jnp.exp(jnp.where(causal, seg, -jnp.inf))
    cb = jnp.einsum('bclgn,bcsgn->bclsg', C, B)
    y_diag = jnp.einsum('bclsgr,bcsgrp->bclgrp', decay * cb[..., None], X)
    decay_to_end = jnp.exp(a_cs[:, :, -1:] - a_cs)
    states = jnp.einsum('bclgn,bclgrp->bcgrpn', B, X * decay_to_end[..., None])
    chunk_decay = jnp.exp(a_cs[:, :, -1])

    def step(h, inp):
        st, dec = inp
        return h * dec[..., None, None] + st, h

    h0 = jnp.zeros((b, g, r, p, n), F32)
    _, prev = lax.scan(step, h0, (jnp.moveaxis(states, 1, 0), jnp.moveaxis(chunk_decay, 1, 0)))
    prev = jnp.moveaxis(prev, 0, 1)
    y_off = jnp.einsum('bclgn,bcgrpn->bclgrp', C, prev) * jnp.exp(a_cs)[..., None]
    return (y_diag + y_off).reshape(b, s, g, r, p)


def mamba2_group(z, xbc, dt_raw, conv_w, conv_b, dt_bias, a_log, d_skip, ssm_norm):
    b, s, _ = z.shape
    G, R, P, N = SSM_GROUPS, SSM_HEADS_PER_GROUP, SSM_HEAD_DIM, SSM_STATE
    xbc = jax.nn.silu(causal_dwconv(xbc, conv_w, conv_b))
    xs, bm, cm = _split(xbc, [SSM_INNER, G * N, G * N])
    xf = xs.reshape(b, s, G, R, P).astype(F32)
    bm = bm.reshape(b, s, G, N).astype(F32)
    cm = cm.reshape(b, s, G, N).astype(F32)
    dt = jax.nn.softplus(dt_raw.astype(F32) + dt_bias.astype(F32)).reshape(b, s, G, R)
    a = -jnp.exp(a_log.astype(F32)).reshape(G, R)
    y = ssd_chunked(xf * dt[..., None], dt * a, bm, cm)
    y = y + xf * d_skip.astype(F32).reshape(G, R)[:, :, None]
    y = y * jax.nn.silu(z.astype(F32)).reshape(b, s, G, R, P)
    y = y.reshape(b, s, G, R * P)
    y = y * lax.rsqrt(jnp.mean(y * y, -1, keepdims=True) + NORM_EPS) * ssm_norm.astype(F32).reshape(G, R * P)
    return y.reshape(b, s, SSM_INNER).astype(z.dtype)


def swa_sink_attention(q, k, v, sinks):
    b, s, _, d = q.shape
    nb = s // ATTN_BLOCK
    grp = SWA_Q_HEADS // SWA_KV_HEADS
    qb = q.reshape(b, nb, ATTN_BLOCK, SWA_KV_HEADS, grp, d)

    def banded(t):
        t = t.reshape(b, nb, ATTN_BLOCK, SWA_KV_HEADS, d)
        prev = jnp.concatenate([jnp.zeros_like(t[:, :1]), t[:, :-1]], axis=1)
        return jnp.concatenate([prev, t], axis=2)

    kw, vw = banded(k), banded(v)
    logits = jnp.einsum('bnqkgd,bnskd->bnkgqs', qb, kw).astype(F32) * (d ** -0.5)
    qpos = jnp.arange(ATTN_BLOCK)[:, None] + ATTN_BLOCK
    kpos = jnp.arange(2 * ATTN_BLOCK)[None, :]
    rel = qpos - kpos
    band = (rel >= 0) & (rel < SWA_WINDOW)
    key_abs = jnp.arange(nb)[:, None] * ATTN_BLOCK + kpos - ATTN_BLOCK
    mask = band[None] & (key_abs >= 0)[:, None, :]
    logits = jnp.where(mask[None, :, None, None], logits, -jnp.inf)
    sink = sinks.astype(F32).reshape(SWA_KV_HEADS, grp)[None, None, :, :, None, None]
    m = jnp.maximum(logits.max(-1, keepdims=True), sink)
    e = jnp.exp(logits - m)
    prob = e / (e.sum(-1, keepdims=True) + jnp.exp(sink - m))
    out = jnp.einsum('bnkgqs,bnskd->bnqkgd', prob.astype(v.dtype), vw)
    return out.reshape(b, s, SWA_Q_HEADS * d)


def ssd_swa_mixer(x, cos, sin, w_in, conv_w, conv_b, dt_bias, a_log, d_skip, ssm_norm, sinks, w_out):
    b, s, _ = x.shape
    kvw = SWA_KV_HEADS * SWA_HEAD_DIM
    z, xbc, dt_raw, q, k, v = _split(x @ w_in, [SSM_INNER, SSM_CONV_DIM, SSM_HEADS,
                                                 SWA_Q_HEADS * SWA_HEAD_DIM, kvw, kvw])
    y_ssm = mamba2_group(z, xbc, dt_raw, conv_w, conv_b, dt_bias, a_log, d_skip, ssm_norm)
    q = apply_rope(q.reshape(b, s, SWA_Q_HEADS, SWA_HEAD_DIM), cos, sin)
    k = apply_rope(k.reshape(b, s, SWA_KV_HEADS, SWA_HEAD_DIM), cos, sin)
    v = v.reshape(b, s, SWA_KV_HEADS, SWA_HEAD_DIM)
    y_att = swa_sink_attention(q, k, v, sinks)
    return jnp.concatenate([y_ssm, y_att], axis=-1) @ w_out


def dsa_mla_mixer(x, cos, sin, w_in, kv_norm, w_uk, w_uv, w_out):
    b, s, _ = x.shape
    q, ckv, krope, qi, ki, wi = _split(x @ w_in, [MLA_HEADS * (MLA_NOPE + MLA_ROPE), MLA_RANK, MLA_ROPE,
                                                  IDX_HEADS * IDX_DIM, IDX_DIM, IDX_HEADS])
    q = q.reshape(b, s, MLA_HEADS, MLA_NOPE + MLA_ROPE)
    q_nope, q_rope = q[..., :MLA_NOPE], q[..., MLA_NOPE:]
    q_rope = apply_rope(q_rope, cos, sin)
    krope = apply_rope(krope[:, :, None, :], cos, sin)[:, :, 0]
    ckv = rms_norm(ckv, kv_norm)
    q_lat = jnp.einsum('bshd,hdr->bshr', q_nope, w_uk)
    qi = apply_rope(qi.reshape(b, s, IDX_HEADS, IDX_DIM), cos, sin)
    ki = apply_rope(ki[:, :, None, :], cos, sin)[:, :, 0]
    wi = wi * (IDX_HEADS ** -0.5 * IDX_DIM ** -0.5)
    topk = min(IDX_TOPK_MAX, s // 4)
    nb = s // ATTN_BLOCK
    key_pos = jnp.arange(s)
    bi = jnp.arange(b)[:, None, None]

    def blocks(t):
        return jnp.moveaxis(t.reshape(b, nb, ATTN_BLOCK, *t.shape[2:]), 1, 0)

    def attend(args):
        qi_b, wi_b, ql_b, qr_b, t_pos = args
        sc = jax.nn.relu(jnp.einsum('bqhd,bsd->bqhs', qi_b, ki))
        idx_score = jnp.einsum('bqhs,bqh->bqs', sc, wi_b).astype(F32)
        visible = key_pos[None, :] <= t_pos[:, None]
        idx_score = jnp.where(visible[None], idx_score, -jnp.inf)
        _, sel = lax.top_k(idx_score, topk)
        valid = sel <= t_pos[None, :, None]
        c_sel = ckv[bi, sel]
        kr_sel = krope[bi, sel]
        logits = (jnp.einsum('bqhr,bqkr->bqhk', ql_b, c_sel)
                  + jnp.einsum('bqhd,bqkd->bqhk', qr_b, kr_sel)).astype(F32) * MLA_SCALE
        logits = jnp.where(valid[:, :, None, :], logits, -jnp.inf)
        prob = jax.nn.softmax(logits, axis=-1).astype(c_sel.dtype)
        return jnp.einsum('bqhk,bqkr->bqhr', prob, c_sel)

    o_lat = lax.map(attend, (blocks(qi), blocks(wi), blocks(q_lat), blocks(q_rope),
                             jnp.arange(s).reshape(nb, ATTN_BLOCK)))
    o_lat = jnp.moveaxis(o_lat, 0, 1).reshape(b, s, MLA_HEADS, MLA_RANK)
    o = jnp.einsum('bshr,hrv->bshv', o_lat, w_uv).reshape(b, s, ODD_MIX)
    return o @ w_out


def hier_moe(x, r_group, r_group_b, r_expert, r_expert_b, w_gate, w_up, w_down):
    b, s, d = x.shape
    h = x.reshape(-1, d)
    t = h.shape[0]
    g_logits = (h @ r_group).astype(F32) + r_group_b.astype(F32)
    g_sel = jnp.argmax(g_logits, axis=-1)
    g_gate = jnp.take_along_axis(jax.nn.softmax(g_logits, axis=-1), g_sel[:, None], axis=-1)
    e_logits = ((h @ r_expert).astype(F32) + r_expert_b.astype(F32)).reshape(t, MOE_GROUPS, MOE_EXPERTS_PER_GROUP)
    e_logits = jnp.take_along_axis(e_logits, g_sel[:, None, None], axis=1)[:, 0]
    top_v, top_i = lax.top_k(e_logits, MOE_TOP_K)
    gate = jax.nn.softmax(top_v, axis=-1) * g_gate
    expert_id = (g_sel[:, None] * MOE_EXPERTS_PER_GROUP + top_i).reshape(-1)
    n = expert_id.shape[0]
    order = jnp.argsort(expert_id)
    sorted_e = expert_id[order]
    counts = jnp.bincount(expert_id, length=MOE_EXPERTS)
    padded = (counts + MOE_BLOCK - 1) // MOE_BLOCK * MOE_BLOCK
    pad_end = jnp.cumsum(padded)
    pad_start = pad_end - padded
    start = jnp.cumsum(counts) - counts
    dest = pad_start[sorted_e] + jnp.arange(n) - start[sorted_e]
    n_rows = n + MOE_EXPERTS * MOE_BLOCK
    row_token = jnp.full((n_rows,), t, jnp.int32).at[dest].set((order // MOE_TOP_K).astype(jnp.int32))
    h_pad = jnp.concatenate([h, jnp.zeros((1, d), h.dtype)], axis=0)
    xin = h_pad[row_token].reshape(n_rows // MOE_BLOCK, MOE_BLOCK, d)
    blk_start = jnp.arange(n_rows // MOE_BLOCK) * MOE_BLOCK
    blk_e = jnp.minimum(jnp.searchsorted(pad_end, blk_start, side='right'), MOE_EXPERTS - 1)

    def expert_block(args):
        xb, e = args
        hid = jax.nn.silu(xb @ w_gate[e]) * (xb @ w_up[e])
        return hid @ w_down[e]

    y_rows = lax.map(expert_block, (xin, blk_e)).reshape(n_rows, d)
    y_assign = jnp.zeros((n, d), y_rows.dtype).at[order].set(y_rows[dest])
    y = (y_assign.reshape(t, MOE_TOP_K, d) * gate[..., None].astype(y_rows.dtype)).sum(axis=1)
    return y.reshape(b, s, d)


def setup_inputs(seed: int = 0) -> dict:
    key = jax.random.key(seed)
    ks = iter(jax.random.split(key, 40))

    def nrm(shape, scale):
        return jax.random.normal(next(ks), shape, F32) * scale

    dt = jnp.exp(jax.random.uniform(next(ks), (N_EVEN, SSM_HEADS), F32)
                 * (np.log(0.1) - np.log(0.001)) + np.log(0.001))
    return {
        'x': nrm((BATCH, SEQ, D_MODEL), 1.0),
        'p': nrm((DEPTH, BATCH, SEQ, PLE_DIM), 1.0),
        'positions': (jax.random.randint(next(ks), (BATCH, 1), 0, 4096, jnp.int32)
                      + jnp.arange(SEQ, dtype=jnp.int32)[None, :]),
        'ev_w_in': nrm((N_EVEN, D_MODEL, EVEN_IN), D_MODEL ** -0.5),
        'ev_conv_w': nrm((N_EVEN, SSM_CONV, SSM_CONV_DIM), SSM_CONV ** -0.5),
        'ev_conv_b': nrm((N_EVEN, SSM_CONV_DIM), 0.01),
        'ev_dt_bias': dt + jnp.log(-jnp.expm1(-dt)),
        'ev_a_log': jnp.log(jax.random.uniform(next(ks), (N_EVEN, SSM_HEADS), F32, 1.0, 16.0)),
        'ev_d_skip': 1.0 + nrm((N_EVEN, SSM_HEADS), 0.01),
        'ev_ssm_norm': 1.0 + nrm((N_EVEN, SSM_INNER), 0.01),
        'ev_sinks': nrm((N_EVEN, SWA_Q_HEADS), 0.5),
        'ev_w_out': nrm((N_EVEN, EVEN_MIX, D_MODEL), EVEN_MIX ** -0.5 * DEEPNORM_BETA),
        'od_w_in': nrm((N_ODD, D_MODEL, ODD_IN), D_MODEL ** -0.5),
        'od_kv_norm': 1.0 + nrm((N_ODD, MLA_RANK), 0.01),
        'od_w_uk': nrm((N_ODD, MLA_HEADS, MLA_NOPE, MLA_RANK), MLA_RANK ** -0.5),
        'od_w_uv': nrm((N_ODD, MLA_HEADS, MLA_RANK, MLA_V), MLA_RANK ** -0.5),
        'od_w_out': nrm((N_ODD, ODD_MIX, D_MODEL), ODD_MIX ** -0.5 * DEEPNORM_BETA),
        'ln1_g': 1.0 + nrm((DEPTH, D_MODEL), 0.01),
        'ln1_b': nrm((DEPTH, D_MODEL), 0.01),
        'ln2_g': 1.0 + nrm((DEPTH, D_MODEL), 0.01),
        'ln2_b': nrm((DEPTH, D_MODEL), 0.01),
        'moe_router_group': nrm((DEPTH, D_MODEL, MOE_GROUPS), D_MODEL ** -0.5),
        'moe_router_group_b': nrm((DEPTH, MOE_GROUPS), 0.01),
        'moe_router_expert': nrm((DEPTH, D_MODEL, MOE_EXPERTS), D_MODEL ** -0.5),
        'moe_router_expert_b': nrm((DEPTH, MOE_EXPERTS), 0.01),
        'moe_w_gate': nrm((DEPTH, MOE_EXPERTS, D_MODEL, MOE_FF), D_MODEL ** -0.5),
        'moe_w_up': nrm((DEPTH, MOE_EXPERTS, D_MODEL, MOE_FF), D_MODEL ** -0.5),
        'moe_w_down': nrm((DEPTH, MOE_EXPERTS, MOE_FF, D_MODEL), MOE_FF ** -0.5 * DEEPNORM_BETA),
        'ple_w_proj': nrm((DEPTH, PLE_DIM, D_MODEL), PLE_DIM ** -0.5),
        'ple_w_gate': nrm((DEPTH, D_MODEL, D_MODEL), D_MODEL ** -0.5),
        'ple_b_gate': nrm((DEPTH, D_MODEL), 0.01),
    }


def reference(x, p, positions, ev_w_in, ev_conv_w, ev_conv_b, ev_dt_bias, ev_a_log, ev_d_skip,
              ev_ssm_norm, ev_sinks, ev_w_out, od_w_in, od_kv_norm, od_w_uk, od_w_uv, od_w_out,
              ln1_g, ln1_b, ln2_g, ln2_b, moe_router_group, moe_router_group_b, moe_router_expert,
              moe_router_expert_b, moe_w_gate, moe_w_up, moe_w_down, ple_w_proj, ple_w_gate, ple_b_gate):
    cos, sin = rope_tables(positions)
    for i in range(DEPTH):
        j = i // 2
        if i % 2 == 0:
            mix = ssd_swa_mixer(x, cos, sin, ev_w_in[j], ev_conv_w[j], ev_conv_b[j], ev_dt_bias[j],
                                ev_a_log[j], ev_d_skip[j], ev_ssm_norm[j], ev_sinks[j], ev_w_out[j])
        else:
            mix = dsa_mla_mixer(x, cos, sin, od_w_in[j], od_kv_norm[j], od_w_uk[j], od_w_uv[j], od_w_out[j])
        x = layer_norm(DEEPNORM_ALPHA * x + mix, ln1_g[i], ln1_b[i])
        ffn = hier_moe(x, moe_router_group[i], moe_router_group_b[i], moe_router_expert[i],
                       moe_router_expert_b[i], moe_w_gate[i], moe_w_up[i], moe_w_down[i])
        x = layer_norm(DEEPNORM_ALPHA * x + ffn, ln2_g[i], ln2_b[i])
        x = x + jax.nn.sigmoid(x @ ple_w_gate[i] + ple_b_gate[i]) * (p[i] @ ple_w_proj[i])
    return x
```

```python
import functools

import jax
import jax.numpy as jnp
import numpy as np
from jax import lax
from jax.experimental import pallas as pl
from jax.experimental.pallas import tpu as pltpu

F32 = jnp.float32
BF16 = jnp.bfloat16
I32 = jnp.int32

D_MODEL = 2048
DEPTH = 4
ROPE_THETA = 10000.0
ROPE_DIM = 64
NORM_EPS = 1e-5
SSM_HEADS = 32
SSM_HEAD_DIM = 64
SSM_INNER = SSM_HEADS * SSM_HEAD_DIM
SSM_GROUPS = 4
SSM_STATE = 128
SSM_CONV = 4
SSM_CHUNK = 128
SWA_Q_HEADS = 16
SWA_KV_HEADS = 2
ATTN_BLOCK = 128
MLA_HEADS = 16
MLA_NOPE = 128
MLA_ROPE = ROPE_DIM
MLA_V = 128
MLA_RANK = 512
MLA_SCALE = (MLA_NOPE + MLA_ROPE) ** -0.5
IDX_HEADS = 16
IDX_DIM = ROPE_DIM
IDX_TOPK_MAX = 256
MOE_GROUPS = 4
MOE_EPG = 8
MOE_EXPERTS = MOE_GROUPS * MOE_EPG
MOE_FF = 512
PLE_DIM = 256
DEEPNORM_ALPHA = (2 * DEPTH) ** 0.25

LANES = 128
SUBLANES = 8
VMEM_LIMIT_BYTES = 56 * 1024 * 1024

EXPERT_ROWS = 256
DSA_SEL_Q = 128
DSA_ATT_Q = 64
DSA_KC = 256
MASK_NEG = -1e30

EV_Z, EV_XS, EV_BC, EV_Q, EV_KV, EV_DT = 0, 2048, 4096, 5120, 6144, 6400
EV_NP = 6656
OD_QN, OD_QR, OD_QI, OD_CKV, OD_KK, OD_WI = 0, 2048, 3072, 4096, 4608, 4736
OD_NP = 5120


def _cparams(sem, vmem=VMEM_LIMIT_BYTES):
    return pltpu.CompilerParams(dimension_semantics=sem, vmem_limit_bytes=vmem)


def _dot(a, b):
    return jnp.dot(a, b, preferred_element_type=F32)


def _dot_nt(a, b):
    return lax.dot_general(a, b, (((1,), (1,)), ((), ())), preferred_element_type=F32)


def _split3(v):
    hi = v.astype(BF16)
    r = v - hi.astype(F32)
    mid = r.astype(BF16)
    lo = (r - mid.astype(F32)).astype(BF16)
    return hi, mid, lo


def _expand(v, e):
    hi, mid, lo = _split3(v)
    return _dot(hi, e) + _dot(mid, e) + _dot(lo, e)


def _silu(v):
    return v * jax.nn.sigmoid(v)


def _layer_norm(v, g, b):
    mu = jnp.mean(v, axis=-1, keepdims=True)
    vc = v - mu
    var = jnp.mean(vc * vc, axis=-1, keepdims=True)
    return vc * lax.rsqrt(var + NORM_EPS) * g + b


def _rope_tile(t, c, s):
    lane = lax.broadcasted_iota(I32, t.shape, 1)
    first_half = (lane & 32) == 0
    swapped = jnp.where(first_half, pltpu.roll(t, LANES - 32, 1), pltpu.roll(t, 32, 1))
    return t * c + swapped * s


def _inproj_kernel(x_ref, w_ref, o_ref):
    o_ref[...] = _dot(x_ref[...].astype(BF16), w_ref[...])


def _inproj(x, w):
    m, k = x.shape
    n = w.shape[1]
    tm, tn = min(1024, m), 512
    return pl.pallas_call(
        _inproj_kernel,
        grid=(m // tm, n // tn),
        in_specs=[pl.BlockSpec((tm, k), lambda i, j: (i, 0)), pl.BlockSpec((k, tn), lambda i, j: (0, j))],
        out_specs=pl.BlockSpec((tm, tn), lambda i, j: (i, j)),
        out_shape=jax.ShapeDtypeStruct((m, n), F32),
        compiler_params=_cparams(("parallel", "arbitrary")),
        name="inproj",
    )(x, w)


def _even_out_kernel(a1_ref, a2_ref, w1_ref, w2_ref, x_ref, g_ref, b_ref, o_ref):
    mix = _dot(a1_ref[...], w1_ref[...]) + _dot(a2_ref[...], w2_ref[...])
    o_ref[...] = _layer_norm(DEEPNORM_ALPHA * x_ref[...] + mix, g_ref[...], b_ref[...])


def _even_out(y_ssm, y_att, w1, w2, x, g, b):
    m = x.shape[0]
    tm = min(256, m)
    row = lambda i: (i, 0)
    fix = lambda i: (0, 0)
    return pl.pallas_call(
        _even_out_kernel,
        grid=(m // tm,),
        in_specs=[
            pl.BlockSpec((tm, y_ssm.shape[1]), row),
            pl.BlockSpec((tm, y_att.shape[1]), row),
            pl.BlockSpec(w1.shape, fix),
            pl.BlockSpec(w2.shape, fix),
            pl.BlockSpec((tm, D_MODEL), row),
            pl.BlockSpec((1, D_MODEL), fix),
            pl.BlockSpec((1, D_MODEL), fix),
        ],
        out_specs=pl.BlockSpec((tm, D_MODEL), row),
        out_shape=jax.ShapeDtypeStruct((m, D_MODEL), F32),
        compiler_params=_cparams(("parallel",)),
        name="even_out_ln",
    )(y_ssm, y_att, w1, w2, x, g, b)


def _ple_kernel(x_ref, p_ref, wg_ref, bg_ref, wp_ref, o_ref):
    x = x_ref[...]
    gate = jax.nn.sigmoid(_dot(x.astype(BF16), wg_ref[...]) + bg_ref[...])
    o_ref[...] = x + gate * _dot(p_ref[...].astype(BF16), wp_ref[...])


def _ple(x, p, wg, bg, wp):
    m = x.shape[0]
    tm = min(256, m)
    row = lambda i: (i, 0)
    fix = lambda i: (0, 0)
    return pl.pallas_call(
        _ple_kernel,
        grid=(m // tm,),
        in_specs=[
            pl.BlockSpec((tm, D_MODEL), row),
            pl.BlockSpec((tm, PLE_DIM), row),
            pl.BlockSpec(wg.shape, fix),
            pl.BlockSpec((1, D_MODEL), fix),
            pl.BlockSpec(wp.shape, fix),
        ],
        out_specs=pl.BlockSpec((tm, D_MODEL), row),
        out_shape=jax.ShapeDtypeStruct((m, D_MODEL), F32),
        compiler_params=_cparams(("parallel",)),
        name="ple_gate",
    )(x, p, wg, bg, wp)


def _ssd_kernel(z_ref, xs_ref, bc_ref, dt_ref, cwx_ref, cbx_ref, cwb_ref, cbb_ref, dtb_ref, alog_ref,
                dsk_ref, nrm_ref, e_ref, y_ref, xs_ext, bc_ext, st_ref):
    c = pl.program_id(0)
    L = SSM_CHUNK
    halo = SUBLANES

    @pl.when(c == 0)
    def _():
        xs_ext[0:halo, :] = jnp.zeros((halo, xs_ext.shape[1]), F32)
        bc_ext[0:halo, :] = jnp.zeros((halo, bc_ext.shape[1]), F32)
        st_ref[...] = jnp.zeros(st_ref.shape, F32)

    xs_ext[halo:halo + L, :] = xs_ref[...]
    bc_ext[halo:halo + L, :] = bc_ref[...]

    def conv(ext, w_ref, b_ref):
        acc = b_ref[...]
        for j in range(SSM_CONV):
            lo = halo - (SSM_CONV - 1) + j
            acc = acc + ext[lo:lo + L, :] * w_ref[j:j + 1, :]
        return acc

    xs = _silu(conv(xs_ext, cwx_ref, cbx_ref))
    bc = _silu(conv(bc_ext, cwb_ref, cbb_ref))
    xs_ext[0:halo, :] = xs_ext[L:L + halo, :]
    bc_ext[0:halo, :] = bc_ext[L:L + halo, :]

    pre = dt_ref[...] + dtb_ref[...]
    dt = jnp.maximum(pre, 0.0) + jnp.log1p(jnp.exp(-jnp.abs(pre)))
    a = dt * (-jnp.exp(alog_ref[...]))
    row = lax.broadcasted_iota(I32, (L, LANES), 0)
    acs = a
    s = 1
    while s < L:
        acs = acs + jnp.where(row >= s, pltpu.roll(acs, s, 0), 0.0)
        s *= 2
    a_last = acs[L - 1:L, :]
    e = e_ref[...]
    dt_x = _expand(dt, e)
    dte_x = _expand(dt * jnp.exp(a_last - acs), e)
    eacs_x = _expand(jnp.exp(acs), e)
    cd_x = _expand(jnp.broadcast_to(jnp.exp(a_last), (SUBLANES, LANES)), e)[0:1, :]
    acs_t = acs.T

    xdt = (xs * dt_x).astype(BF16)
    xd = (xs * dte_x).astype(BF16)
    tri = lax.broadcasted_iota(I32, (L, L), 0) >= lax.broadcasted_iota(I32, (L, L), 1)
    first_head = lax.broadcasted_iota(I32, (L, LANES), 1) < SSM_HEAD_DIM
    n_state = SSM_STATE
    gw = SSM_INNER // SSM_GROUPS
    ys = []
    for g in range(SSM_GROUPS):
        bg = bc[:, g * n_state:(g + 1) * n_state]
        cg = bc[:, SSM_GROUPS * n_state + g * n_state:SSM_GROUPS * n_state + (g + 1) * n_state]
        bb, cb16 = bg.astype(BF16), cg.astype(BF16)
        cbm = _dot_nt(cb16, bb)
        st = st_ref[g]
        y_off = _dot(cb16, st.astype(BF16)) * eacs_x[:, g * gw:(g + 1) * gw]
        st_ref[g] = st * cd_x[:, g * gw:(g + 1) * gw] + _dot(bg.T.astype(BF16), xd[:, g * gw:(g + 1) * gw])
        parts = []
        for j in range(gw // LANES):
            lo = g * gw + j * LANES
            xp = xdt[:, lo:lo + LANES]
            out = None
            for par in (0, 1):
                h = lo // SSM_HEAD_DIM + par
                seg = acs[:, h:h + 1] - acs_t[h:h + 1, :]
                lm = (jnp.exp(jnp.where(tri, seg, -jnp.inf)) * cbm).astype(BF16)
                xm = jnp.where(first_head if par == 0 else jnp.logical_not(first_head), xp, jnp.zeros_like(xp))
                d = _dot(lm, xm)
                out = d if out is None else out + d
            parts.append(out)
        ys.append(jnp.concatenate(parts, axis=1) + y_off)
    y = jnp.concatenate(ys, axis=1) + xs * dsk_ref[...]
    y = y * _silu(z_ref[...])
    outs = []
    for g in range(SSM_GROUPS):
        yg = y[:, g * gw:(g + 1) * gw]
        ms = jnp.mean(yg * yg, axis=-1, keepdims=True)
        outs.append(yg * lax.rsqrt(ms + NORM_EPS))
    y_ref[...] = (jnp.concatenate(outs, axis=1) * nrm_ref[...]).astype(BF16)


def _ssd(xp, cw, cb, dtb, alog, dsk, nrm, e):
    s = xp.shape[0]
    L = SSM_CHUNK
    bcw = 2 * SSM_GROUPS * SSM_STATE
    cwx, cwb = cw[:, :SSM_INNER], cw[:, SSM_INNER:]
    cbx, cbb = cb[:, :SSM_INNER], cb[:, SSM_INNER:]
    fix = lambda i: (0, 0)
    return pl.pallas_call(
        _ssd_kernel,
        grid=(s // L,),
        in_specs=[
            pl.BlockSpec((L, SSM_INNER), lambda i: (i, EV_Z // SSM_INNER)),
            pl.BlockSpec((L, SSM_INNER), lambda i: (i, EV_XS // SSM_INNER)),
            pl.BlockSpec((L, bcw), lambda i: (i, EV_BC // bcw)),
            pl.BlockSpec((L, LANES), lambda i: (i, EV_DT // LANES)),
            pl.BlockSpec(cwx.shape, fix), pl.BlockSpec(cbx.shape, fix),
            pl.BlockSpec(cwb.shape, fix), pl.BlockSpec(cbb.shape, fix),
            pl.BlockSpec((1, LANES), fix), pl.BlockSpec((1, LANES), fix),
            pl.BlockSpec((1, SSM_INNER), fix), pl.BlockSpec((1, SSM_INNER), fix),
            pl.BlockSpec(e.shape, fix),
        ],
        out_specs=pl.BlockSpec((L, SSM_INNER), lambda i: (i, 0)),
        out_shape=jax.ShapeDtypeStruct((s, SSM_INNER), BF16),
        scratch_shapes=[
            pltpu.VMEM((L + 2 * SUBLANES, SSM_INNER), F32),
            pltpu.VMEM((L + 2 * SUBLANES, bcw), F32),
            pltpu.VMEM((SSM_GROUPS, SSM_STATE, SSM_INNER // SSM_GROUPS), F32),
        ],
        compiler_params=_cparams(("arbitrary",)),
        name="ssd_scan",
    )(xp, xp, xp, xp, cwx, cbx, cwb, cbb, dtb, alog, dsk, nrm, e)


def _swa_kernel(sink_ref, q_ref, kvc_ref, kvp_ref, cq_ref, sq_ref, cp_ref, sp_ref, o_ref):
    i = pl.program_id(0)
    B = ATTN_BLOCK
    lane = lax.broadcasted_iota(I32, (B, LANES), 1)
    lo_half = lane < ROPE_DIM
    cq, sq = cq_ref[...], sq_ref[...]
    kc = _rope_tile(kvc_ref[:, 0:LANES], cq, sq)
    kp = _rope_tile(kvp_ref[:, 0:LANES], cp_ref[...], sp_ref[...])
    kcat = jnp.concatenate([kp, kc], axis=0)
    kmat = (kcat.astype(BF16), pltpu.roll(kcat, ROPE_DIM, 1).astype(BF16))
    vcat = jnp.concatenate([kvp_ref[:, LANES:2 * LANES], kvc_ref[:, LANES:2 * LANES]], axis=0)
    vrol = pltpu.roll(vcat, ROPE_DIM, 1)
    lane2 = lax.broadcasted_iota(I32, (2 * B, LANES), 1) < ROPE_DIM
    vdup = (jnp.where(lane2, vcat, vrol).astype(BF16), jnp.where(lane2, vrol, vcat).astype(BF16))
    r = lax.broadcasted_iota(I32, (B, 2 * B), 0)
    col = lax.broadcasted_iota(I32, (B, 2 * B), 1)
    mask = (col > r) & (col <= r + B) & ((i > 0) | (col >= B))
    scale = ROPE_DIM ** -0.5
    hpg = SWA_Q_HEADS // SWA_KV_HEADS
    for j in range(SWA_Q_HEADS // 2):
        g = (2 * j) // hpg
        qt = _rope_tile(q_ref[:, j * LANES:(j + 1) * LANES], cq, sq)
        outs = []
        for par in (0, 1):
            h = 2 * j + par
            qm = jnp.where(lo_half if par == 0 else jnp.logical_not(lo_half), qt, 0.0).astype(BF16)
            logit = _dot_nt(qm, kmat[0] if par == g else kmat[1]) * scale
            logit = jnp.where(mask, logit, -jnp.inf)
            sink = sink_ref[h]
            m = jnp.maximum(jnp.max(logit, axis=-1, keepdims=True), sink)
            ex = jnp.exp(logit - m)
            prob = ex / (jnp.sum(ex, axis=-1, keepdims=True) + jnp.exp(sink - m))
            outs.append(_dot(prob.astype(BF16), vdup[g]))
        o_ref[:, j * LANES:(j + 1) * LANES] = jnp.where(lo_half, outs[0], outs[1]).astype(BF16)


def _swa(xp, sinks, cos_t, sin_t):
    s = xp.shape[0]
    B = ATTN_BLOCK
    qw = SWA_Q_HEADS * ROPE_DIM
    kvw = 2 * SWA_KV_HEADS * ROPE_DIM
    prev = lambda i: (jnp.maximum(i - 1, 0), 0)
    cur = lambda i: (i, 0)
    return pl.pallas_call(
        _swa_kernel,
        grid=(s // B,),
        in_specs=[
            pl.BlockSpec(memory_space=pltpu.SMEM),
            pl.BlockSpec((B, qw), lambda i: (i, EV_Q // qw)),
            pl.BlockSpec((B, kvw), lambda i: (i, EV_KV // kvw)),
            pl.BlockSpec((B, kvw), lambda i: (jnp.maximum(i - 1, 0), EV_KV // kvw)),
            pl.BlockSpec((B, LANES), cur), pl.BlockSpec((B, LANES), cur),
            pl.BlockSpec((B, LANES), prev), pl.BlockSpec((B, LANES), prev),
        ],
        out_specs=pl.BlockSpec((B, qw), cur),
        out_shape=jax.ShapeDtypeStruct((s, qw), BF16),
        compiler_params=_cparams(("parallel",)),
        name="swa_sink",
    )(sinks, xp, xp, xp, cos_t, sin_t, cos_t, sin_t)


def _dsa_prep_kernel(qn_ref, qr_ref, qi_ref, ckv_ref, kk_ref, wi_ref, wuk_ref, kvn_ref, c_ref, s_ref,
                     qlat_ref, qrope_ref, qidx_ref, ckvn_ref, kr_ref, ki_ref, wis_ref):
    c, s = c_ref[...], s_ref[...]
    lane = lax.broadcasted_iota(I32, c.shape, 1)
    lo_half = lane < ROPE_DIM
    for h in range(MLA_HEADS):
        qn = qn_ref[:, h * MLA_NOPE:(h + 1) * MLA_NOPE].astype(BF16)
        qlat_ref[h] = _dot(qn, wuk_ref[h]).astype(BF16)
    for src, dst in ((qr_ref, qrope_ref), (qi_ref, qidx_ref)):
        for j in range(MLA_HEADS // 2):
            t = _rope_tile(src[:, j * LANES:(j + 1) * LANES], c, s)
            dst[2 * j] = jnp.where(lo_half, t, 0.0).astype(BF16)
            dst[2 * j + 1] = jnp.where(lo_half, pltpu.roll(t, ROPE_DIM, 1), 0.0).astype(BF16)
    kk = _rope_tile(kk_ref[...], c, s)
    kr_ref[...] = jnp.where(lo_half, kk, 0.0).astype(BF16)
    ki_ref[...] = jnp.where(lo_half, pltpu.roll(kk, ROPE_DIM, 1), 0.0).astype(BF16)
    ckv = ckv_ref[...]
    ms = jnp.mean(ckv * ckv, axis=-1, keepdims=True)
    ckvn_ref[...] = (ckv * lax.rsqrt(ms + NORM_EPS) * kvn_ref[...]).astype(BF16)
    wis_ref[...] = wi_ref[...] * (IDX_HEADS ** -0.5 * IDX_DIM ** -0.5)


def _dsa_prep(xp, wuk, kvn, cos_t, sin_t):
    s = xp.shape[0]
    tm = min(256, s)
    H = MLA_HEADS
    fix2 = lambda i: (0, 0)
    hrow = lambda i: (0, i, 0)
    row = lambda i: (i, 0)
    return pl.pallas_call(
        _dsa_prep_kernel,
        grid=(s // tm,),
        in_specs=[
            pl.BlockSpec((tm, 2048), lambda i: (i, OD_QN // 2048)),
            pl.BlockSpec((tm, 1024), lambda i: (i, OD_QR // 1024)),
            pl.BlockSpec((tm, 1024), lambda i: (i, OD_QI // 1024)),
            pl.BlockSpec((tm, MLA_RANK), lambda i: (i, OD_CKV // MLA_RANK)),
            pl.BlockSpec((tm, LANES), lambda i: (i, OD_KK // LANES)),
            pl.BlockSpec((tm, LANES), lambda i: (i, OD_WI // LANES)),
            pl.BlockSpec(wuk.shape, lambda i: (0, 0, 0)),
            pl.BlockSpec((1, MLA_RANK), fix2),
            pl.BlockSpec((tm, LANES), row), pl.BlockSpec((tm, LANES), row),
        ],
        out_specs=[
            pl.BlockSpec((H, tm, MLA_RANK), hrow),
            pl.BlockSpec((H, tm, LANES), hrow),
            pl.BlockSpec((H, tm, LANES), hrow),
            pl.BlockSpec((tm, MLA_RANK), row),
            pl.BlockSpec((tm, LANES), row),
            pl.BlockSpec((tm, LANES), row),
            pl.BlockSpec((tm, LANES), row),
        ],
        out_shape=[
            jax.ShapeDtypeStruct((H, s, MLA_RANK), BF16),
            jax.ShapeDtypeStruct((H, s, LANES), BF16),
            jax.ShapeDtypeStruct((H, s, LANES), BF16),
            jax.ShapeDtypeStruct((s, MLA_RANK), BF16),
            jax.ShapeDtypeStruct((s, LANES), BF16),
            jax.ShapeDtypeStruct((s, LANES), BF16),
            jax.ShapeDtypeStruct((s, LANES), F32),
        ],
        compiler_params=_cparams(("parallel",)),
        name="dsa_prep",
    )(xp, xp, xp, xp, xp, xp, wuk, kvn, cos_t, sin_t)


def _dsa_select_kernel(topk, qi_ref, wi_ref, ki_ref, bias_ref, key_ref):
    i = pl.program_id(0)
    Q, KC = DSA_SEL_Q, DSA_KC
    H = IDX_HEADS
    n_chunks = bias_ref.shape[0]
    n_vis = ((i + 1) * Q + KC - 1) // KC
    qi = qi_ref[...].reshape(H * Q, LANES)
    wi = wi_ref[...]
    wcols = [wi[:, h:h + 1] for h in range(H)]
    qpos = i * Q + lax.broadcasted_iota(I32, (Q, KC), 0)
    kloc = lax.broadcasted_iota(I32, (Q, KC), 1)

    def score_chunk(c, carry):
        k = ki_ref[pl.ds(pl.multiple_of(c * KC, KC), KC), :]
        sc = _dot_nt(qi, k)
        acc = jnp.zeros((Q, KC), F32)
        for h in range(H):
            acc = acc + jnp.maximum(sc[h * Q:(h + 1) * Q, :], 0.0) * wcols[h]
        acc = jnp.where(c * KC + kloc <= qpos, acc, -jnp.inf)
        bits = pltpu.bitcast(acc, I32)
        key_ref[c] = bits ^ ((bits >> 31) & jnp.int32(0x7FFFFFFF))
        return carry

    lax.fori_loop(0, n_vis, score_chunk, 0)

    def count_ge(cand):
        def body(c, cnt):
            ones = jnp.where(key_ref[c] >= cand, 1, 0)
            for b in range(KC // LANES):
                cnt = cnt + ones[:, b * LANES:(b + 1) * LANES]
            return cnt
        cnt = lax.fori_loop(0, n_vis, body, jnp.zeros((Q, LANES), I32))
        return jnp.sum(cnt.astype(F32), axis=-1, keepdims=True)

    int_min = jnp.int32(-2 ** 31)
    thr = jnp.where(count_ge(jnp.zeros((Q, 1), I32)) >= topk, 0, int_min).astype(I32)

    def bit_step(b, thr):
        cand = thr | (jnp.int32(1) << (30 - b))
        return jnp.where(count_ge(cand) >= topk, cand, thr)

    thr = lax.fori_loop(0, 31, bit_step, thr)

    def emit(c, carry):
        sel = (key_ref[c] >= thr) & (c * KC + kloc <= qpos)
        bias_ref[c] = jnp.where(sel, 0.0, MASK_NEG)
        return carry

    lax.fori_loop(0, n_vis, emit, 0)

    def fill(c, carry):
        bias_ref[c] = jnp.full((Q, KC), MASK_NEG, F32)
        return carry

    lax.fori_loop(n_vis, n_chunks, fill, 0)


def _dsa_select(qidx, wis, ki, topk):
    H, s, _ = qidx.shape
    Q, KC = DSA_SEL_Q, DSA_KC
    nch = s // KC
    return pl.pallas_call(
        functools.partial(_dsa_select_kernel, topk),
        grid=(s // Q,),
        in_specs=[
            pl.BlockSpec((H, Q, LANES), lambda i: (0, i, 0)),
            pl.BlockSpec((Q, LANES), lambda i: (i, 0)),
            pl.BlockSpec((s, LANES), lambda i: (0, 0)),
        ],
        out_specs=pl.BlockSpec((nch, Q, KC), lambda i: (0, i, 0)),
        out_shape=jax.ShapeDtypeStruct((nch, s, KC), F32),
        scratch_shapes=[pltpu.VMEM((nch, Q, KC), I32)],
        compiler_params=_cparams(("parallel",)),
        name="dsa_select",
    )(qidx, wis, ki)


def _dsa_attn_kernel(ql_ref, qr_ref, bias_ref, ckv_ref, kr_ref, o_ref, m_ref, l_ref, acc_ref):
    i = pl.program_id(0)
    Q, KC, H = DSA_ATT_Q, DSA_KC, MLA_HEADS
    n_vis = ((i + 1) * Q + KC - 1) // KC
    ql = ql_ref[...].reshape(H * Q, MLA_RANK)
    qr = qr_ref[...].reshape(H * Q, LANES)
    m_ref[...] = jnp.full(m_ref.shape, -jnp.inf, F32)
    l_ref[...] = jnp.zeros(l_ref.shape, F32)
    acc_ref[...] = jnp.zeros(acc_ref.shape, F32)

    def step(c, carry):
        off = pl.multiple_of(c * KC, KC)
        ck = ckv_ref[pl.ds(off, KC), :]
        logit = (_dot_nt(ql, ck) + _dot_nt(qr, kr_ref[pl.ds(off, KC), :])) * MLA_SCALE
        logit = (logit.reshape(H, Q, KC) + bias_ref[c][None]).reshape(H * Q, KC)
        m_old = m_ref[...]
        m_new = jnp.maximum(m_old, jnp.max(logit, axis=-1, keepdims=True))
        alpha = jnp.exp(m_old - m_new)
        p = jnp.exp(logit - m_new)
        l_ref[...] = alpha * l_ref[...] + jnp.sum(p, axis=-1, keepdims=True)
        acc_ref[...] = alpha * acc_ref[...] + _dot(p.astype(BF16), ck)
        m_ref[...] = m_new
        return carry

    lax.fori_loop(0, n_vis, step, 0)
    o_ref[...] = (acc_ref[...] / l_ref[...]).reshape(H, Q, MLA_RANK).astype(BF16)


def _dsa_attn(qlat, qrope, bias, ckvn, kr):
    H, s, _ = qlat.shape
    Q, KC = DSA_ATT_Q, DSA_KC
    nch = s // KC
    hrow = lambda i: (0, i, 0)
    fix = lambda i: (0, 0)
    return pl.pallas_call(
        _dsa_attn_kernel,
        grid=(s // Q,),
        in_specs=[
            pl.BlockSpec((H, Q, MLA_RANK), hrow),
            pl.BlockSpec((H, Q, LANES), hrow),
            pl.BlockSpec((nch, Q, KC), hrow),
            pl.BlockSpec((s, MLA_RANK), fix),
            pl.BlockSpec((s, LANES), fix),
        ],
        out_specs=pl.BlockSpec((H, Q, MLA_RANK), hrow),
        out_shape=jax.ShapeDtypeStruct((H, s, MLA_RANK), BF16),
        scratch_shapes=[
            pltpu.VMEM((H * Q, 1), F32),
            pltpu.VMEM((H * Q, 1), F32),
            pltpu.VMEM((H * Q, MLA_RANK), F32),
        ],
        compiler_params=_cparams(("parallel",)),
        name="dsa_attn",
    )(qlat, qrope, bias, ckvn, kr)


def _odd_out_kernel(ol_ref, wuv_ref, wo_ref, x_ref, g_ref, b_ref, o_ref, u_ref):
    for h in range(MLA_HEADS):
        u_ref[:, h * MLA_V:(h + 1) * MLA_V] = _dot(ol_ref[h], wuv_ref[h]).astype(BF16)
    mix = _dot(u_ref[...], wo_ref[...])
    o_ref[...] = _layer_norm(DEEPNORM_ALPHA * x_ref[...] + mix, g_ref[...], b_ref[...])


def _odd_out(olat, wuv, wo, x, g, b):
    H, s, _ = olat.shape
    tm = min(256, s)
    row = lambda i: (i, 0)
    fix = lambda i: (0, 0)
    return pl.pallas_call(
        _odd_out_kernel,
        grid=(s // tm,),
        in_specs=[
            pl.BlockSpec((H, tm, MLA_RANK), lambda i: (0, i, 0)),
            pl.BlockSpec(wuv.shape, lambda i: (0, 0, 0)),
            pl.BlockSpec(wo.shape, fix),
            pl.BlockSpec((tm, D_MODEL), row),
            pl.BlockSpec((1, D_MODEL), fix), pl.BlockSpec((1, D_MODEL), fix),
        ],
        out_specs=pl.BlockSpec((tm, D_MODEL), row),
        out_shape=jax.ShapeDtypeStruct((s, D_MODEL), F32),
        scratch_shapes=[pltpu.VMEM((tm, MLA_HEADS * MLA_V), BF16)],
        compiler_params=_cparams(("parallel",)),
        name="odd_out_ln",
    )(olat, wuv, wo, x, g, b)


ROUTE_LANE0 = MOE_GROUPS


def _router_kernel(h_ref, wr_ref, br_ref, info_ref, cnt_ref, run_ref):
    i = pl.program_id(0)

    @pl.when(i == 0)
    def _():
        run_ref[...] = jnp.zeros(run_ref.shape, F32)

    h = h_ref[...]
    tm = h.shape[0]
    h_hi = h.astype(BF16)
    h_lo = (h - h_hi.astype(F32)).astype(BF16)
    w = wr_ref[...]
    w_hi = w.astype(BF16)
    w_lo = (w - w_hi.astype(F32)).astype(BF16)
    logits = _dot(h_hi, w_hi) + (_dot(h_hi, w_lo) + _dot(h_lo, w_hi)) + br_ref[...]
    lane = lax.broadcasted_iota(I32, (tm, LANES), 1)
    lane_f = lane.astype(F32)
    neg = -jnp.inf
    big = float(LANES)
    is_grp = lane < MOE_GROUPS
    gl = jnp.where(is_grp, logits, neg)
    gmax = jnp.max(gl, axis=-1, keepdims=True)
    gsel = jnp.min(jnp.where(gl == gmax, lane_f, big), axis=-1, keepdims=True)
    gsum = jnp.sum(jnp.where(is_grp, jnp.exp(logits - gmax), 0.0), axis=-1, keepdims=True)
    egrp = ((lane - ROUTE_LANE0) >> 3).astype(F32)
    valid = (lane >= ROUTE_LANE0) & (lane < ROUTE_LANE0 + MOE_EXPERTS) & (egrp == gsel)
    el = jnp.where(valid, logits, neg)
    v1 = jnp.max(el, axis=-1, keepdims=True)
    i1 = jnp.min(jnp.where(el == v1, lane_f, big), axis=-1, keepdims=True)
    el2 = jnp.where(lane_f == i1, neg, el)
    v2 = jnp.max(el2, axis=-1, keepdims=True)
    i2 = jnp.min(jnp.where(el2 == v2, lane_f, big), axis=-1, keepdims=True)
    t = jnp.exp(v2 - v1)
    p1 = 1.0 / (1.0 + t)
    p2 = t / (1.0 + t)
    ggate = 1.0 / gsum
    m1 = lane_f == i1
    m2 = lane_f == i2
    memb = jnp.where(m1 | m2, 1.0, 0.0)
    tri = (lax.broadcasted_iota(I32, (tm, tm), 0) > lax.broadcasted_iota(I32, (tm, tm), 1))
    cum = _dot(jnp.where(tri, 1.0, 0.0).astype(BF16), memb.astype(BF16)) + run_ref[...]
    rank1 = jnp.sum(jnp.where(m1, cum, 0.0), axis=-1, keepdims=True)
    rank2 = jnp.sum(jnp.where(m2, cum, 0.0), axis=-1, keepdims=True)
    run_ref[...] = run_ref[...] + jnp.sum(memb, axis=0, keepdims=True)
    info = jnp.where(lane == 0, i1 - ROUTE_LANE0, 0.0)
    info = jnp.where(lane == 1, i2 - ROUTE_LANE0, info)
    info = jnp.where(lane == 2, p1 * ggate, info)
    info = jnp.where(lane == 3, p2 * ggate, info)
    info = jnp.where(lane == 4, rank1, info)
    info = jnp.where(lane == 5, rank2, info)
    info_ref[...] = info
    cnt_ref[...] = run_ref[...]


def _router(h, wr, br):
    t = h.shape[0]
    tm = min(512, t)
    return pl.pallas_call(
        _router_kernel,
        grid=(t // tm,),
        in_specs=[
            pl.BlockSpec((tm, D_MODEL), lambda i: (i, 0)),
            pl.BlockSpec(wr.shape, lambda i: (0, 0)),
            pl.BlockSpec((1, LANES), lambda i: (0, 0)),
        ],
        out_specs=[pl.BlockSpec((tm, LANES), lambda i: (i, 0)), pl.BlockSpec((1, LANES), lambda i: (0, 0))],
        out_shape=[jax.ShapeDtypeStruct((t, LANES), F32), jax.ShapeDtypeStruct((1, LANES), F32)],
        scratch_shapes=[pltpu.VMEM((1, LANES), F32)],
        compiler_params=_cparams(("arbitrary",)),
        name="moe_router",
    )(h, wr, br)


def _plan_kernel(info_ref, ps_ref, pos_ref):
    info = info_ref[...]
    lane = lax.broadcasted_iota(I32, info.shape, 1)
    lane_f = lane.astype(F32)
    ps = ps_ref[...]
    pos1 = jnp.sum(jnp.where(lane_f == info[:, 0:1], ps, 0.0), axis=-1, keepdims=True) + info[:, 4:5]
    pos2 = jnp.sum(jnp.where(lane_f == info[:, 1:2], ps, 0.0), axis=-1, keepdims=True) + info[:, 5:6]
    pos_ref[...] = jnp.where(lane == 0, pos1, jnp.where(lane == 1, pos2, 0.0)).astype(I32)


def _plan(info, pad_start):
    t = info.shape[0]
    tm = min(1024, t)
    return pl.pallas_call(
        _plan_kernel,
        grid=(t // tm,),
        in_specs=[pl.BlockSpec((tm, LANES), lambda i: (i, 0)), pl.BlockSpec((1, LANES), lambda i: (0, 0))],
        out_specs=pl.BlockSpec((tm, LANES), lambda i: (i, 0)),
        out_shape=jax.ShapeDtypeStruct((t, LANES), I32),
        compiler_params=_cparams(("parallel",)),
        name="moe_plan",
    )(info, pad_start)


DISPATCH_TOKENS = 64


def _dispatch_kernel(pos_ref, h_ref, xin_in_ref, xin_ref, sem):
    del xin_in_ref
    n_batches = h_ref.shape[0] // DISPATCH_TOKENS

    def row_copy(t, dst_row, slot):
        return pltpu.make_async_copy(h_ref.at[pl.ds(t, 1)], xin_ref.at[pl.ds(dst_row, 1)], sem.at[slot])

    def issue(b, slot):
        def body(tt, carry):
            t = b * DISPATCH_TOKENS + tt
            row_copy(t, pos_ref[2 * t], slot).start()
            row_copy(t, pos_ref[2 * t + 1], slot).start()
            return carry
        lax.fori_loop(0, DISPATCH_TOKENS, body, 0)

    def drain(slot):
        def body(tt, carry):
            row_copy(0, 0, slot).wait()
            row_copy(0, 0, slot).wait()
            return carry
        lax.fori_loop(0, DISPATCH_TOKENS, body, 0)

    issue(0, 0)

    def outer(b, carry):
        issue(b, b % 2)
        drain((b - 1) % 2)
        return carry

    lax.fori_loop(1, n_batches, outer, 0)
    drain((n_batches - 1) % 2)


def _dispatch(pos_flat, h, xin):
    return pl.pallas_call(
        _dispatch_kernel,
        grid_spec=pltpu.PrefetchScalarGridSpec(
            num_scalar_prefetch=1,
            grid=(1,),
            in_specs=[pl.BlockSpec(memory_space=pl.ANY), pl.BlockSpec(memory_space=pl.ANY)],
            out_specs=pl.BlockSpec(memory_space=pl.ANY),
            scratch_shapes=[pltpu.SemaphoreType.DMA((2,))],
        ),
        out_shape=jax.ShapeDtypeStruct(xin.shape, xin.dtype),
        input_output_aliases={2: 0},
        compiler_params=_cparams(("arbitrary",)),
        name="moe_dispatch",
    )(pos_flat, h, xin)


def _expert_kernel(be_ref, nu_ref, x_ref, wg_ref, wu_ref, wd_ref, o_ref):
    del be_ref
    b = pl.program_id(0)

    @pl.when(b < nu_ref[0])
    def _():
        x = x_ref[...].astype(BF16)
        hid = _silu(_dot(x, wg_ref[0].astype(BF16))) * _dot(x, wu_ref[0].astype(BF16))
        o_ref[...] = _dot(hid.astype(BF16), wd_ref[0].astype(BF16))

    @pl.when(b >= nu_ref[0])
    def _():
        o_ref[...] = jnp.zeros(o_ref.shape, F32)


def _experts(blk_e, n_used, xin, wg, wu, wd):
    r = xin.shape[0]
    nb = r // EXPERT_ROWS
    xmap = lambda b, be, nu: (jnp.minimum(b, nu[0] - 1), 0)
    wmap = lambda b, be, nu: (be[b], 0, 0)
    return pl.pallas_call(
        _expert_kernel,
        grid_spec=pltpu.PrefetchScalarGridSpec(
            num_scalar_prefetch=2,
            grid=(nb,),
            in_specs=[
                pl.BlockSpec((EXPERT_ROWS, D_MODEL), xmap),
                pl.BlockSpec((1, D_MODEL, MOE_FF), wmap),
                pl.BlockSpec((1, D_MODEL, MOE_FF), wmap),
                pl.BlockSpec((1, MOE_FF, D_MODEL), wmap),
            ],
            out_specs=pl.BlockSpec((EXPERT_ROWS, D_MODEL), lambda b, be, nu: (b, 0)),
        ),
        out_shape=jax.ShapeDtypeStruct((r, D_MODEL), F32),
        compiler_params=_cparams(("arbitrary",)),
        name="moe_experts",
    )(blk_e, n_used, xin, wg, wu, wd)


COMBINE_TOKENS = 128


def _combine_kernel(pos_ref, y_ref, info_ref, x_ref, g_ref, b_ref, o_ref, buf, sem):
    i = pl.program_id(0)
    n = pl.num_programs(0)
    TB = COMBINE_TOKENS

    def row_copy(src_row, k, tt, slot):
        return pltpu.make_async_copy(y_ref.at[pl.ds(src_row, 1)], buf.at[slot, k, pl.ds(tt, 1)], sem.at[slot])

    def issue(blk, slot):
        def body(tt, carry):
            t = blk * TB + tt
            row_copy(pos_ref[2 * t], 0, tt, slot).start()
            row_copy(pos_ref[2 * t + 1], 1, tt, slot).start()
            return carry
        lax.fori_loop(0, TB, body, 0)

    def drain(slot):
        def body(tt, carry):
            row_copy(0, 0, 0, slot).wait()
            row_copy(0, 1, 0, slot).wait()
            return carry
        lax.fori_loop(0, TB, body, 0)

    slot = i % 2

    @pl.when(i == 0)
    def _():
        issue(0, 0)

    drain(slot)

    @pl.when(i + 1 < n)
    def _():
        issue(i + 1, 1 - slot)

    info = info_ref[...]
    y = buf[slot, 0] * info[:, 2:3] + buf[slot, 1] * info[:, 3:4]
    o_ref[...] = _layer_norm(DEEPNORM_ALPHA * x_ref[...] + y, g_ref[...], b_ref[...])


def _combine(pos_flat, y_rows, info, x, g, b):
    t = x.shape[0]
    TB = min(COMBINE_TOKENS, t)
    row = lambda i, pos: (i, 0)
    fix = lambda i, pos: (0, 0)
    return pl.pallas_call(
        _combine_kernel,
        grid_spec=pltpu.PrefetchScalarGridSpec(
            num_scalar_prefetch=1,
            grid=(t // TB,),
            in_specs=[
                pl.BlockSpec(memory_space=pl.ANY),
                pl.BlockSpec((TB, LANES), row),
                pl.BlockSpec((TB, D_MODEL), row),
                pl.BlockSpec((1, D_MODEL), fix), pl.BlockSpec((1, D_MODEL), fix),
            ],
            out_specs=pl.BlockSpec((TB, D_MODEL), row),
            scratch_shapes=[pltpu.VMEM((2, 2, TB, D_MODEL), F32), pltpu.SemaphoreType.DMA((2,))],
        ),
        out_shape=jax.ShapeDtypeStruct((t, D_MODEL), F32),
        compiler_params=_cparams(("arbitrary",)),
        name="moe_combine_ln",
    )(pos_flat, y_rows, info, x, g, b)


def _hier_moe_ln(x, xin_buf, wr, br, wg, wu, wd, ln_g, ln_b):
    t = x.shape[0]
    info, cnt = _router(x, wr, br)
    counts = cnt[0, ROUTE_LANE0:ROUTE_LANE0 + MOE_EXPERTS].astype(I32)
    padded = (counts + EXPERT_ROWS - 1) // EXPERT_ROWS * EXPERT_ROWS
    pad_end = jnp.cumsum(padded)
    pad_start = jnp.zeros((1, LANES), F32).at[0, :MOE_EXPERTS].set((pad_end - padded).astype(F32))
    n_blocks = xin_buf.shape[0] // EXPERT_ROWS
    blk_start = jnp.arange(n_blocks, dtype=I32) * EXPERT_ROWS
    blk_e = jnp.minimum(jnp.sum(pad_end[None, :] <= blk_start[:, None], axis=1), MOE_EXPERTS - 1).astype(I32)
    n_used = (pad_end[-1:] // EXPERT_ROWS).astype(I32)
    pos = _plan(info, pad_start)[:, :2].reshape(-1)
    xin_buf = _dispatch(pos, x, xin_buf)
    y_rows = _experts(blk_e, n_used, xin_buf, wg, wu, wd)
    return _combine(pos, y_rows, info, x, ln_g, ln_b), xin_buf


def _rope_tables(positions):
    inv = ROPE_THETA ** (-jnp.arange(0, ROPE_DIM, 2, dtype=F32) / ROPE_DIM)
    ang = positions.astype(F32)[:, None] * inv
    cos, sin = jnp.cos(ang), jnp.sin(ang)
    cos_t = jnp.tile(cos, (1, LANES // (ROPE_DIM // 2)))
    sin_t = jnp.tile(jnp.concatenate([-sin, sin], axis=1), (1, LANES // ROPE_DIM))
    return cos_t, sin_t


def _pad_cols(w, n):
    return jnp.pad(w, ((0, 0), (0, n - w.shape[1])))


def _even_w_in(w):
    z, xbc, dt, q, kv = jnp.split(w, [2048, 5120, 5152, 6176], axis=1)
    return _pad_cols(jnp.concatenate([z, xbc, q, kv, dt], axis=1), EV_NP).astype(BF16)


def _odd_w_in(w):
    q, ckv, krope, qi, ki, wi = jnp.split(w, [3072, 3584, 3648, 4672, 4736], axis=1)
    q = q.reshape(D_MODEL, MLA_HEADS, MLA_NOPE + MLA_ROPE)
    qn = q[:, :, :MLA_NOPE].reshape(D_MODEL, -1)
    qr = q[:, :, MLA_NOPE:].reshape(D_MODEL, -1)
    return _pad_cols(jnp.concatenate([qn, qr, qi, ckv, krope, ki, _pad_cols(wi, LANES)], axis=1), OD_NP).astype(BF16)


def _head_expand_matrix():
    e = np.zeros((LANES, SSM_INNER), np.float32)
    for h in range(SSM_HEADS):
        e[h, h * SSM_HEAD_DIM:(h + 1) * SSM_HEAD_DIM] = 1.0
    return jnp.asarray(e, BF16)


def kernel(x, p, positions, ev_w_in, ev_conv_w, ev_conv_b, ev_dt_bias, ev_a_log, ev_d_skip, ev_ssm_norm, ev_sinks, ev_w_out, od_w_in, od_kv_norm, od_w_uk, od_w_uv, od_w_out, ln1_g, ln1_b, ln2_g, ln2_b, moe_router_group, moe_router_group_b, moe_router_expert, moe_router_expert_b, moe_w_gate, moe_w_up, moe_w_down, ple_w_proj, ple_w_gate, ple_b_gate):
    batch, s, d = x.shape
    assert batch == 1 and d == D_MODEL
    xs = x[0]
    cos_t, sin_t = _rope_tables(positions[0])
    e_mat = _head_expand_matrix()
    topk = min(IDX_TOPK_MAX, s // 4)
    xin_buf = jnp.zeros((2 * s + MOE_EXPERTS * EXPERT_ROWS, D_MODEL), F32)
    row = lambda v: v.reshape(1, -1)
    pad_row = lambda v: _pad_cols(v.reshape(1, -1), LANES)
    for i in range(DEPTH):
        j = i // 2
        if i % 2 == 0:
            xp = _inproj(xs, _even_w_in(ev_w_in[j]))
            y_ssm = _ssd(xp, ev_conv_w[j], row(ev_conv_b[j]), pad_row(ev_dt_bias[j]), pad_row(ev_a_log[j]),
                         row(jnp.repeat(ev_d_skip[j], SSM_HEAD_DIM)), row(ev_ssm_norm[j]), e_mat)
            y_att = _swa(xp, ev_sinks[j], cos_t, sin_t)
            w_out = ev_w_out[j].astype(BF16)
            xs = _even_out(y_ssm, y_att, w_out[:SSM_INNER], w_out[SSM_INNER:], xs, row(ln1_g[i]), row(ln1_b[i]))
        else:
            xp = _inproj(xs, _odd_w_in(od_w_in[j]))
            qlat, qrope, qidx, ckvn, kr, ki, wis = _dsa_prep(
                xp, od_w_uk[j].astype(BF16), row(od_kv_norm[j]), cos_t, sin_t)
            bias = _dsa_select(qidx, wis, ki, topk)
            olat = _dsa_attn(qlat, qrope, bias, ckvn, kr)
            xs = _odd_out(olat, od_w_uv[j].astype(BF16), od_w_out[j].astype(BF16), xs, row(ln1_g[i]), row(ln1_b[i]))
        wr = _pad_cols(jnp.concatenate([moe_router_group[i], moe_router_expert[i]], axis=1), LANES)
        br = pad_row(jnp.concatenate([moe_router_group_b[i], moe_router_expert_b[i]]))
        xs, xin_buf = _hier_moe_ln(xs, xin_buf, wr, br, moe_w_gate[i], moe_w_up[i], moe_w_down[i],
                                   row(ln2_g[i]), row(ln2_b[i]))
        xs = _ple(xs, p[i, 0], ple_w_gate[i].astype(BF16), row(ple_b_gate[i]), ple_w_proj[i].astype(BF16))
    return xs[None]
```

```python
import functools

import jax
import jax.numpy as jnp
import numpy as np
from jax import lax
from jax.experimental import pallas as pl
from jax.experimental.pallas import tpu as pltpu

F32 = jnp.float32
BF16 = jnp.bfloat16
I32 = jnp.int32

D_MODEL = 2048
DEPTH = 4
ROPE_THETA = 10000.0
ROPE_DIM = 64
NORM_EPS = 1e-5
SSM_HEADS = 32
SSM_HEAD_DIM = 64
SSM_INNER = SSM_HEADS * SSM_HEAD_DIM
SSM_GROUPS = 4
SSM_STATE = 128
SSM_CONV = 4
SSM_CHUNK = 128
SWA_Q_HEADS = 16
SWA_KV_HEADS = 2
ATTN_BLOCK = 128
MLA_HEADS = 16
MLA_NOPE = 128
MLA_ROPE = ROPE_DIM
MLA_V = 128
MLA_RANK = 512
MLA_SCALE = (MLA_NOPE + MLA_ROPE) ** -0.5
IDX_HEADS = 16
IDX_DIM = ROPE_DIM
IDX_TOPK_MAX = 256
MOE_GROUPS = 4
MOE_EPG = 8
MOE_EXPERTS = MOE_GROUPS * MOE_EPG
MOE_FF = 512
PLE_DIM = 256
DEEPNORM_ALPHA = (2 * DEPTH) ** 0.25

LANES = 128
SUBLANES = 8
VMEM_LIMIT_BYTES = 56 * 1024 * 1024

EXPERT_ROWS = 256
DSA_SEL_Q = 128
DSA_ATT_Q = 64
DSA_KC = 512
DSA_SEL_HEADS_PER_DOT = 4
MASK_NEG = -1e30

EV_Z, EV_XS, EV_BC, EV_Q, EV_KV, EV_DT = 0, 2048, 4096, 5120, 6144, 6400
EV_NP = 6656
OD_QN, OD_QR, OD_QI, OD_CKV, OD_KK, OD_WI = 0, 2048, 3072, 4096, 4608, 4736
OD_NP = 5120


def _cparams(sem, vmem=VMEM_LIMIT_BYTES):
    return pltpu.CompilerParams(dimension_semantics=sem, vmem_limit_bytes=vmem)


def _dot(a, b):
    return jnp.dot(a, b, preferred_element_type=F32)


def _dot_nt(a, b):
    return lax.dot_general(a, b, (((1,), (1,)), ((), ())), preferred_element_type=F32)


def _split3(v):
    hi = v.astype(BF16)
    r = v - hi.astype(F32)
    mid = r.astype(BF16)
    lo = (r - mid.astype(F32)).astype(BF16)
    return hi, mid, lo


def _expand(v, e):
    hi, mid, lo = _split3(v)
    return _dot(hi, e) + _dot(mid, e) + _dot(lo, e)


def _silu(v):
    return v * jax.nn.sigmoid(v)


def _layer_norm(v, g, b):
    mu = jnp.mean(v, axis=-1, keepdims=True)
    vc = v - mu
    var = jnp.mean(vc * vc, axis=-1, keepdims=True)
    return vc * lax.rsqrt(var + NORM_EPS) * g + b


def _rope_tile(t, c, s):
    lane = lax.broadcasted_iota(I32, t.shape, 1)
    first_half = (lane & 32) == 0
    swapped = jnp.where(first_half, pltpu.roll(t, LANES - 32, 1), pltpu.roll(t, 32, 1))
    return t * c + swapped * s


def _inproj_kernel(x_ref, w_ref, o_ref):
    o_ref[...] = _dot(x_ref[...].astype(BF16), w_ref[...])


def _inproj(x, w):
    m, k = x.shape
    n = w.shape[1]
    tm, tn = min(1024, m), 512
    return pl.pallas_call(
        _inproj_kernel,
        grid=(m // tm, n // tn),
        in_specs=[pl.BlockSpec((tm, k), lambda i, j: (i, 0)), pl.BlockSpec((k, tn), lambda i, j: (0, j))],
        out_specs=pl.BlockSpec((tm, tn), lambda i, j: (i, j)),
        out_shape=jax.ShapeDtypeStruct((m, n), F32),
        compiler_params=_cparams(("parallel", "arbitrary")),
        name="inproj",
    )(x, w)


def _even_out_kernel(a1_ref, a2_ref, w1_ref, w2_ref, x_ref, g_ref, b_ref, o_ref):
    mix = _dot(a1_ref[...], w1_ref[...]) + _dot(a2_ref[...], w2_ref[...])
    o_ref[...] = _layer_norm(DEEPNORM_ALPHA * x_ref[...] + mix, g_ref[...], b_ref[...])


def _even_out(y_ssm, y_att, w1, w2, x, g, b):
    m = x.shape[0]
    tm = min(256, m)
    row = lambda i: (i, 0)
    fix = lambda i: (0, 0)
    return pl.pallas_call(
        _even_out_kernel,
        grid=(m // tm,),
        in_specs=[
            pl.BlockSpec((tm, y_ssm.shape[1]), row),
            pl.BlockSpec((tm, y_att.shape[1]), row),
            pl.BlockSpec(w1.shape, fix),
            pl.BlockSpec(w2.shape, fix),
            pl.BlockSpec((tm, D_MODEL), row),
            pl.BlockSpec((1, D_MODEL), fix),
            pl.BlockSpec((1, D_MODEL), fix),
        ],
        out_specs=pl.BlockSpec((tm, D_MODEL), row),
        out_shape=jax.ShapeDtypeStruct((m, D_MODEL), F32),
        compiler_params=_cparams(("parallel",)),
        name="even_out_ln",
    )(y_ssm, y_att, w1, w2, x, g, b)


def _ple_kernel(x_ref, p_ref, wg_ref, bg_ref, wp_ref, o_ref):
    x = x_ref[...]
    gate = jax.nn.sigmoid(_dot(x.astype(BF16), wg_ref[...]) + bg_ref[...])
    o_ref[...] = x + gate * _dot(p_ref[...].astype(BF16), wp_ref[...])


def _ple(x, p, wg, bg, wp):
    m = x.shape[0]
    tm = min(256, m)
    row = lambda i: (i, 0)
    fix = lambda i: (0, 0)
    return pl.pallas_call(
        _ple_kernel,
        grid=(m // tm,),
        in_specs=[
            pl.BlockSpec((tm, D_MODEL), row),
            pl.BlockSpec((tm, PLE_DIM), row),
            pl.BlockSpec(wg.shape, fix),
            pl.BlockSpec((1, D_MODEL), fix),
            pl.BlockSpec(wp.shape, fix),
        ],
        out_specs=pl.BlockSpec((tm, D_MODEL), row),
        out_shape=jax.ShapeDtypeStruct((m, D_MODEL), F32),
        compiler_params=_cparams(("parallel",)),
        name="ple_gate",
    )(x, p, wg, bg, wp)


def _ssd_kernel(z_ref, xs_ref, bc_ref, dt_ref, cwx_ref, cbx_ref, cwb_ref, cbb_ref, dtb_ref, alog_ref,
                dsk_ref, nrm_ref, e_ref, y_ref, xs_ext, bc_ext, st_ref):
    c = pl.program_id(0)
    L = SSM_CHUNK
    halo = SUBLANES

    @pl.when(c == 0)
    def _():
        xs_ext[0:halo, :] = jnp.zeros((halo, xs_ext.shape[1]), F32)
        bc_ext[0:halo, :] = jnp.zeros((halo, bc_ext.shape[1]), F32)
        st_ref[...] = jnp.zeros(st_ref.shape, F32)

    xs_ext[halo:halo + L, :] = xs_ref[...]
    bc_ext[halo:halo + L, :] = bc_ref[...]

    def conv(ext, w_ref, b_ref):
        acc = b_ref[...]
        for j in range(SSM_CONV):
            lo = halo - (SSM_CONV - 1) + j
            acc = acc + ext[lo:lo + L, :] * w_ref[j:j + 1, :]
        return acc

    xs = _silu(conv(xs_ext, cwx_ref, cbx_ref))
    bc = _silu(conv(bc_ext, cwb_ref, cbb_ref))
    xs_ext[0:halo, :] = xs_ext[L:L + halo, :]
    bc_ext[0:halo, :] = bc_ext[L:L + halo, :]

    pre = dt_ref[...] + dtb_ref[...]
    dt = jnp.maximum(pre, 0.0) + jnp.log1p(jnp.exp(-jnp.abs(pre)))
    a = dt * (-jnp.exp(alog_ref[...]))
    row = lax.broadcasted_iota(I32, (L, LANES), 0)
    acs = a
    s = 1
    while s < L:
        acs = acs + jnp.where(row >= s, pltpu.roll(acs, s, 0), 0.0)
        s *= 2
    a_last = acs[L - 1:L, :]
    e = e_ref[...]
    dt_x = _expand(dt, e)
    dte_x = _expand(dt * jnp.exp(a_last - acs), e)
    eacs_x = _expand(jnp.exp(acs), e)
    cd_x = _expand(jnp.broadcast_to(jnp.exp(a_last), (SUBLANES, LANES)), e)[0:1, :]
    acs_t = acs.T

    xdt = (xs * dt_x).astype(BF16)
    xd = (xs * dte_x).astype(BF16)
    tri = lax.broadcasted_iota(I32, (L, L), 0) >= lax.broadcasted_iota(I32, (L, L), 1)
    first_head = lax.broadcasted_iota(I32, (L, LANES), 1) < SSM_HEAD_DIM
    n_state = SSM_STATE
    gw = SSM_INNER // SSM_GROUPS
    ys = []
    for g in range(SSM_GROUPS):
        bg = bc[:, g * n_state:(g + 1) * n_state]
        cg = bc[:, SSM_GROUPS * n_state + g * n_state:SSM_GROUPS * n_state + (g + 1) * n_state]
        bb, cb16 = bg.astype(BF16), cg.astype(BF16)
        cbm = _dot_nt(cb16, bb)
        st = st_ref[g]
        y_off = _dot(cb16, st.astype(BF16)) * eacs_x[:, g * gw:(g + 1) * gw]
        st_ref[g] = st * cd_x[:, g * gw:(g + 1) * gw] + _dot(bg.T.astype(BF16), xd[:, g * gw:(g + 1) * gw])
        parts = []
        for j in range(gw // LANES):
            lo = g * gw + j * LANES
            xp = xdt[:, lo:lo + LANES]
            out = None
            for par in (0, 1):
                h = lo // SSM_HEAD_DIM + par
                seg = acs[:, h:h + 1] - acs_t[h:h + 1, :]
                lm = (jnp.exp(jnp.where(tri, seg, -jnp.inf)) * cbm).astype(BF16)
                xm = jnp.where(first_head if par == 0 else jnp.logical_not(first_head), xp, jnp.zeros_like(xp))
                d = _dot(lm, xm)
                out = d if out is None else out + d
            parts.append(out)
        ys.append(jnp.concatenate(parts, axis=1) + y_off)
    y = jnp.concatenate(ys, axis=1) + xs * dsk_ref[...]
    y = y * _silu(z_ref[...])
    outs = []
    for g in range(SSM_GROUPS):
        yg = y[:, g * gw:(g + 1) * gw]
        ms = jnp.mean(yg * yg, axis=-1, keepdims=True)
        outs.append(yg * lax.rsqrt(ms + NORM_EPS))
    y_ref[...] = (jnp.concatenate(outs, axis=1) * nrm_ref[...]).astype(BF16)


def _ssd(xp, cw, cb, dtb, alog, dsk, nrm, e):
    s = xp.shape[0]
    L = SSM_CHUNK
    bcw = 2 * SSM_GROUPS * SSM_STATE
    cwx, cwb = cw[:, :SSM_INNER], cw[:, SSM_INNER:]
    cbx, cbb = cb[:, :SSM_INNER], cb[:, SSM_INNER:]
    fix = lambda i: (0, 0)
    return pl.pallas_call(
        _ssd_kernel,
        grid=(s // L,),
        in_specs=[
            pl.BlockSpec((L, SSM_INNER), lambda i: (i, EV_Z // SSM_INNER)),
            pl.BlockSpec((L, SSM_INNER), lambda i: (i, EV_XS // SSM_INNER)),
            pl.BlockSpec((L, bcw), lambda i: (i, EV_BC // bcw)),
            pl.BlockSpec((L, LANES), lambda i: (i, EV_DT // LANES)),
            pl.BlockSpec(cwx.shape, fix), pl.BlockSpec(cbx.shape, fix),
            pl.BlockSpec(cwb.shape, fix), pl.BlockSpec(cbb.shape, fix),
            pl.BlockSpec((1, LANES), fix), pl.BlockSpec((1, LANES), fix),
            pl.BlockSpec((1, SSM_INNER), fix), pl.BlockSpec((1, SSM_INNER), fix),
            pl.BlockSpec(e.shape, fix),
        ],
        out_specs=pl.BlockSpec((L, SSM_INNER), lambda i: (i, 0)),
        out_shape=jax.ShapeDtypeStruct((s, SSM_INNER), BF16),
        scratch_shapes=[
            pltpu.VMEM((L + 2 * SUBLANES, SSM_INNER), F32),
            pltpu.VMEM((L + 2 * SUBLANES, bcw), F32),
            pltpu.VMEM((SSM_GROUPS, SSM_STATE, SSM_INNER // SSM_GROUPS), F32),
        ],
        compiler_params=_cparams(("arbitrary",)),
        name="ssd_scan",
    )(xp, xp, xp, xp, cwx, cbx, cwb, cbb, dtb, alog, dsk, nrm, e)


def _swa_kernel(sink_ref, q_ref, kvc_ref, kvp_ref, cq_ref, sq_ref, cp_ref, sp_ref, o_ref):
    i = pl.program_id(0)
    B = ATTN_BLOCK
    lane = lax.broadcasted_iota(I32, (B, LANES), 1)
    lo_half = lane < ROPE_DIM
    cq, sq = cq_ref[...], sq_ref[...]
    kc = _rope_tile(kvc_ref[:, 0:LANES], cq, sq)
    kp = _rope_tile(kvp_ref[:, 0:LANES], cp_ref[...], sp_ref[...])
    kcat = jnp.concatenate([kp, kc], axis=0)
    kmat = (kcat.astype(BF16), pltpu.roll(kcat, ROPE_DIM, 1).astype(BF16))
    vcat = jnp.concatenate([kvp_ref[:, LANES:2 * LANES], kvc_ref[:, LANES:2 * LANES]], axis=0)
    vrol = pltpu.roll(vcat, ROPE_DIM, 1)
    lane2 = lax.broadcasted_iota(I32, (2 * B, LANES), 1) < ROPE_DIM
    vdup = (jnp.where(lane2, vcat, vrol).astype(BF16), jnp.where(lane2, vrol, vcat).astype(BF16))
    r = lax.broadcasted_iota(I32, (B, 2 * B), 0)
    col = lax.broadcasted_iota(I32, (B, 2 * B), 1)
    mask = (col > r) & (col <= r + B) & ((i > 0) | (col >= B))
    scale = ROPE_DIM ** -0.5
    hpg = SWA_Q_HEADS // SWA_KV_HEADS
    for j in range(SWA_Q_HEADS // 2):
        g = (2 * j) // hpg
        qt = _rope_tile(q_ref[:, j * LANES:(j + 1) * LANES], cq, sq)
        outs = []
        for par in (0, 1):
            h = 2 * j + par
            qm = jnp.where(lo_half if par == 0 else jnp.logical_not(lo_half), qt, 0.0).astype(BF16)
            logit = _dot_nt(qm, kmat[0] if par == g else kmat[1]) * scale
            logit = jnp.where(mask, logit, -jnp.inf)
            sink = sink_ref[h]
            m = jnp.maximum(jnp.max(logit, axis=-1, keepdims=True), sink)
            ex = jnp.exp(logit - m)
            prob = ex / (jnp.sum(ex, axis=-1, keepdims=True) + jnp.exp(sink - m))
            outs.append(_dot(prob.astype(BF16), vdup[g]))
        o_ref[:, j * LANES:(j + 1) * LANES] = jnp.where(lo_half, outs[0], outs[1]).astype(BF16)


def _swa(xp, sinks, cos_t, sin_t):
    s = xp.shape[0]
    B = ATTN_BLOCK
    qw = SWA_Q_HEADS * ROPE_DIM
    kvw = 2 * SWA_KV_HEADS * ROPE_DIM
    prev = lambda i: (jnp.maximum(i - 1, 0), 0)
    cur = lambda i: (i, 0)
    return pl.pallas_call(
        _swa_kernel,
        grid=(s // B,),
        in_specs=[
            pl.BlockSpec(memory_space=pltpu.SMEM),
            pl.BlockSpec((B, qw), lambda i: (i, EV_Q // qw)),
            pl.BlockSpec((B, kvw), lambda i: (i, EV_KV // kvw)),
            pl.BlockSpec((B, kvw), lambda i: (jnp.maximum(i - 1, 0), EV_KV // kvw)),
            pl.BlockSpec((B, LANES), cur), pl.BlockSpec((B, LANES), cur),
            pl.BlockSpec((B, LANES), prev), pl.BlockSpec((B, LANES), prev),
        ],
        out_specs=pl.BlockSpec((B, qw), cur),
        out_shape=jax.ShapeDtypeStruct((s, qw), BF16),
        compiler_params=_cparams(("parallel",)),
        name="swa_sink",
    )(sinks, xp, xp, xp, cos_t, sin_t, cos_t, sin_t)


def _dsa_prep_kernel(qn_ref, qr_ref, qi_ref, ckv_ref, kk_ref, wi_ref, wuk_ref, kvn_ref, c_ref, s_ref,
                     qlat_ref, qrope_ref, qidx_ref, ckvn_ref, kr_ref, ki_ref, wis_ref):
    c, s = c_ref[...], s_ref[...]
    lane = lax.broadcasted_iota(I32, c.shape, 1)
    lo_half = lane < ROPE_DIM
    for h in range(MLA_HEADS):
        qn = qn_ref[:, h * MLA_NOPE:(h + 1) * MLA_NOPE].astype(BF16)
        qlat_ref[h] = _dot(qn, wuk_ref[h]).astype(BF16)
    for src, dst in ((qr_ref, qrope_ref), (qi_ref, qidx_ref)):
        for j in range(MLA_HEADS // 2):
            t = _rope_tile(src[:, j * LANES:(j + 1) * LANES], c, s)
            dst[2 * j] = jnp.where(lo_half, t, 0.0).astype(BF16)
            dst[2 * j + 1] = jnp.where(lo_half, pltpu.roll(t, ROPE_DIM, 1), 0.0).astype(BF16)
    kk = _rope_tile(kk_ref[...], c, s)
    kr_ref[...] = jnp.where(lo_half, kk, 0.0).astype(BF16)
    ki_ref[...] = jnp.where(lo_half, pltpu.roll(kk, ROPE_DIM, 1), 0.0).astype(BF16)
    ckv = ckv_ref[...]
    ms = jnp.mean(ckv * ckv, axis=-1, keepdims=True)
    ckvn_ref[...] = (ckv * lax.rsqrt(ms + NORM_EPS) * kvn_ref[...]).astype(BF16)
    wis_ref[...] = wi_ref[...] * (IDX_HEADS ** -0.5 * IDX_DIM ** -0.5)


def _dsa_prep(xp, wuk, kvn, cos_t, sin_t):
    s = xp.shape[0]
    tm = min(256, s)
    H = MLA_HEADS
    fix2 = lambda i: (0, 0)
    hrow = lambda i: (0, i, 0)
    row = lambda i: (i, 0)
    return pl.pallas_call(
        _dsa_prep_kernel,
        grid=(s // tm,),
        in_specs=[
            pl.BlockSpec((tm, 2048), lambda i: (i, OD_QN // 2048)),
            pl.BlockSpec((tm, 1024), lambda i: (i, OD_QR // 1024)),
            pl.BlockSpec((tm, 1024), lambda i: (i, OD_QI // 1024)),
            pl.BlockSpec((tm, MLA_RANK), lambda i: (i, OD_CKV // MLA_RANK)),
            pl.BlockSpec((tm, LANES), lambda i: (i, OD_KK // LANES)),
            pl.BlockSpec((tm, LANES), lambda i: (i, OD_WI // LANES)),
            pl.BlockSpec(wuk.shape, lambda i: (0, 0, 0)),
            pl.BlockSpec((1, MLA_RANK), fix2),
            pl.BlockSpec((tm, LANES), row), pl.BlockSpec((tm, LANES), row),
        ],
        out_specs=[
            pl.BlockSpec((H, tm, MLA_RANK), hrow),
            pl.BlockSpec((H, tm, LANES), hrow),
            pl.BlockSpec((H, tm, LANES), hrow),
            pl.BlockSpec((tm, MLA_RANK), row),
            pl.BlockSpec((tm, LANES), row),
            pl.BlockSpec((tm, LANES), row),
            pl.BlockSpec((tm, LANES), row),
        ],
        out_shape=[
            jax.ShapeDtypeStruct((H, s, MLA_RANK), BF16),
            jax.ShapeDtypeStruct((H, s, LANES), BF16),
            jax.ShapeDtypeStruct((H, s, LANES), BF16),
            jax.ShapeDtypeStruct((s, MLA_RANK), BF16),
            jax.ShapeDtypeStruct((s, LANES), BF16),
            jax.ShapeDtypeStruct((s, LANES), BF16),
            jax.ShapeDtypeStruct((s, LANES), F32),
        ],
        compiler_params=_cparams(("parallel",)),
        name="dsa_prep",
    )(xp, xp, xp, xp, xp, xp, wuk, kvn, cos_t, sin_t)


def _dsa_select_kernel(topk, qi_ref, wi_ref, ki_ref, bias_ref, key_ref):
    i = pl.program_id(0)
    Q, KC = DSA_SEL_Q, DSA_KC
    H = IDX_HEADS
    n_chunks = bias_ref.shape[0]
    n_vis = ((i + 1) * Q + KC - 1) // KC
    wi = wi_ref[...]
    wcols = [wi[:, h:h + 1] for h in range(H)]
    qpos = i * Q + lax.broadcasted_iota(I32, (Q, KC), 0)
    kloc = lax.broadcasted_iota(I32, (Q, KC), 1)
    HG = DSA_SEL_HEADS_PER_DOT

    def score_chunk(c, carry):
        k = ki_ref[pl.ds(pl.multiple_of(c * KC, KC), KC), :]
        acc = jnp.zeros((Q, KC), F32)
        for g in range(H // HG):
            sc = _dot_nt(qi_ref[g * HG:(g + 1) * HG].reshape(HG * Q, LANES), k)
            for hh in range(HG):
                acc = acc + jnp.maximum(sc[hh * Q:(hh + 1) * Q, :], 0.0) * wcols[g * HG + hh]
        acc = jnp.where(c * KC + kloc <= qpos, acc, -jnp.inf)
        bits = pltpu.bitcast(acc, I32)
        key_ref[c] = bits ^ ((bits >> 31) & jnp.int32(0x7FFFFFFF))
        return carry

    lax.fori_loop(0, n_vis, score_chunk, 0)

    def count_ge(cand):
        def body(c, cnt):
            ones = jnp.where(key_ref[c] >= cand, 1, 0)
            for b in range(KC // LANES):
                cnt = cnt + ones[:, b * LANES:(b + 1) * LANES]
            return cnt
        cnt = lax.fori_loop(0, n_vis, body, jnp.zeros((Q, LANES), I32))
        return jnp.sum(cnt.astype(F32), axis=-1, keepdims=True)

    int_min = jnp.int32(-2 ** 31)
    thr = jnp.where(count_ge(jnp.zeros((Q, 1), I32)) >= topk, 0, int_min).astype(I32)

    def bit_step(b, thr):
        cand = thr | (jnp.int32(1) << (30 - b))
        return jnp.where(count_ge(cand) >= topk, cand, thr)

    thr = lax.fori_loop(0, 31, bit_step, thr)

    def emit(c, carry):
        sel = (key_ref[c] >= thr) & (c * KC + kloc <= qpos)
        bias_ref[c] = jnp.where(sel, 0.0, MASK_NEG)
        return carry

    lax.fori_loop(0, n_vis, emit, 0)

    def fill(c, carry):
        bias_ref[c] = jnp.full((Q, KC), MASK_NEG, F32)
        return carry

    lax.fori_loop(n_vis, n_chunks, fill, 0)


def _dsa_select(qidx, wis, ki, topk):
    H, s, _ = qidx.shape
    Q, KC = DSA_SEL_Q, DSA_KC
    nch = s // KC
    return pl.pallas_call(
        functools.partial(_dsa_select_kernel, topk),
        grid=(s // Q,),
        in_specs=[
            pl.BlockSpec((H, Q, LANES), lambda i: (0, i, 0)),
            pl.BlockSpec((Q, LANES), lambda i: (i, 0)),
            pl.BlockSpec((s, LANES), lambda i: (0, 0)),
        ],
        out_specs=pl.BlockSpec((nch, Q, KC), lambda i: (0, i, 0)),
        out_shape=jax.ShapeDtypeStruct((nch, s, KC), F32),
        scratch_shapes=[pltpu.VMEM((nch, Q, KC), I32)],
        compiler_params=_cparams(("parallel",)),
        name="dsa_select",
    )(qidx, wis, ki)


def _dsa_attn_kernel(ql_ref, qr_ref, bias_ref, ckv_ref, kr_ref, o_ref, m_ref, l_ref, acc_ref, s_ref):
    i = pl.program_id(0)
    Q, KC, H = DSA_ATT_Q, DSA_KC, MLA_HEADS
    n_vis = ((i + 1) * Q + KC - 1) // KC
    n_chunks = bias_ref.shape[0]
    m_ref[...] = jnp.full(m_ref.shape, -jnp.inf, F32)
    l_ref[...] = jnp.zeros(l_ref.shape, F32)
    acc_ref[...] = jnp.zeros(acc_ref.shape, F32)

    def keys(c):
        off = pl.multiple_of(c * KC, KC)
        return ckv_ref[pl.ds(off, KC), :], kr_ref[pl.ds(off, KC), :]

    def raw_logits(c):
        ck, kr = keys(c)
        ql = ql_ref[...].reshape(H * Q, MLA_RANK)
        qr = qr_ref[...].reshape(H * Q, LANES)
        return _dot_nt(ql, ck) + _dot_nt(qr, kr)

    def consume(c, slot):
        logit = s_ref[slot] * MLA_SCALE
        logit = (logit.reshape(H, Q, KC) + bias_ref[c][None]).reshape(H * Q, KC)
        m_old = m_ref[...]
        m_new = jnp.maximum(m_old, jnp.max(logit, axis=-1, keepdims=True))
        alpha = jnp.exp(m_old - m_new)
        p = jnp.exp(logit - m_new)
        l_ref[...] = alpha * l_ref[...] + jnp.sum(p, axis=-1, keepdims=True)
        acc_ref[...] = alpha * acc_ref[...] + _dot(p.astype(BF16), keys(c)[0])
        m_ref[...] = m_new

    s_ref[0] = raw_logits(0)

    def pair(j, carry):
        c = 2 * j
        s_ref[1] = raw_logits(c + 1)
        consume(c, 0)
        s_ref[0] = raw_logits(jnp.minimum(c + 2, n_chunks - 1))
        consume(c + 1, 1)
        return carry

    lax.fori_loop(0, (n_vis + 1) // 2, pair, 0)
    o_ref[...] = (acc_ref[...] / l_ref[...]).reshape(H, Q, MLA_RANK).astype(BF16)


def _dsa_attn(qlat, qrope, bias, ckvn, kr):
    H, s, _ = qlat.shape
    Q, KC = DSA_ATT_Q, DSA_KC
    nch = s // KC
    assert nch % 2 == 0
    hrow = lambda i: (0, i, 0)
    fix = lambda i: (0, 0)
    return pl.pallas_call(
        _dsa_attn_kernel,
        grid=(s // Q,),
        in_specs=[
            pl.BlockSpec((H, Q, MLA_RANK), hrow),
            pl.BlockSpec((H, Q, LANES), hrow),
            pl.BlockSpec((nch, Q, KC), hrow),
            pl.BlockSpec((s, MLA_RANK), fix),
            pl.BlockSpec((s, LANES), fix),
        ],
        out_specs=pl.BlockSpec((H, Q, MLA_RANK), hrow),
        out_shape=jax.ShapeDtypeStruct((H, s, MLA_RANK), BF16),
        scratch_shapes=[
            pltpu.VMEM((H * Q, 1), F32),
            pltpu.VMEM((H * Q, 1), F32),
            pltpu.VMEM((H * Q, MLA_RANK), F32),
            pltpu.VMEM((2, H * Q, KC), F32),
        ],
        compiler_params=_cparams(("parallel",)),
        name="dsa_attn",
    )(qlat, qrope, bias, ckvn, kr)


def _odd_out_kernel(ol_ref, wuv_ref, wo_ref, x_ref, g_ref, b_ref, o_ref, u_ref):
    for h in range(MLA_HEADS):
        u_ref[:, h * MLA_V:(h + 1) * MLA_V] = _dot(ol_ref[h], wuv_ref[h]).astype(BF16)
    mix = _dot(u_ref[...], wo_ref[...])
    o_ref[...] = _layer_norm(DEEPNORM_ALPHA * x_ref[...] + mix, g_ref[...], b_ref[...])


def _odd_out(olat, wuv, wo, x, g, b):
    H, s, _ = olat.shape
    tm = min(256, s)
    row = lambda i: (i, 0)
    fix = lambda i: (0, 0)
    return pl.pallas_call(
        _odd_out_kernel,
        grid=(s // tm,),
        in_specs=[
            pl.BlockSpec((H, tm, MLA_RANK), lambda i: (0, i, 0)),
            pl.BlockSpec(wuv.shape, lambda i: (0, 0, 0)),
            pl.BlockSpec(wo.shape, fix),
            pl.BlockSpec((tm, D_MODEL), row),
            pl.BlockSpec((1, D_MODEL), fix), pl.BlockSpec((1, D_MODEL), fix),
        ],
        out_specs=pl.BlockSpec((tm, D_MODEL), row),
        out_shape=jax.ShapeDtypeStruct((s, D_MODEL), F32),
        scratch_shapes=[pltpu.VMEM((tm, MLA_HEADS * MLA_V), BF16)],
        compiler_params=_cparams(("parallel",)),
        name="odd_out_ln",
    )(olat, wuv, wo, x, g, b)


ROUTE_LANE0 = MOE_GROUPS


def _router_kernel(h_ref, wr_ref, br_ref, info_ref, cnt_ref, run_ref):
    i = pl.program_id(0)

    @pl.when(i == 0)
    def _():
        run_ref[...] = jnp.zeros(run_ref.shape, F32)

    h = h_ref[...]
    tm = h.shape[0]
    h_hi = h.astype(BF16)
    h_lo = (h - h_hi.astype(F32)).astype(BF16)
    w = wr_ref[...]
    w_hi = w.astype(BF16)
    w_lo = (w - w_hi.astype(F32)).astype(BF16)
    logits = _dot(h_hi, w_hi) + (_dot(h_hi, w_lo) + _dot(h_lo, w_hi)) + br_ref[...]
    lane = lax.broadcasted_iota(I32, (tm, LANES), 1)
    lane_f = lane.astype(F32)
    neg = -jnp.inf
    big = float(LANES)
    is_grp = lane < MOE_GROUPS
    gl = jnp.where(is_grp, logits, neg)
    gmax = jnp.max(gl, axis=-1, keepdims=True)
    gsel = jnp.min(jnp.where(gl == gmax, lane_f, big), axis=-1, keepdims=True)
    gsum = jnp.sum(jnp.where(is_grp, jnp.exp(logits - gmax), 0.0), axis=-1, keepdims=True)
    egrp = ((lane - ROUTE_LANE0) >> 3).astype(F32)
    valid = (lane >= ROUTE_LANE0) & (lane < ROUTE_LANE0 + MOE_EXPERTS) & (egrp == gsel)
    el = jnp.where(valid, logits, neg)
    v1 = jnp.max(el, axis=-1, keepdims=True)
    i1 = jnp.min(jnp.where(el == v1, lane_f, big), axis=-1, keepdims=True)
    el2 = jnp.where(lane_f == i1, neg, el)
    v2 = jnp.max(el2, axis=-1, keepdims=True)
    i2 = jnp.min(jnp.where(el2 == v2, lane_f, big), axis=-1, keepdims=True)
    t = jnp.exp(v2 - v1)
    p1 = 1.0 / (1.0 + t)
    p2 = t / (1.0 + t)
    ggate = 1.0 / gsum
    m1 = lane_f == i1
    m2 = lane_f == i2
    memb = jnp.where(m1 | m2, 1.0, 0.0)
    tri = (lax.broadcasted_iota(I32, (tm, tm), 0) > lax.broadcasted_iota(I32, (tm, tm), 1))
    cum = _dot(jnp.where(tri, 1.0, 0.0).astype(BF16), memb.astype(BF16)) + run_ref[...]
    rank1 = jnp.sum(jnp.where(m1, cum, 0.0), axis=-1, keepdims=True)
    rank2 = jnp.sum(jnp.where(m2, cum, 0.0), axis=-1, keepdims=True)
    run_ref[...] = run_ref[...] + jnp.sum(memb, axis=0, keepdims=True)
    info = jnp.where(lane == 0, i1 - ROUTE_LANE0, 0.0)
    info = jnp.where(lane == 1, i2 - ROUTE_LANE0, info)
    info = jnp.where(lane == 2, p1 * ggate, info)
    info = jnp.where(lane == 3, p2 * ggate, info)
    info = jnp.where(lane == 4, rank1, info)
    info = jnp.where(lane == 5, rank2, info)
    info_ref[...] = info
    cnt_ref[...] = run_ref[...]


def _router(h, wr, br):
    t = h.shape[0]
    tm = min(512, t)
    return pl.pallas_call(
        _router_kernel,
        grid=(t // tm,),
        in_specs=[
            pl.BlockSpec((tm, D_MODEL), lambda i: (i, 0)),
            pl.BlockSpec(wr.shape, lambda i: (0, 0)),
            pl.BlockSpec((1, LANES), lambda i: (0, 0)),
        ],
        out_specs=[pl.BlockSpec((tm, LANES), lambda i: (i, 0)), pl.BlockSpec((1, LANES), lambda i: (0, 0))],
        out_shape=[jax.ShapeDtypeStruct((t, LANES), F32), jax.ShapeDtypeStruct((1, LANES), F32)],
        scratch_shapes=[pltpu.VMEM((1, LANES), F32)],
        compiler_params=_cparams(("arbitrary",)),
        name="moe_router",
    )(h, wr, br)


def _plan_kernel(info_ref, ps_ref, pos_ref):
    info = info_ref[...]
    lane = lax.broadcasted_iota(I32, info.shape, 1)
    lane_f = lane.astype(F32)
    ps = ps_ref[...]
    pos1 = jnp.sum(jnp.where(lane_f == info[:, 0:1], ps, 0.0), axis=-1, keepdims=True) + info[:, 4:5]
    pos2 = jnp.sum(jnp.where(lane_f == info[:, 1:2], ps, 0.0), axis=-1, keepdims=True) + info[:, 5:6]
    pos_ref[...] = jnp.where(lane == 0, pos1, jnp.where(lane == 1, pos2, 0.0)).astype(I32)


def _plan(info, pad_start):
    t = info.shape[0]
    tm = min(1024, t)
    return pl.pallas_call(
        _plan_kernel,
        grid=(t // tm,),
        in_specs=[pl.BlockSpec((tm, LANES), lambda i: (i, 0)), pl.BlockSpec((1, LANES), lambda i: (0, 0))],
        out_specs=pl.BlockSpec((tm, LANES), lambda i: (i, 0)),
        out_shape=jax.ShapeDtypeStruct((t, LANES), I32),
        compiler_params=_cparams(("parallel",)),
        name="moe_plan",
    )(info, pad_start)


def _invert_kernel(pos_ref, rt_ref):
    def clear(r, carry):
        rt_ref[r] = 0
        return carry

    lax.fori_loop(0, rt_ref.shape[0], clear, 0, unroll=8)

    def put(n, carry):
        rt_ref[pos_ref[n]] = lax.shift_right_logical(n, 1)
        return carry

    lax.fori_loop(0, pos_ref.shape[0], put, 0, unroll=8)


def _invert(pos_flat, n_rows):
    return pl.pallas_call(
        _invert_kernel,
        in_specs=[pl.BlockSpec(memory_space=pltpu.SMEM)],
        out_specs=pl.BlockSpec(memory_space=pltpu.SMEM),
        out_shape=jax.ShapeDtypeStruct((n_rows,), I32),
        name="moe_invert",
    )(pos_flat)


def _expert_kernel(be_ref, nu_ref, rt_ref, h_ref, wg_ref, wu_ref, wd_ref, o_ref, xbuf, wgb, wub, wdb, sem):
    b = pl.program_id(0)
    n_used = nu_ref[0]
    R = EXPERT_ROWS

    def row_copy(tok, r, slot):
        return pltpu.make_async_copy(h_ref.at[pl.ds(tok, 1)], xbuf.at[slot, pl.ds(r, 1)], sem.at[slot])

    def issue(blk, slot):
        def body(r, carry):
            row_copy(rt_ref[blk * R + r], r, slot).start()
            return carry
        lax.fori_loop(0, R, body, 0, unroll=8)

    def drain(slot):
        def body(r, carry):
            row_copy(0, 0, slot).wait()
            return carry
        lax.fori_loop(0, R, body, 0, unroll=8)

    slot = b % 2

    @pl.when(b == 0)
    def _():
        issue(0, 0)

    @pl.when(b < n_used)
    def _():
        drain(slot)

        @pl.when(b + 1 < n_used)
        def _():
            issue(b + 1, 1 - slot)

        @pl.when((b == 0) | (be_ref[b] != be_ref[jnp.maximum(b - 1, 0)]))
        def _():
            wgb[...] = wg_ref[0].astype(BF16)
            wub[...] = wu_ref[0].astype(BF16)
            wdb[...] = wd_ref[0].astype(BF16)

        x = xbuf[slot].astype(BF16)
        hid = _silu(_dot(x, wgb[...])) * _dot(x, wub[...])
        o_ref[...] = _dot(hid.astype(BF16), wdb[...])

    @pl.when(b >= n_used)
    def _():
        o_ref[...] = jnp.zeros(o_ref.shape, F32)


def _experts(blk_e, n_used, row_token, h, wg, wu, wd):
    r = row_token.shape[0]
    nb = r // EXPERT_ROWS
    wmap = lambda b, be, nu, rt: (be[b], 0, 0)
    return pl.pallas_call(
        _expert_kernel,
        grid_spec=pltpu.PrefetchScalarGridSpec(
            num_scalar_prefetch=3,
            grid=(nb,),
            in_specs=[
                pl.BlockSpec(memory_space=pl.ANY),
                pl.BlockSpec((1, D_MODEL, MOE_FF), wmap),
                pl.BlockSpec((1, D_MODEL, MOE_FF), wmap),
                pl.BlockSpec((1, MOE_FF, D_MODEL), wmap),
            ],
            out_specs=pl.BlockSpec((EXPERT_ROWS, D_MODEL), lambda b, be, nu, rt: (b, 0)),
            scratch_shapes=[
                pltpu.VMEM((2, EXPERT_ROWS, D_MODEL), F32),
                pltpu.VMEM((D_MODEL, MOE_FF), BF16),
                pltpu.VMEM((D_MODEL, MOE_FF), BF16),
                pltpu.VMEM((MOE_FF, D_MODEL), BF16),
                pltpu.SemaphoreType.DMA((2,)),
            ],
        ),
        out_shape=jax.ShapeDtypeStruct((r, D_MODEL), F32),
        compiler_params=_cparams(("arbitrary",)),
        name="moe_experts",
    )(blk_e, n_used, row_token, h, wg, wu, wd)


COMBINE_TOKENS = 128


def _combine_kernel(pos_ref, y_ref, info_ref, x_ref, g_ref, b_ref, o_ref, buf, sem):
    i = pl.program_id(0)
    n = pl.num_programs(0)
    TB = COMBINE_TOKENS

    def row_copy(src_row, k, tt, slot):
        return pltpu.make_async_copy(y_ref.at[pl.ds(src_row, 1)], buf.at[slot, k, pl.ds(tt, 1)], sem.at[slot])

    def issue(blk, slot):
        def body(tt, carry):
            t = blk * TB + tt
            row_copy(pos_ref[2 * t], 0, tt, slot).start()
            row_copy(pos_ref[2 * t + 1], 1, tt, slot).start()
            return carry
        lax.fori_loop(0, TB, body, 0)

    def drain(slot):
        def body(tt, carry):
            row_copy(0, 0, 0, slot).wait()
            row_copy(0, 1, 0, slot).wait()
            return carry
        lax.fori_loop(0, TB, body, 0)

    slot = i % 2

    @pl.when(i == 0)
    def _():
        issue(0, 0)

    drain(slot)

    @pl.when(i + 1 < n)
    def _():
        issue(i + 1, 1 - slot)

    info = info_ref[...]
    y = buf[slot, 0] * info[:, 2:3] + buf[slot, 1] * info[:, 3:4]
    o_ref[...] = _layer_norm(DEEPNORM_ALPHA * x_ref[...] + y, g_ref[...], b_ref[...])


def _combine(pos_flat, y_rows, info, x, g, b):
    t = x.shape[0]
    TB = min(COMBINE_TOKENS, t)
    row = lambda i, pos: (i, 0)
    fix = lambda i, pos: (0, 0)
    return pl.pallas_call(
        _combine_kernel,
        grid_spec=pltpu.PrefetchScalarGridSpec(
            num_scalar_prefetch=1,
            grid=(t // TB,),
            in_specs=[
                pl.BlockSpec(memory_space=pl.ANY),
                pl.BlockSpec((TB, LANES), row),
                pl.BlockSpec((TB, D_MODEL), row),
                pl.BlockSpec((1, D_MODEL), fix), pl.BlockSpec((1, D_MODEL), fix),
            ],
            out_specs=pl.BlockSpec((TB, D_MODEL), row),
            scratch_shapes=[pltpu.VMEM((2, 2, TB, D_MODEL), F32), pltpu.SemaphoreType.DMA((2,))],
        ),
        out_shape=jax.ShapeDtypeStruct((t, D_MODEL), F32),
        compiler_params=_cparams(("arbitrary",)),
        name="moe_combine_ln",
    )(pos_flat, y_rows, info, x, g, b)


def _hier_moe_ln(x, wr, br, wg, wu, wd, ln_g, ln_b):
    t = x.shape[0]
    info, cnt = _router(x, wr, br)
    counts = cnt[0, ROUTE_LANE0:ROUTE_LANE0 + MOE_EXPERTS].astype(I32)
    padded = (counts + EXPERT_ROWS - 1) // EXPERT_ROWS * EXPERT_ROWS
    pad_end = jnp.cumsum(padded)
    pad_start = jnp.zeros((1, LANES), F32).at[0, :MOE_EXPERTS].set((pad_end - padded).astype(F32))
    n_rows = 2 * t + MOE_EXPERTS * EXPERT_ROWS
    n_blocks = n_rows // EXPERT_ROWS
    blk_start = jnp.arange(n_blocks, dtype=I32) * EXPERT_ROWS
    blk_e = jnp.minimum(jnp.sum(pad_end[None, :] <= blk_start[:, None], axis=1), MOE_EXPERTS - 1).astype(I32)
    n_used = (pad_end[-1:] // EXPERT_ROWS).astype(I32)
    pos = _plan(info, pad_start)[:, :2].reshape(-1)
    y_rows = _experts(blk_e, n_used, _invert(pos, n_rows), x, wg, wu, wd)
    return _combine(pos, y_rows, info, x, ln_g, ln_b)


def _rope_tables(positions):
    inv = ROPE_THETA ** (-jnp.arange(0, ROPE_DIM, 2, dtype=F32) / ROPE_DIM)
    ang = positions.astype(F32)[:, None] * inv
    cos, sin = jnp.cos(ang), jnp.sin(ang)
    cos_t = jnp.tile(cos, (1, LANES // (ROPE_DIM // 2)))
    sin_t = jnp.tile(jnp.concatenate([-sin, sin], axis=1), (1, LANES // ROPE_DIM))
    return cos_t, sin_t


def _pad_cols(w, n):
    return jnp.pad(w, ((0, 0), (0, n - w.shape[1])))


def _even_w_in(w):
    z, xbc, dt, q, kv = jnp.split(w, [2048, 5120, 5152, 6176], axis=1)
    return _pad_cols(jnp.concatenate([z, xbc, q, kv, dt], axis=1), EV_NP).astype(BF16)


def _odd_w_in(w):
    q, ckv, krope, qi, ki, wi = jnp.split(w, [3072, 3584, 3648, 4672, 4736], axis=1)
    q = q.reshape(D_MODEL, MLA_HEADS, MLA_NOPE + MLA_ROPE)
    qn = q[:, :, :MLA_NOPE].reshape(D_MODEL, -1)
    qr = q[:, :, MLA_NOPE:].reshape(D_MODEL, -1)
    return _pad_cols(jnp.concatenate([qn, qr, qi, ckv, krope, ki, _pad_cols(wi, LANES)], axis=1), OD_NP).astype(BF16)


def _head_expand_matrix():
    e = np.zeros((LANES, SSM_INNER), np.float32)
    for h in range(SSM_HEADS):
        e[h, h * SSM_HEAD_DIM:(h + 1) * SSM_HEAD_DIM] = 1.0
    return jnp.asarray(e, BF16)


def kernel(x, p, positions, ev_w_in, ev_conv_w, ev_conv_b, ev_dt_bias, ev_a_log, ev_d_skip, ev_ssm_norm, ev_sinks, ev_w_out, od_w_in, od_kv_norm, od_w_uk, od_w_uv, od_w_out, ln1_g, ln1_b, ln2_g, ln2_b, moe_router_group, moe_router_group_b, moe_router_expert, moe_router_expert_b, moe_w_gate, moe_w_up, moe_w_down, ple_w_proj, ple_w_gate, ple_b_gate):
    batch, s, d = x.shape
    assert batch == 1 and d == D_MODEL
    xs = x[0]
    cos_t, sin_t = _rope_tables(positions[0])
    e_mat = _head_expand_matrix()
    topk = min(IDX_TOPK_MAX, s // 4)
    row = lambda v: v.reshape(1, -1)
    pad_row = lambda v: _pad_cols(v.reshape(1, -1), LANES)
    for i in range(DEPTH):
        j = i // 2
        if i % 2 == 0:
            xp = _inproj(xs, _even_w_in(ev_w_in[j]))
            y_ssm = _ssd(xp, ev_conv_w[j], row(ev_conv_b[j]), pad_row(ev_dt_bias[j]), pad_row(ev_a_log[j]),
                         row(jnp.repeat(ev_d_skip[j], SSM_HEAD_DIM)), row(ev_ssm_norm[j]), e_mat)
            y_att = _swa(xp, ev_sinks[j], cos_t, sin_t)
            w_out = ev_w_out[j].astype(BF16)
            xs = _even_out(y_ssm, y_att, w_out[:SSM_INNER], w_out[SSM_INNER:], xs, row(ln1_g[i]), row(ln1_b[i]))
        else:
            xp = _inproj(xs, _odd_w_in(od_w_in[j]))
            qlat, qrope, qidx, ckvn, kr, ki, wis = _dsa_prep(
                xp, od_w_uk[j].astype(BF16), row(od_kv_norm[j]), cos_t, sin_t)
            bias = _dsa_select(qidx, wis, ki, topk)
            olat = _dsa_attn(qlat, qrope, bias, ckvn, kr)
            xs = _odd_out(olat, od_w_uv[j].astype(BF16), od_w_out[j].astype(BF16), xs, row(ln1_g[i]), row(ln1_b[i]))
        wr = _pad_cols(jnp.concatenate([moe_router_group[i], moe_router_expert[i]], axis=1), LANES)
        br = pad_row(jnp.concatenate([moe_router_group_b[i], moe_router_expert_b[i]]))
        xs = _hier_moe_ln(xs, wr, br, moe_w_gate[i], moe_w_up[i], moe_w_down[i], row(ln2_g[i]), row(ln2_b[i]))
        xs = _ple(xs, p[i, 0], ple_w_gate[i].astype(BF16), row(ple_b_gate[i]), ple_w_proj[i].astype(BF16))
    return xs[None]
```

```python
import functools

import jax
import jax.numpy as jnp
import numpy as np
from jax import lax
from jax.experimental import pallas as pl
from jax.experimental.pallas import tpu as pltpu

F32 = jnp.float32
BF16 = jnp.bfloat16
I32 = jnp.int32
I16 = jnp.int16

D_MODEL = 2048
DEPTH = 4
ROPE_THETA = 10000.0
ROPE_DIM = 64
NORM_EPS = 1e-5
SSM_HEADS = 32
SSM_HEAD_DIM = 64
SSM_INNER = SSM_HEADS * SSM_HEAD_DIM
SSM_GROUPS = 4
SSM_STATE = 128
SSM_CONV = 4
SSM_CHUNK = 128
SWA_Q_HEADS = 16
SWA_KV_HEADS = 2
ATTN_BLOCK = 128
MLA_HEADS = 16
MLA_NOPE = 128
MLA_ROPE = ROPE_DIM
MLA_V = 128
MLA_RANK = 512
MLA_SCALE = (MLA_NOPE + MLA_ROPE) ** -0.5
IDX_HEADS = 16
IDX_DIM = ROPE_DIM
IDX_TOPK_MAX = 256
MOE_GROUPS = 4
MOE_EPG = 8
MOE_EXPERTS = MOE_GROUPS * MOE_EPG
MOE_FF = 512
PLE_DIM = 256
DEEPNORM_ALPHA = (2 * DEPTH) ** 0.25

LANES = 128
SUBLANES = 8
VMEM_LIMIT_BYTES = 56 * 1024 * 1024

EXPERT_ROWS = 256
DSA_SEL_Q = 128
DSA_ATT_Q = 64
DSA_KC = 512
DSA_SEL_HEADS_PER_DOT = 4
MASK_NEG = -1e30

EV_Z, EV_XS, EV_BC, EV_Q, EV_KV, EV_DT = 0, 2048, 4096, 5120, 6144, 6400
EV_NP = 6656
OD_QN, OD_QR, OD_QI, OD_CKV, OD_KK, OD_WI = 0, 2048, 3072, 4096, 4608, 4736
OD_NP = 5120


def _cparams(sem, vmem=VMEM_LIMIT_BYTES):
    return pltpu.CompilerParams(dimension_semantics=sem, vmem_limit_bytes=vmem)


def _dot(a, b):
    return jnp.dot(a, b, preferred_element_type=F32)


def _dot_nt(a, b):
    return lax.dot_general(a, b, (((1,), (1,)), ((), ())), preferred_element_type=F32)


def _split3(v):
    hi = v.astype(BF16)
    r = v - hi.astype(F32)
    mid = r.astype(BF16)
    lo = (r - mid.astype(F32)).astype(BF16)
    return hi, mid, lo


def _expand(v, e):
    hi, mid, lo = _split3(v)
    return _dot(hi, e) + _dot(mid, e) + _dot(lo, e)


def _silu(v):
    return v * jax.nn.sigmoid(v)


def _layer_norm(v, g, b):
    mu = jnp.mean(v, axis=-1, keepdims=True)
    vc = v - mu
    var = jnp.mean(vc * vc, axis=-1, keepdims=True)
    return vc * lax.rsqrt(var + NORM_EPS) * g + b


def _rope_tile(t, c, s):
    lane = lax.broadcasted_iota(I32, t.shape, 1)
    first_half = (lane & 32) == 0
    swapped = jnp.where(first_half, pltpu.roll(t, LANES - 32, 1), pltpu.roll(t, 32, 1))
    return t * c + swapped * s


def _inproj_kernel(x_ref, w_ref, o_ref):
    o_ref[...] = _dot(x_ref[...].astype(BF16), w_ref[...])


def _inproj(x, w):
    m, k = x.shape
    n = w.shape[1]
    tm, tn = min(1024, m), 512
    return pl.pallas_call(
        _inproj_kernel,
        grid=(m // tm, n // tn),
        in_specs=[pl.BlockSpec((tm, k), lambda i, j: (i, 0)), pl.BlockSpec((k, tn), lambda i, j: (0, j))],
        out_specs=pl.BlockSpec((tm, tn), lambda i, j: (i, j)),
        out_shape=jax.ShapeDtypeStruct((m, n), F32),
        compiler_params=_cparams(("parallel", "arbitrary")),
        name="inproj",
    )(x, w)


def _even_out_kernel(a1_ref, a2_ref, w1_ref, w2_ref, x_ref, g_ref, b_ref, o_ref):
    mix = _dot(a1_ref[...], w1_ref[...]) + _dot(a2_ref[...], w2_ref[...])
    o_ref[...] = _layer_norm(DEEPNORM_ALPHA * x_ref[...] + mix, g_ref[...], b_ref[...])


def _even_out(y_ssm, y_att, w1, w2, x, g, b):
    m = x.shape[0]
    tm = min(256, m)
    row = lambda i: (i, 0)
    fix = lambda i: (0, 0)
    return pl.pallas_call(
        _even_out_kernel,
        grid=(m // tm,),
        in_specs=[
            pl.BlockSpec((tm, y_ssm.shape[1]), row),
            pl.BlockSpec((tm, y_att.shape[1]), row),
            pl.BlockSpec(w1.shape, fix),
            pl.BlockSpec(w2.shape, fix),
            pl.BlockSpec((tm, D_MODEL), row),
            pl.BlockSpec((1, D_MODEL), fix),
            pl.BlockSpec((1, D_MODEL), fix),
        ],
        out_specs=pl.BlockSpec((tm, D_MODEL), row),
        out_shape=jax.ShapeDtypeStruct((m, D_MODEL), F32),
        compiler_params=_cparams(("parallel",)),
        name="even_out_ln",
    )(y_ssm, y_att, w1, w2, x, g, b)


def _ssd_kernel(z_ref, xs_ref, bc_ref, dt_ref, cwx_ref, cbx_ref, cwb_ref, cbb_ref, dtb_ref, alog_ref,
                dsk_ref, nrm_ref, e_ref, y_ref, xs_ext, bc_ext, st_ref):
    c = pl.program_id(0)
    L = SSM_CHUNK
    halo = SUBLANES

    @pl.when(c == 0)
    def _():
        xs_ext[0:halo, :] = jnp.zeros((halo, xs_ext.shape[1]), F32)
        bc_ext[0:halo, :] = jnp.zeros((halo, bc_ext.shape[1]), F32)
        st_ref[...] = jnp.zeros(st_ref.shape, F32)

    xs_ext[halo:halo + L, :] = xs_ref[...]
    bc_ext[halo:halo + L, :] = bc_ref[...]

    def conv(ext, w_ref, b_ref):
        acc = b_ref[...]
        for j in range(SSM_CONV):
            lo = halo - (SSM_CONV - 1) + j
            acc = acc + ext[lo:lo + L, :] * w_ref[j:j + 1, :]
        return acc

    xs = _silu(conv(xs_ext, cwx_ref, cbx_ref))
    bc = _silu(conv(bc_ext, cwb_ref, cbb_ref))
    xs_ext[0:halo, :] = xs_ext[L:L + halo, :]
    bc_ext[0:halo, :] = bc_ext[L:L + halo, :]

    pre = dt_ref[...] + dtb_ref[...]
    dt = jnp.maximum(pre, 0.0) + jnp.log1p(jnp.exp(-jnp.abs(pre)))
    a = dt * (-jnp.exp(alog_ref[...]))
    row = lax.broadcasted_iota(I32, (L, LANES), 0)
    acs = a
    s = 1
    while s < L:
        acs = acs + jnp.where(row >= s, pltpu.roll(acs, s, 0), 0.0)
        s *= 2
    a_last = acs[L - 1:L, :]
    e = e_ref[...]
    dt_x = _expand(dt, e)
    dte_x = _expand(dt * jnp.exp(a_last - acs), e)
    eacs_x = _expand(jnp.exp(acs), e)
    cd_x = _expand(jnp.broadcast_to(jnp.exp(a_last), (SUBLANES, LANES)), e)[0:1, :]
    acs_t = acs.T

    xdt = (xs * dt_x).astype(BF16)
    xd = (xs * dte_x).astype(BF16)
    tri = lax.broadcasted_iota(I32, (L, L), 0) >= lax.broadcasted_iota(I32, (L, L), 1)
    first_head = lax.broadcasted_iota(I32, (L, LANES), 1) < SSM_HEAD_DIM
    n_state = SSM_STATE
    gw = SSM_INNER // SSM_GROUPS
    ys = []
    for g in range(SSM_GROUPS):
        bg = bc[:, g * n_state:(g + 1) * n_state]
        cg = bc[:, SSM_GROUPS * n_state + g * n_state:SSM_GROUPS * n_state + (g + 1) * n_state]
        bb, cb16 = bg.astype(BF16), cg.astype(BF16)
        cbm = _dot_nt(cb16, bb)
        st = st_ref[g]
        y_off = _dot(cb16, st.astype(BF16)) * eacs_x[:, g * gw:(g + 1) * gw]
        st_ref[g] = st * cd_x[:, g * gw:(g + 1) * gw] + _dot(bg.T.astype(BF16), xd[:, g * gw:(g + 1) * gw])
        parts = []
        for j in range(gw // LANES):
            lo = g * gw + j * LANES
            xp = xdt[:, lo:lo + LANES]
            out = None
            for par in (0, 1):
                h = lo // SSM_HEAD_DIM + par
                seg = acs[:, h:h + 1] - acs_t[h:h + 1, :]
                lm = (jnp.exp(jnp.where(tri, seg, -jnp.inf)) * cbm).astype(BF16)
                xm = jnp.where(first_head if par == 0 else jnp.logical_not(first_head), xp, jnp.zeros_like(xp))
                d = _dot(lm, xm)
                out = d if out is None else out + d
            parts.append(out)
        ys.append(jnp.concatenate(parts, axis=1) + y_off)
    y = jnp.concatenate(ys, axis=1) + xs * dsk_ref[...]
    y = y * _silu(z_ref[...])
    outs = []
    for g in range(SSM_GROUPS):
        yg = y[:, g * gw:(g + 1) * gw]
        ms = jnp.mean(yg * yg, axis=-1, keepdims=True)
        outs.append(yg * lax.rsqrt(ms + NORM_EPS))
    y_ref[...] = (jnp.concatenate(outs, axis=1) * nrm_ref[...]).astype(BF16)


def _ssd(xp, cw, cb, dtb, alog, dsk, nrm, e):
    s = xp.shape[0]
    L = SSM_CHUNK
    bcw = 2 * SSM_GROUPS * SSM_STATE
    cwx, cwb = cw[:, :SSM_INNER], cw[:, SSM_INNER:]
    cbx, cbb = cb[:, :SSM_INNER], cb[:, SSM_INNER:]
    fix = lambda i: (0, 0)
    return pl.pallas_call(
        _ssd_kernel,
        grid=(s // L,),
        in_specs=[
            pl.BlockSpec((L, SSM_INNER), lambda i: (i, EV_Z // SSM_INNER)),
            pl.BlockSpec((L, SSM_INNER), lambda i: (i, EV_XS // SSM_INNER)),
            pl.BlockSpec((L, bcw), lambda i: (i, EV_BC // bcw)),
            pl.BlockSpec((L, LANES), lambda i: (i, EV_DT // LANES)),
            pl.BlockSpec(cwx.shape, fix), pl.BlockSpec(cbx.shape, fix),
            pl.BlockSpec(cwb.shape, fix), pl.BlockSpec(cbb.shape, fix),
            pl.BlockSpec((1, LANES), fix), pl.BlockSpec((1, LANES), fix),
            pl.BlockSpec((1, SSM_INNER), fix), pl.BlockSpec((1, SSM_INNER), fix),
            pl.BlockSpec(e.shape, fix),
        ],
        out_specs=pl.BlockSpec((L, SSM_INNER), lambda i: (i, 0)),
        out_shape=jax.ShapeDtypeStruct((s, SSM_INNER), BF16),
        scratch_shapes=[
            pltpu.VMEM((L + 2 * SUBLANES, SSM_INNER), F32),
            pltpu.VMEM((L + 2 * SUBLANES, bcw), F32),
            pltpu.VMEM((SSM_GROUPS, SSM_STATE, SSM_INNER // SSM_GROUPS), F32),
        ],
        compiler_params=_cparams(("arbitrary",)),
        name="ssd_scan",
    )(xp, xp, xp, xp, cwx, cbx, cwb, cbb, dtb, alog, dsk, nrm, e)


def _swa_kernel(sink_ref, q_ref, kvc_ref, kvp_ref, cq_ref, sq_ref, cp_ref, sp_ref, o_ref):
    i = pl.program_id(0)
    B = ATTN_BLOCK
    lane = lax.broadcasted_iota(I32, (B, LANES), 1)
    lo_half = lane < ROPE_DIM
    cq, sq = cq_ref[...], sq_ref[...]
    kc = _rope_tile(kvc_ref[:, 0:LANES], cq, sq)
    kp = _rope_tile(kvp_ref[:, 0:LANES], cp_ref[...], sp_ref[...])
    kcat = jnp.concatenate([kp, kc], axis=0)
    kmat = (kcat.astype(BF16), pltpu.roll(kcat, ROPE_DIM, 1).astype(BF16))
    vcat = jnp.concatenate([kvp_ref[:, LANES:2 * LANES], kvc_ref[:, LANES:2 * LANES]], axis=0)
    vrol = pltpu.roll(vcat, ROPE_DIM, 1)
    lane2 = lax.broadcasted_iota(I32, (2 * B, LANES), 1) < ROPE_DIM
    vdup = (jnp.where(lane2, vcat, vrol).astype(BF16), jnp.where(lane2, vrol, vcat).astype(BF16))
    r = lax.broadcasted_iota(I32, (B, 2 * B), 0)
    col = lax.broadcasted_iota(I32, (B, 2 * B), 1)
    mask = (col > r) & (col <= r + B) & ((i > 0) | (col >= B))
    scale = ROPE_DIM ** -0.5
    hpg = SWA_Q_HEADS // SWA_KV_HEADS
    for j in range(SWA_Q_HEADS // 2):
        g = (2 * j) // hpg
        qt = _rope_tile(q_ref[:, j * LANES:(j + 1) * LANES], cq, sq)
        outs = []
        for par in (0, 1):
            h = 2 * j + par
            qm = jnp.where(lo_half if par == 0 else jnp.logical_not(lo_half), qt, 0.0).astype(BF16)
            logit = _dot_nt(qm, kmat[0] if par == g else kmat[1]) * scale
            logit = jnp.where(mask, logit, -jnp.inf)
            sink = sink_ref[h]
            m = jnp.maximum(jnp.max(logit, axis=-1, keepdims=True), sink)
            ex = jnp.exp(logit - m)
            prob = ex / (jnp.sum(ex, axis=-1, keepdims=True) + jnp.exp(sink - m))
            outs.append(_dot(prob.astype(BF16), vdup[g]))
        o_ref[:, j * LANES:(j + 1) * LANES] = jnp.where(lo_half, outs[0], outs[1]).astype(BF16)


def _swa(xp, sinks, cos_t, sin_t):
    s = xp.shape[0]
    B = ATTN_BLOCK
    qw = SWA_Q_HEADS * ROPE_DIM
    kvw = 2 * SWA_KV_HEADS * ROPE_DIM
    prev = lambda i: (jnp.maximum(i - 1, 0), 0)
    cur = lambda i: (i, 0)
    return pl.pallas_call(
        _swa_kernel,
        grid=(s // B,),
        in_specs=[
            pl.BlockSpec(memory_space=pltpu.SMEM),
            pl.BlockSpec((B, qw), lambda i: (i, EV_Q // qw)),
            pl.BlockSpec((B, kvw), lambda i: (i, EV_KV // kvw)),
            pl.BlockSpec((B, kvw), lambda i: (jnp.maximum(i - 1, 0), EV_KV // kvw)),
            pl.BlockSpec((B, LANES), cur), pl.BlockSpec((B, LANES), cur),
            pl.BlockSpec((B, LANES), prev), pl.BlockSpec((B, LANES), prev),
        ],
        out_specs=pl.BlockSpec((B, qw), cur),
        out_shape=jax.ShapeDtypeStruct((s, qw), BF16),
        compiler_params=_cparams(("parallel",)),
        name="swa_sink",
    )(sinks, xp, xp, xp, cos_t, sin_t, cos_t, sin_t)


def _dsa_prep_kernel(qn_ref, qr_ref, qi_ref, ckv_ref, kk_ref, wi_ref, wuk_ref, kvn_ref, c_ref, s_ref,
                     qlat_ref, qrope_ref, qidx_ref, ckvn_ref, kr_ref, ki_ref, wis_ref):
    c, s = c_ref[...], s_ref[...]
    lane = lax.broadcasted_iota(I32, c.shape, 1)
    lo_half = lane < ROPE_DIM
    for h in range(MLA_HEADS):
        qn = qn_ref[:, h * MLA_NOPE:(h + 1) * MLA_NOPE].astype(BF16)
        qlat_ref[h] = _dot(qn, wuk_ref[h]).astype(BF16)
    for src, dst in ((qr_ref, qrope_ref), (qi_ref, qidx_ref)):
        for j in range(MLA_HEADS // 2):
            t = _rope_tile(src[:, j * LANES:(j + 1) * LANES], c, s)
            dst[2 * j] = jnp.where(lo_half, t, 0.0).astype(BF16)
            dst[2 * j + 1] = jnp.where(lo_half, pltpu.roll(t, ROPE_DIM, 1), 0.0).astype(BF16)
    kk = _rope_tile(kk_ref[...], c, s)
    kr_ref[...] = jnp.where(lo_half, kk, 0.0).astype(BF16)
    ki_ref[...] = jnp.where(lo_half, pltpu.roll(kk, ROPE_DIM, 1), 0.0).astype(BF16)
    ckv = ckv_ref[...]
    ms = jnp.mean(ckv * ckv, axis=-1, keepdims=True)
    ckvn_ref[...] = (ckv * lax.rsqrt(ms + NORM_EPS) * kvn_ref[...]).astype(BF16)
    wis_ref[...] = wi_ref[...] * (IDX_HEADS ** -0.5 * IDX_DIM ** -0.5)


def _dsa_prep(xp, wuk, kvn, cos_t, sin_t):
    s = xp.shape[0]
    tm = min(256, s)
    H = MLA_HEADS
    fix2 = lambda i: (0, 0)
    hrow = lambda i: (0, i, 0)
    row = lambda i: (i, 0)
    return pl.pallas_call(
        _dsa_prep_kernel,
        grid=(s // tm,),
        in_specs=[
            pl.BlockSpec((tm, 2048), lambda i: (i, OD_QN // 2048)),
            pl.BlockSpec((tm, 1024), lambda i: (i, OD_QR // 1024)),
            pl.BlockSpec((tm, 1024), lambda i: (i, OD_QI // 1024)),
            pl.BlockSpec((tm, MLA_RANK), lambda i: (i, OD_CKV // MLA_RANK)),
            pl.BlockSpec((tm, LANES), lambda i: (i, OD_KK // LANES)),
            pl.BlockSpec((tm, LANES), lambda i: (i, OD_WI // LANES)),
            pl.BlockSpec(wuk.shape, lambda i: (0, 0, 0)),
            pl.BlockSpec((1, MLA_RANK), fix2),
            pl.BlockSpec((tm, LANES), row), pl.BlockSpec((tm, LANES), row),
        ],
        out_specs=[
            pl.BlockSpec((H, tm, MLA_RANK), hrow),
            pl.BlockSpec((H, tm, LANES), hrow),
            pl.BlockSpec((H, tm, LANES), hrow),
            pl.BlockSpec((tm, MLA_RANK), row),
            pl.BlockSpec((tm, LANES), row),
            pl.BlockSpec((tm, LANES), row),
            pl.BlockSpec((tm, LANES), row),
        ],
        out_shape=[
            jax.ShapeDtypeStruct((H, s, MLA_RANK), BF16),
            jax.ShapeDtypeStruct((H, s, LANES), BF16),
            jax.ShapeDtypeStruct((H, s, LANES), BF16),
            jax.ShapeDtypeStruct((s, MLA_RANK), BF16),
            jax.ShapeDtypeStruct((s, LANES), BF16),
            jax.ShapeDtypeStruct((s, LANES), BF16),
            jax.ShapeDtypeStruct((s, LANES), F32),
        ],
        compiler_params=_cparams(("parallel",)),
        name="dsa_prep",
    )(xp, xp, xp, xp, xp, xp, wuk, kvn, cos_t, sin_t)


def _dsa_select_kernel(topk, qi_ref, wi_ref, ki_ref, bias_ref, key_ref, hi_ref, lo_ref):
    i = pl.program_id(0)
    Q, KC = DSA_SEL_Q, DSA_KC
    H = IDX_HEADS
    n_chunks = bias_ref.shape[0]
    n_vis = ((i + 1) * Q + KC - 1) // KC
    wi = wi_ref[...]
    wcols = [wi[:, h:h + 1] for h in range(H)]
    qpos = i * Q + lax.broadcasted_iota(I32, (Q, KC), 0)
    kloc = lax.broadcasted_iota(I32, (Q, KC), 1)
    HG = DSA_SEL_HEADS_PER_DOT
    HALF = 32768
    ONE16, ZERO16 = jnp.int16(1), jnp.int16(0)

    def score_chunk(c, carry):
        k = ki_ref[pl.ds(pl.multiple_of(c * KC, KC), KC), :]
        acc = jnp.zeros((Q, KC), F32)
        for g in range(H // HG):
            sc = _dot_nt(qi_ref[g * HG:(g + 1) * HG].reshape(HG * Q, LANES), k)
            for hh in range(HG):
                acc = acc + jnp.maximum(sc[hh * Q:(hh + 1) * Q, :], 0.0) * wcols[g * HG + hh]
        acc = jnp.where(c * KC + kloc <= qpos, acc, -jnp.inf)
        bits = pltpu.bitcast(acc, I32)
        key = bits ^ ((bits >> 31) & jnp.int32(0x7FFFFFFF))
        key_ref[c] = key
        hi_ref[c] = (key >> 16).astype(I16)
        lo_ref[c] = ((key & jnp.int32(0xFFFF)) - HALF).astype(I16)
        return carry

    lax.fori_loop(0, n_vis, score_chunk, 0)

    def count_ge(ref, cand):
        cand16 = jnp.broadcast_to(cand, (Q, LANES)).astype(I16)

        def body(c, cnt):
            v = ref[c]
            for b in range(KC // LANES):
                cnt = cnt + jnp.where(v[:, b * LANES:(b + 1) * LANES] >= cand16, ONE16, ZERO16)
            return cnt
        cnt = lax.fori_loop(0, n_vis, body, jnp.zeros((Q, LANES), I16))
        return jnp.sum(cnt.astype(I32).astype(F32), axis=-1, keepdims=True)

    def search16(ref, need):
        thr = jnp.where(count_ge(ref, jnp.zeros((Q, 1), I32)) >= need, 0, -HALF).astype(I32)

        def bit_step(b, thr):
            cand = thr | (jnp.int32(1) << (14 - b))
            return jnp.where(count_ge(ref, cand) >= need, cand, thr)

        return lax.fori_loop(0, 15, bit_step, thr)

    need = jnp.full((Q, 1), float(topk), F32)
    p_hi = search16(hi_ref, need)
    above = count_ge(hi_ref, p_hi + 1)
    p_hi16 = jnp.broadcast_to(p_hi, (Q, LANES)).astype(I16)

    def keep_ties(c, carry):
        hi, lo = hi_ref[c], lo_ref[c]
        for b in range(KC // LANES):
            sl = slice(b * LANES, (b + 1) * LANES)
            lo_ref[c, :, sl] = jnp.where(hi[:, sl] == p_hi16, lo[:, sl], jnp.int16(-HALF))
        return carry

    lax.fori_loop(0, n_vis, keep_ties, 0)
    p_lo = search16(lo_ref, need - above)
    thr = p_hi * (2 * HALF) + (p_lo + HALF)

    def emit(c, carry):
        sel = (key_ref[c] >= thr) & (c * KC + kloc <= qpos)
        bias_ref[c] = jnp.where(sel, 0.0, MASK_NEG)
        return carry

    lax.fori_loop(0, n_vis, emit, 0)

    def fill(c, carry):
        bias_ref[c] = jnp.full((Q, KC), MASK_NEG, F32)
        return carry

    lax.fori_loop(n_vis, n_chunks, fill, 0)


def _dsa_select(qidx, wis, ki, topk):
    H, s, _ = qidx.shape
    Q, KC = DSA_SEL_Q, DSA_KC
    nch = s // KC
    return pl.pallas_call(
        functools.partial(_dsa_select_kernel, topk),
        grid=(s // Q,),
        in_specs=[
            pl.BlockSpec((H, Q, LANES), lambda i: (0, i, 0)),
            pl.BlockSpec((Q, LANES), lambda i: (i, 0)),
            pl.BlockSpec((s, LANES), lambda i: (0, 0)),
        ],
        out_specs=pl.BlockSpec((nch, Q, KC), lambda i: (0, i, 0)),
        out_shape=jax.ShapeDtypeStruct((nch, s, KC), F32),
        scratch_shapes=[pltpu.VMEM((nch, Q, KC), I32), pltpu.VMEM((nch, Q, KC), I16), pltpu.VMEM((nch, Q, KC), I16)],
        compiler_params=_cparams(("parallel",)),
        name="dsa_select",
    )(qidx, wis, ki)


def _dsa_attn_kernel(ql_ref, qr_ref, bias_ref, ckv_ref, kr_ref, o_ref, m_ref, l_ref, acc_ref, s_ref):
    i = pl.program_id(0)
    Q, KC, H = DSA_ATT_Q, DSA_KC, MLA_HEADS
    n_vis = ((i + 1) * Q + KC - 1) // KC
    n_chunks = bias_ref.shape[0]
    m_ref[...] = jnp.full(m_ref.shape, -jnp.inf, F32)
    l_ref[...] = jnp.zeros(l_ref.shape, F32)
    acc_ref[...] = jnp.zeros(acc_ref.shape, F32)

    def keys(c):
        off = pl.multiple_of(c * KC, KC)
        return ckv_ref[pl.ds(off, KC), :], kr_ref[pl.ds(off, KC), :]

    def raw_logits(c):
        ck, kr = keys(c)
        ql = ql_ref[...].reshape(H * Q, MLA_RANK)
        qr = qr_ref[...].reshape(H * Q, LANES)
        return _dot_nt(ql, ck) + _dot_nt(qr, kr)

    def consume(c, slot):
        logit = s_ref[slot] * MLA_SCALE
        logit = (logit.reshape(H, Q, KC) + bias_ref[c][None]).reshape(H * Q, KC)
        m_old = m_ref[...]
        m_new = jnp.maximum(m_old, jnp.max(logit, axis=-1, keepdims=True))
        alpha = jnp.exp(m_old - m_new)
        p = jnp.exp(logit - m_new)
        l_ref[...] = alpha * l_ref[...] + jnp.sum(p, axis=-1, keepdims=True)
        acc_ref[...] = alpha * acc_ref[...] + _dot(p.astype(BF16), keys(c)[0])
        m_ref[...] = m_new

    s_ref[0] = raw_logits(0)

    def pair(j, carry):
        c = 2 * j
        s_ref[1] = raw_logits(c + 1)
        consume(c, 0)
        s_ref[0] = raw_logits(jnp.minimum(c + 2, n_chunks - 1))
        consume(c + 1, 1)
        return carry

    lax.fori_loop(0, (n_vis + 1) // 2, pair, 0)
    o_ref[...] = (acc_ref[...] / l_ref[...]).reshape(H, Q, MLA_RANK).astype(BF16)


def _dsa_attn(qlat, qrope, bias, ckvn, kr):
    H, s, _ = qlat.shape
    Q, KC = DSA_ATT_Q, DSA_KC
    nch = s // KC
    assert nch % 2 == 0
    hrow = lambda i: (0, i, 0)
    fix = lambda i: (0, 0)
    return pl.pallas_call(
        _dsa_attn_kernel,
        grid=(s // Q,),
        in_specs=[
            pl.BlockSpec((H, Q, MLA_RANK), hrow),
            pl.BlockSpec((H, Q, LANES), hrow),
            pl.BlockSpec((nch, Q, KC), hrow),
            pl.BlockSpec((s, MLA_RANK), fix),
            pl.BlockSpec((s, LANES), fix),
        ],
        out_specs=pl.BlockSpec((H, Q, MLA_RANK), hrow),
        out_shape=jax.ShapeDtypeStruct((H, s, MLA_RANK), BF16),
        scratch_shapes=[
            pltpu.VMEM((H * Q, 1), F32),
            pltpu.VMEM((H * Q, 1), F32),
            pltpu.VMEM((H * Q, MLA_RANK), F32),
            pltpu.VMEM((2, H * Q, KC), F32),
        ],
        compiler_params=_cparams(("parallel",)),
        name="dsa_attn",
    )(qlat, qrope, bias, ckvn, kr)


def _odd_out_kernel(ol_ref, wuv_ref, wo_ref, x_ref, g_ref, b_ref, o_ref, u_ref):
    for h in range(MLA_HEADS):
        u_ref[:, h * MLA_V:(h + 1) * MLA_V] = _dot(ol_ref[h], wuv_ref[h]).astype(BF16)
    mix = _dot(u_ref[...], wo_ref[...])
    o_ref[...] = _layer_norm(DEEPNORM_ALPHA * x_ref[...] + mix, g_ref[...], b_ref[...])


def _odd_out(olat, wuv, wo, x, g, b):
    H, s, _ = olat.shape
    tm = min(256, s)
    row = lambda i: (i, 0)
    fix = lambda i: (0, 0)
    return pl.pallas_call(
        _odd_out_kernel,
        grid=(s // tm,),
        in_specs=[
            pl.BlockSpec((H, tm, MLA_RANK), lambda i: (0, i, 0)),
            pl.BlockSpec(wuv.shape, lambda i: (0, 0, 0)),
            pl.BlockSpec(wo.shape, fix),
            pl.BlockSpec((tm, D_MODEL), row),
            pl.BlockSpec((1, D_MODEL), fix), pl.BlockSpec((1, D_MODEL), fix),
        ],
        out_specs=pl.BlockSpec((tm, D_MODEL), row),
        out_shape=jax.ShapeDtypeStruct((s, D_MODEL), F32),
        scratch_shapes=[pltpu.VMEM((tm, MLA_HEADS * MLA_V), BF16)],
        compiler_params=_cparams(("parallel",)),
        name="odd_out_ln",
    )(olat, wuv, wo, x, g, b)


ROUTE_LANE0 = MOE_GROUPS


def _router_kernel(h_ref, wr_ref, br_ref, info_ref, cnt_ref, run_ref):
    i = pl.program_id(0)

    @pl.when(i == 0)
    def _():
        run_ref[...] = jnp.zeros(run_ref.shape, F32)

    h = h_ref[...]
    tm = h.shape[0]
    h_hi = h.astype(BF16)
    h_lo = (h - h_hi.astype(F32)).astype(BF16)
    w = wr_ref[...]
    w_hi = w.astype(BF16)
    w_lo = (w - w_hi.astype(F32)).astype(BF16)
    logits = _dot(h_hi, w_hi) + (_dot(h_hi, w_lo) + _dot(h_lo, w_hi)) + br_ref[...]
    lane = lax.broadcasted_iota(I32, (tm, LANES), 1)
    lane_f = lane.astype(F32)
    neg = -jnp.inf
    big = float(LANES)
    is_grp = lane < MOE_GROUPS
    gl = jnp.where(is_grp, logits, neg)
    gmax = jnp.max(gl, axis=-1, keepdims=True)
    gsel = jnp.min(jnp.where(gl == gmax, lane_f, big), axis=-1, keepdims=True)
    gsum = jnp.sum(jnp.where(is_grp, jnp.exp(logits - gmax), 0.0), axis=-1, keepdims=True)
    egrp = ((lane - ROUTE_LANE0) >> 3).astype(F32)
    valid = (lane >= ROUTE_LANE0) & (lane < ROUTE_LANE0 + MOE_EXPERTS) & (egrp == gsel)
    el = jnp.where(valid, logits, neg)
    v1 = jnp.max(el, axis=-1, keepdims=True)
    i1 = jnp.min(jnp.where(el == v1, lane_f, big), axis=-1, keepdims=True)
    el2 = jnp.where(lane_f == i1, neg, el)
    v2 = jnp.max(el2, axis=-1, keepdims=True)
    i2 = jnp.min(jnp.where(el2 == v2, lane_f, big), axis=-1, keepdims=True)
    t = jnp.exp(v2 - v1)
    p1 = 1.0 / (1.0 + t)
    p2 = t / (1.0 + t)
    ggate = 1.0 / gsum
    m1 = lane_f == i1
    m2 = lane_f == i2
    memb = jnp.where(m1 | m2, 1.0, 0.0)
    tri = (lax.broadcasted_iota(I32, (tm, tm), 0) > lax.broadcasted_iota(I32, (tm, tm), 1))
    cum = _dot(jnp.where(tri, 1.0, 0.0).astype(BF16), memb.astype(BF16)) + run_ref[...]
    rank1 = jnp.sum(jnp.where(m1, cum, 0.0), axis=-1, keepdims=True)
    rank2 = jnp.sum(jnp.where(m2, cum, 0.0), axis=-1, keepdims=True)
    run_ref[...] = run_ref[...] + jnp.sum(memb, axis=0, keepdims=True)
    info = jnp.where(lane == 0, i1 - ROUTE_LANE0, 0.0)
    info = jnp.where(lane == 1, i2 - ROUTE_LANE0, info)
    info = jnp.where(lane == 2, p1 * ggate, info)
    info = jnp.where(lane == 3, p2 * ggate, info)
    info = jnp.where(lane == 4, rank1, info)
    info = jnp.where(lane == 5, rank2, info)
    info_ref[...] = info
    cnt_ref[...] = run_ref[...]


def _router(h, wr, br):
    t = h.shape[0]
    tm = min(512, t)
    return pl.pallas_call(
        _router_kernel,
        grid=(t // tm,),
        in_specs=[
            pl.BlockSpec((tm, D_MODEL), lambda i: (i, 0)),
            pl.BlockSpec(wr.shape, lambda i: (0, 0)),
            pl.BlockSpec((1, LANES), lambda i: (0, 0)),
        ],
        out_specs=[pl.BlockSpec((tm, LANES), lambda i: (i, 0)), pl.BlockSpec((1, LANES), lambda i: (0, 0))],
        out_shape=[jax.ShapeDtypeStruct((t, LANES), F32), jax.ShapeDtypeStruct((1, LANES), F32)],
        scratch_shapes=[pltpu.VMEM((1, LANES), F32)],
        compiler_params=_cparams(("arbitrary",)),
        name="moe_router",
    )(h, wr, br)


def _plan_kernel(info_ref, ps_ref, pos_ref):
    info = info_ref[...]
    lane = lax.broadcasted_iota(I32, info.shape, 1)
    lane_f = lane.astype(F32)
    ps = ps_ref[...]
    pos1 = jnp.sum(jnp.where(lane_f == info[:, 0:1], ps, 0.0), axis=-1, keepdims=True) + info[:, 4:5]
    pos2 = jnp.sum(jnp.where(lane_f == info[:, 1:2], ps, 0.0), axis=-1, keepdims=True) + info[:, 5:6]
    pos_ref[...] = jnp.where(lane == 0, pos1, jnp.where(lane == 1, pos2, 0.0)).astype(I32)


def _plan(info, pad_start):
    t = info.shape[0]
    tm = min(1024, t)
    return pl.pallas_call(
        _plan_kernel,
        grid=(t // tm,),
        in_specs=[pl.BlockSpec((tm, LANES), lambda i: (i, 0)), pl.BlockSpec((1, LANES), lambda i: (0, 0))],
        out_specs=pl.BlockSpec((tm, LANES), lambda i: (i, 0)),
        out_shape=jax.ShapeDtypeStruct((t, LANES), I32),
        compiler_params=_cparams(("parallel",)),
        name="moe_plan",
    )(info, pad_start)


def _invert_kernel(pos_ref, rt_ref):
    def clear(r, carry):
        rt_ref[r] = 0
        return carry

    lax.fori_loop(0, rt_ref.shape[0], clear, 0, unroll=8)

    def put(n, carry):
        rt_ref[pos_ref[n]] = lax.shift_right_logical(n, 1)
        return carry

    lax.fori_loop(0, pos_ref.shape[0], put, 0, unroll=8)


def _invert(pos_flat, n_rows):
    return pl.pallas_call(
        _invert_kernel,
        in_specs=[pl.BlockSpec(memory_space=pltpu.SMEM)],
        out_specs=pl.BlockSpec(memory_space=pltpu.SMEM),
        out_shape=jax.ShapeDtypeStruct((n_rows,), I32),
        name="moe_invert",
    )(pos_flat)


def _expert_kernel(be_ref, nu_ref, rt_ref, h_ref, wg_ref, wu_ref, wd_ref, o_ref, xbuf0, xbuf1, wgb, wub, wdb, sem):
    b = pl.program_id(0)
    n_used = nu_ref[0]
    R = EXPERT_ROWS

    bufs = (xbuf0, xbuf1)

    def row_copy(tok, r, slot):
        return pltpu.make_async_copy(h_ref.at[pl.ds(tok, 1)], bufs[slot].at[pl.ds(r, 1)], sem.at[slot])

    def drain(slot):
        def body(r, carry):
            row_copy(0, 0, slot).wait()
            return carry
        lax.fori_loop(0, R, body, 0, unroll=8)

    @pl.when(b == 0)
    def _():
        def body(r, carry):
            row_copy(rt_ref[r], r, 0).start()
            return carry
        lax.fori_loop(0, R, body, 0, unroll=8)

    def block(slot):
        drain(slot)

        @pl.when((b == 0) | (be_ref[b] != be_ref[jnp.maximum(b - 1, 0)]))
        def _():
            wgb[...] = wg_ref[0, 0].astype(BF16)
            wub[...] = wu_ref[0, 0].astype(BF16)
            wdb[...] = wd_ref[0, 0].astype(BF16)

        base = (b + 1) * R
        for r in range(R):
            row_copy(rt_ref[base + r], r, 1 - slot).start()
        x = bufs[slot][...].astype(BF16)
        hid = _silu(_dot(x, wgb[...])) * _dot(x, wub[...])
        o_ref[...] = _dot(hid.astype(BF16), wdb[...])

    for parity in (0, 1):
        pl.when((b < n_used) & (b % 2 == parity))(functools.partial(block, parity))

    @pl.when(b >= n_used)
    def _():
        o_ref[...] = jnp.zeros(o_ref.shape, F32)

    for parity in (0, 1):
        pl.when((b == n_used) & (b % 2 == parity))(functools.partial(drain, parity))


def _experts(blk_e, n_used, row_token, h, wg, wu, wd, layer):
    r = row_token.shape[0]
    nb = r // EXPERT_ROWS
    wmap = lambda b, be, nu, rt: (layer, be[b], 0, 0)
    return pl.pallas_call(
        _expert_kernel,
        grid_spec=pltpu.PrefetchScalarGridSpec(
            num_scalar_prefetch=3,
            grid=(nb,),
            in_specs=[
                pl.BlockSpec(memory_space=pl.ANY),
                pl.BlockSpec((1, 1, D_MODEL, MOE_FF), wmap),
                pl.BlockSpec((1, 1, D_MODEL, MOE_FF), wmap),
                pl.BlockSpec((1, 1, MOE_FF, D_MODEL), wmap),
            ],
            out_specs=pl.BlockSpec((EXPERT_ROWS, D_MODEL), lambda b, be, nu, rt: (b, 0)),
            scratch_shapes=[
                pltpu.VMEM((EXPERT_ROWS, D_MODEL), F32),
                pltpu.VMEM((EXPERT_ROWS, D_MODEL), F32),
                pltpu.VMEM((D_MODEL, MOE_FF), BF16),
                pltpu.VMEM((D_MODEL, MOE_FF), BF16),
                pltpu.VMEM((MOE_FF, D_MODEL), BF16),
                pltpu.SemaphoreType.DMA((2,)),
            ],
        ),
        out_shape=jax.ShapeDtypeStruct((r, D_MODEL), F32),
        compiler_params=_cparams(("arbitrary",)),
        name="moe_experts",
    )(blk_e, n_used, row_token, h, wg, wu, wd)


COMBINE_TOKENS = 256


def _combine_ple_kernel(pos_ref, y_ref, info_ref, x_ref, g_ref, b_ref, p_ref, wg_ref, bg_ref, wp_ref, o_ref,
                        buf0, buf1, sem):
    i = pl.program_id(0)
    n = pl.num_programs(0)
    TB = x_ref.shape[0]
    bufs = (buf0, buf1)

    def row_copy(src_row, k, tt, slot):
        return pltpu.make_async_copy(y_ref.at[pl.ds(src_row, 1)], bufs[slot].at[k, pl.ds(tt, 1)], sem.at[slot])

    def drain(slot):
        def body(tt, carry):
            row_copy(0, 0, 0, slot).wait()
            row_copy(0, 1, 0, slot).wait()
            return carry
        lax.fori_loop(0, TB, body, 0, unroll=8)

    @pl.when(i == 0)
    def _():
        def body(tt, carry):
            row_copy(pos_ref[2 * tt], 0, tt, 0).start()
            row_copy(pos_ref[2 * tt + 1], 1, tt, 0).start()
            return carry
        lax.fori_loop(0, TB, body, 0, unroll=8)

    def block(slot):
        drain(slot)
        base = jnp.minimum(i + 1, n - 1) * (2 * TB)
        for tt in range(TB):
            row_copy(pos_ref[base + 2 * tt], 0, tt, 1 - slot).start()
            row_copy(pos_ref[base + 2 * tt + 1], 1, tt, 1 - slot).start()
        info = info_ref[...]
        y = bufs[slot][0] * info[:, 2:3] + bufs[slot][1] * info[:, 3:4]
        x2 = _layer_norm(DEEPNORM_ALPHA * x_ref[...] + y, g_ref[...], b_ref[...])
        gate = jax.nn.sigmoid(_dot(x2.astype(BF16), wg_ref[...]) + bg_ref[...])
        o_ref[...] = x2 + gate * _dot(p_ref[...].astype(BF16), wp_ref[...])

    for parity in (0, 1):
        pl.when(i % 2 == parity)(functools.partial(block, parity))
    for parity in (0, 1):
        pl.when((i == n - 1) & (i % 2 == parity))(functools.partial(drain, 1 - parity))


def _combine_ple(pos_flat, y_rows, info, x, g, b, p, wg, bg, wp):
    t = x.shape[0]
    TB = min(COMBINE_TOKENS, t)
    row = lambda i, pos: (i, 0)
    fix = lambda i, pos: (0, 0)
    return pl.pallas_call(
        _combine_ple_kernel,
        grid_spec=pltpu.PrefetchScalarGridSpec(
            num_scalar_prefetch=1,
            grid=(t // TB,),
            in_specs=[
                pl.BlockSpec(memory_space=pl.ANY),
                pl.BlockSpec((TB, LANES), row),
                pl.BlockSpec((TB, D_MODEL), row),
                pl.BlockSpec((1, D_MODEL), fix), pl.BlockSpec((1, D_MODEL), fix),
                pl.BlockSpec((TB, PLE_DIM), row),
                pl.BlockSpec(wg.shape, fix),
                pl.BlockSpec((1, D_MODEL), fix),
                pl.BlockSpec(wp.shape, fix),
            ],
            out_specs=pl.BlockSpec((TB, D_MODEL), row),
            scratch_shapes=[
                pltpu.VMEM((2, TB, D_MODEL), F32),
                pltpu.VMEM((2, TB, D_MODEL), F32),
                pltpu.SemaphoreType.DMA((2,)),
            ],
        ),
        out_shape=jax.ShapeDtypeStruct((t, D_MODEL), F32),
        compiler_params=_cparams(("arbitrary",)),
        name="moe_combine_ln_ple",
    )(pos_flat, y_rows, info, x, g, b, p, wg, bg, wp)


def _hier_moe_ln_ple(x, wr, br, wg, wu, wd, layer, ln_g, ln_b, p, ple_wg, ple_bg, ple_wp):
    t = x.shape[0]
    info, cnt = _router(x, wr, br)
    counts = cnt[0, ROUTE_LANE0:ROUTE_LANE0 + MOE_EXPERTS].astype(I32)
    padded = (counts + EXPERT_ROWS - 1) // EXPERT_ROWS * EXPERT_ROWS
    pad_end = jnp.cumsum(padded)
    pad_start = jnp.zeros((1, LANES), F32).at[0, :MOE_EXPERTS].set((pad_end - padded).astype(F32))
    n_rows = 2 * t + MOE_EXPERTS * EXPERT_ROWS
    n_blocks = n_rows // EXPERT_ROWS
    blk_start = jnp.arange(n_blocks, dtype=I32) * EXPERT_ROWS
    blk_e = jnp.minimum(jnp.sum(pad_end[None, :] <= blk_start[:, None], axis=1), MOE_EXPERTS - 1).astype(I32)
    n_used = (pad_end[-1:] // EXPERT_ROWS).astype(I32)
    pos = _plan(info, pad_start)[:, :2].reshape(-1)
    y_rows = _experts(blk_e, n_used, _invert(pos, n_rows), x, wg, wu, wd, layer)
    return _combine_ple(pos, y_rows, info, x, ln_g, ln_b, p, ple_wg, ple_bg, ple_wp)


def _rope_tables(positions):
    inv = ROPE_THETA ** (-jnp.arange(0, ROPE_DIM, 2, dtype=F32) / ROPE_DIM)
    ang = positions.astype(F32)[:, None] * inv
    cos, sin = jnp.cos(ang), jnp.sin(ang)
    cos_t = jnp.tile(cos, (1, LANES // (ROPE_DIM // 2)))
    sin_t = jnp.tile(jnp.concatenate([-sin, sin], axis=1), (1, LANES // ROPE_DIM))
    return cos_t, sin_t


def _pad_cols(w, n):
    return jnp.pad(w, ((0, 0), (0, n - w.shape[1])))


def _even_w_in(w):
    z, xbc, dt, q, kv = jnp.split(w, [2048, 5120, 5152, 6176], axis=1)
    return _pad_cols(jnp.concatenate([z, xbc, q, kv, dt], axis=1), EV_NP).astype(BF16)


def _odd_w_in(w):
    q, ckv, krope, qi, ki, wi = jnp.split(w, [3072, 3584, 3648, 4672, 4736], axis=1)
    q = q.reshape(D_MODEL, MLA_HEADS, MLA_NOPE + MLA_ROPE)
    qn = q[:, :, :MLA_NOPE].reshape(D_MODEL, -1)
    qr = q[:, :, MLA_NOPE:].reshape(D_MODEL, -1)
    return _pad_cols(jnp.concatenate([qn, qr, qi, ckv, krope, ki, _pad_cols(wi, LANES)], axis=1), OD_NP).astype(BF16)


def _head_expand_matrix():
    e = np.zeros((LANES, SSM_INNER), np.float32)
    for h in range(SSM_HEADS):
        e[h, h * SSM_HEAD_DIM:(h + 1) * SSM_HEAD_DIM] = 1.0
    return jnp.asarray(e, BF16)


def kernel(x, p, positions, ev_w_in, ev_conv_w, ev_conv_b, ev_dt_bias, ev_a_log, ev_d_skip, ev_ssm_norm, ev_sinks, ev_w_out, od_w_in, od_kv_norm, od_w_uk, od_w_uv, od_w_out, ln1_g, ln1_b, ln2_g, ln2_b, moe_router_group, moe_router_group_b, moe_router_expert, moe_router_expert_b, moe_w_gate, moe_w_up, moe_w_down, ple_w_proj, ple_w_gate, ple_b_gate):
    batch, s, d = x.shape
    assert batch == 1 and d == D_MODEL
    xs = x[0]
    cos_t, sin_t = _rope_tables(positions[0])
    e_mat = _head_expand_matrix()
    topk = min(IDX_TOPK_MAX, s // 4)
    row = lambda v: v.reshape(1, -1)
    pad_row = lambda v: _pad_cols(v.reshape(1, -1), LANES)
    for i in range(DEPTH):
        j = i // 2
        if i % 2 == 0:
            xp = _inproj(xs, _even_w_in(ev_w_in[j]))
            y_ssm = _ssd(xp, ev_conv_w[j], row(ev_conv_b[j]), pad_row(ev_dt_bias[j]), pad_row(ev_a_log[j]),
                         row(jnp.repeat(ev_d_skip[j], SSM_HEAD_DIM)), row(ev_ssm_norm[j]), e_mat)
            y_att = _swa(xp, ev_sinks[j], cos_t, sin_t)
            w_out = ev_w_out[j].astype(BF16)
            xs = _even_out(y_ssm, y_att, w_out[:SSM_INNER], w_out[SSM_INNER:], xs, row(ln1_g[i]), row(ln1_b[i]))
        else:
            xp = _inproj(xs, _odd_w_in(od_w_in[j]))
            qlat, qrope, qidx, ckvn, kr, ki, wis = _dsa_prep(
                xp, od_w_uk[j].astype(BF16), row(od_kv_norm[j]), cos_t, sin_t)
            bias = _dsa_select(qidx, wis, ki, topk)
            olat = _dsa_attn(qlat, qrope, bias, ckvn, kr)
            xs = _odd_out(olat, od_w_uv[j].astype(BF16), od_w_out[j].astype(BF16), xs, row(ln1_g[i]), row(ln1_b[i]))
        wr = _pad_cols(jnp.concatenate([moe_router_group[i], moe_router_expert[i]], axis=1), LANES)
        br = pad_row(jnp.concatenate([moe_router_group_b[i], moe_router_expert_b[i]]))
        xs = _hier_moe_ln_ple(xs, wr, br, moe_w_gate, moe_w_up, moe_w_down, i, row(ln2_g[i]), row(ln2_b[i]),
                              p[i, 0], ple_w_gate[i].astype(BF16), row(ple_b_gate[i]), ple_w_proj[i].astype(BF16))
    return xs[None]
```

```python
import functools

import jax
import jax.numpy as jnp
import numpy as np
from jax import lax
from jax.experimental import pallas as pl
from jax.experimental.pallas import tpu as pltpu

F32 = jnp.float32
BF16 = jnp.bfloat16
I32 = jnp.int32
I16 = jnp.int16

D_MODEL = 2048
DEPTH = 4
ROPE_THETA = 10000.0
ROPE_DIM = 64
NORM_EPS = 1e-5
SSM_HEADS = 32
SSM_HEAD_DIM = 64
SSM_INNER = SSM_HEADS * SSM_HEAD_DIM
SSM_GROUPS = 4
SSM_STATE = 128
SSM_CONV = 4
SSM_CHUNK = 128
SWA_Q_HEADS = 16
SWA_KV_HEADS = 2
ATTN_BLOCK = 128
MLA_HEADS = 16
MLA_NOPE = 128
MLA_ROPE = ROPE_DIM
MLA_V = 128
MLA_RANK = 512
MLA_SCALE = (MLA_NOPE + MLA_ROPE) ** -0.5
IDX_HEADS = 16
IDX_DIM = ROPE_DIM
IDX_TOPK_MAX = 256
MOE_GROUPS = 4
MOE_EPG = 8
MOE_EXPERTS = MOE_GROUPS * MOE_EPG
MOE_FF = 512
PLE_DIM = 256
DEEPNORM_ALPHA = (2 * DEPTH) ** 0.25

LANES = 128
ROW_TILES = D_MODEL // LANES
SUBLANES = 8
VMEM_LIMIT_BYTES = 56 * 1024 * 1024

EXPERT_ROWS = 256
GATHER_DMA_PRIORITY = 1
DSA_SEL_Q = 128
DSA_ATT_Q = 64
DSA_KC = 512
DSA_SEL_HEADS_PER_DOT = 4
MASK_NEG = -1e30

EV_Z, EV_XS, EV_BC, EV_Q, EV_KV, EV_DT = 0, 2048, 4096, 5120, 6144, 6400
EV_NP = 6656
OD_QN, OD_QR, OD_QI, OD_CKV, OD_KK, OD_WI = 0, 2048, 3072, 4096, 4608, 4736
OD_NP = 5120


def _cparams(sem, vmem=VMEM_LIMIT_BYTES):
    return pltpu.CompilerParams(dimension_semantics=sem, vmem_limit_bytes=vmem)


def _dot(a, b):
    return jnp.dot(a, b, preferred_element_type=F32)


def _dot_nt(a, b):
    return lax.dot_general(a, b, (((1,), (1,)), ((), ())), preferred_element_type=F32)


def _split3(v):
    hi = v.astype(BF16)
    r = v - hi.astype(F32)
    mid = r.astype(BF16)
    lo = (r - mid.astype(F32)).astype(BF16)
    return hi, mid, lo


def _expand(v, e):
    hi, mid, lo = _split3(v)
    return _dot(hi, e) + _dot(mid, e) + _dot(lo, e)


def _silu(v):
    return v * jax.nn.sigmoid(v)


def _layer_norm(v, g, b):
    mu = jnp.mean(v, axis=-1, keepdims=True)
    vc = v - mu
    var = jnp.mean(vc * vc, axis=-1, keepdims=True)
    return vc * lax.rsqrt(var + NORM_EPS) * g + b


def _rope_tile(t, c, s):
    lane = lax.broadcasted_iota(I32, t.shape, 1)
    first_half = (lane & 32) == 0
    swapped = jnp.where(first_half, pltpu.roll(t, LANES - 32, 1), pltpu.roll(t, 32, 1))
    return t * c + swapped * s


def _inproj_kernel(x_ref, w_ref, o_ref):
    o_ref[...] = _dot(x_ref[...].astype(BF16), w_ref[...])


def _inproj(x, w):
    m, k = x.shape
    n = w.shape[1]
    tm, tn = min(1024, m), 512
    return pl.pallas_call(
        _inproj_kernel,
        grid=(m // tm, n // tn),
        in_specs=[pl.BlockSpec((tm, k), lambda i, j: (i, 0)), pl.BlockSpec((k, tn), lambda i, j: (0, j))],
        out_specs=pl.BlockSpec((tm, tn), lambda i, j: (i, j)),
        out_shape=jax.ShapeDtypeStruct((m, n), F32),
        compiler_params=_cparams(("parallel", "arbitrary")),
        name="inproj",
    )(x, w)


def _store_row_layouts(y, o_ref, of_ref):
    o_ref[...] = y
    for j in range(ROW_TILES):
        of_ref[pl.ds(j, y.shape[0], stride=ROW_TILES), :] = y[:, j * LANES:(j + 1) * LANES]


def _row_layout_outputs(m, tm):
    specs = [pl.BlockSpec((tm, D_MODEL), lambda i: (i, 0)), pl.BlockSpec((tm * ROW_TILES, LANES), lambda i: (i, 0))]
    shapes = [jax.ShapeDtypeStruct((m, D_MODEL), F32), jax.ShapeDtypeStruct((m * ROW_TILES, LANES), F32)]
    return specs, shapes


def _even_out_kernel(a1_ref, a2_ref, w1_ref, w2_ref, x_ref, g_ref, b_ref, o_ref, of_ref):
    mix = _dot(a1_ref[...], w1_ref[...]) + _dot(a2_ref[...], w2_ref[...])
    _store_row_layouts(_layer_norm(DEEPNORM_ALPHA * x_ref[...] + mix, g_ref[...], b_ref[...]), o_ref, of_ref)


def _even_out(y_ssm, y_att, w1, w2, x, g, b):
    m = x.shape[0]
    tm = min(256, m)
    row = lambda i: (i, 0)
    fix = lambda i: (0, 0)
    out_specs, out_shape = _row_layout_outputs(m, tm)
    return pl.pallas_call(
        _even_out_kernel,
        grid=(m // tm,),
        in_specs=[
            pl.BlockSpec((tm, y_ssm.shape[1]), row),
            pl.BlockSpec((tm, y_att.shape[1]), row),
            pl.BlockSpec(w1.shape, fix),
            pl.BlockSpec(w2.shape, fix),
            pl.BlockSpec((tm, D_MODEL), row),
            pl.BlockSpec((1, D_MODEL), fix),
            pl.BlockSpec((1, D_MODEL), fix),
        ],
        out_specs=out_specs,
        out_shape=out_shape,
        compiler_params=_cparams(("parallel",)),
        name="even_out_ln",
    )(y_ssm, y_att, w1, w2, x, g, b)


def _ssd_kernel(z_ref, xs_ref, bc_ref, dt_ref, cwx_ref, cbx_ref, cwb_ref, cbb_ref, dtb_ref, alog_ref,
                dsk_ref, nrm_ref, e_ref, y_ref, xs_ext, bc_ext, st_ref):
    c = pl.program_id(0)
    L = SSM_CHUNK
    halo = SUBLANES

    @pl.when(c == 0)
    def _():
        xs_ext[0:halo, :] = jnp.zeros((halo, xs_ext.shape[1]), F32)
        bc_ext[0:halo, :] = jnp.zeros((halo, bc_ext.shape[1]), F32)
        st_ref[...] = jnp.zeros(st_ref.shape, F32)

    xs_ext[halo:halo + L, :] = xs_ref[...]
    bc_ext[halo:halo + L, :] = bc_ref[...]

    def conv(ext, w_ref, b_ref):
        acc = b_ref[...]
        for j in range(SSM_CONV):
            lo = halo - (SSM_CONV - 1) + j
            acc = acc + ext[lo:lo + L, :] * w_ref[j:j + 1, :]
        return acc

    xs = _silu(conv(xs_ext, cwx_ref, cbx_ref))
    bc = _silu(conv(bc_ext, cwb_ref, cbb_ref))
    xs_ext[0:halo, :] = xs_ext[L:L + halo, :]
    bc_ext[0:halo, :] = bc_ext[L:L + halo, :]

    pre = dt_ref[...] + dtb_ref[...]
    dt = jnp.maximum(pre, 0.0) + jnp.log1p(jnp.exp(-jnp.abs(pre)))
    a = dt * (-jnp.exp(alog_ref[...]))
    row = lax.broadcasted_iota(I32, (L, LANES), 0)
    acs = a
    s = 1
    while s < L:
        acs = acs + jnp.where(row >= s, pltpu.roll(acs, s, 0), 0.0)
        s *= 2
    a_last = acs[L - 1:L, :]
    e = e_ref[...]
    dt_x = _expand(dt, e)
    dte_x = _expand(dt * jnp.exp(a_last - acs), e)
    eacs_x = _expand(jnp.exp(acs), e)
    cd_x = _expand(jnp.broadcast_to(jnp.exp(a_last), (SUBLANES, LANES)), e)[0:1, :]
    acs_t = acs.T

    xdt = (xs * dt_x).astype(BF16)
    xd = (xs * dte_x).astype(BF16)
    tri = lax.broadcasted_iota(I32, (L, L), 0) >= lax.broadcasted_iota(I32, (L, L), 1)
    first_head = lax.broadcasted_iota(I32, (L, LANES), 1) < SSM_HEAD_DIM
    n_state = SSM_STATE
    gw = SSM_INNER // SSM_GROUPS
    ys = []
    for g in range(SSM_GROUPS):
        bg = bc[:, g * n_state:(g + 1) * n_state]
        cg = bc[:, SSM_GROUPS * n_state + g * n_state:SSM_GROUPS * n_state + (g + 1) * n_state]
        bb, cb16 = bg.astype(BF16), cg.astype(BF16)
        cbm = _dot_nt(cb16, bb)
        st = st_ref[g]
        y_off = _dot(cb16, st.astype(BF16)) * eacs_x[:, g * gw:(g + 1) * gw]
        st_ref[g] = st * cd_x[:, g * gw:(g + 1) * gw] + _dot(bg.T.astype(BF16), xd[:, g * gw:(g + 1) * gw])
        parts = []
        for j in range(gw // LANES):
            lo = g * gw + j * LANES
            xp = xdt[:, lo:lo + LANES]
            out = None
            for par in (0, 1):
                h = lo // SSM_HEAD_DIM + par
                seg = acs[:, h:h + 1] - acs_t[h:h + 1, :]
                lm = (jnp.exp(jnp.where(tri, seg, -jnp.inf)) * cbm).astype(BF16)
                xm = jnp.where(first_head if par == 0 else jnp.logical_not(first_head), xp, jnp.zeros_like(xp))
                d = _dot(lm, xm)
                out = d if out is None else out + d
            parts.append(out)
        ys.append(jnp.concatenate(parts, axis=1) + y_off)
    y = jnp.concatenate(ys, axis=1) + xs * dsk_ref[...]
    y = y * _silu(z_ref[...])
    outs = []
    for g in range(SSM_GROUPS):
        yg = y[:, g * gw:(g + 1) * gw]
        ms = jnp.mean(yg * yg, axis=-1, keepdims=True)
        outs.append(yg * lax.rsqrt(ms + NORM_EPS))
    y_ref[...] = (jnp.concatenate(outs, axis=1) * nrm_ref[...]).astype(BF16)


def _ssd(xp, cw, cb, dtb, alog, dsk, nrm, e):
    s = xp.shape[0]
    L = SSM_CHUNK
    bcw = 2 * SSM_GROUPS * SSM_STATE
    cwx, cwb = cw[:, :SSM_INNER], cw[:, SSM_INNER:]
    cbx, cbb = cb[:, :SSM_INNER], cb[:, SSM_INNER:]
    fix = lambda i: (0, 0)
    return pl.pallas_call(
        _ssd_kernel,
        grid=(s // L,),
        in_specs=[
            pl.BlockSpec((L, SSM_INNER), lambda i: (i, EV_Z // SSM_INNER)),
            pl.BlockSpec((L, SSM_INNER), lambda i: (i, EV_XS // SSM_INNER)),
            pl.BlockSpec((L, bcw), lambda i: (i, EV_BC // bcw)),
            pl.BlockSpec((L, LANES), lambda i: (i, EV_DT // LANES)),
            pl.BlockSpec(cwx.shape, fix), pl.BlockSpec(cbx.shape, fix),
            pl.BlockSpec(cwb.shape, fix), pl.BlockSpec(cbb.shape, fix),
            pl.BlockSpec((1, LANES), fix), pl.BlockSpec((1, LANES), fix),
            pl.BlockSpec((1, SSM_INNER), fix), pl.BlockSpec((1, SSM_INNER), fix),
            pl.BlockSpec(e.shape, fix),
        ],
        out_specs=pl.BlockSpec((L, SSM_INNER), lambda i: (i, 0)),
        out_shape=jax.ShapeDtypeStruct((s, SSM_INNER), BF16),
        scratch_shapes=[
            pltpu.VMEM((L + 2 * SUBLANES, SSM_INNER), F32),
            pltpu.VMEM((L + 2 * SUBLANES, bcw), F32),
            pltpu.VMEM((SSM_GROUPS, SSM_STATE, SSM_INNER // SSM_GROUPS), F32),
        ],
        compiler_params=_cparams(("arbitrary",)),
        name="ssd_scan",
    )(xp, xp, xp, xp, cwx, cbx, cwb, cbb, dtb, alog, dsk, nrm, e)


def _swa_kernel(sink_ref, q_ref, kvc_ref, kvp_ref, cq_ref, sq_ref, cp_ref, sp_ref, o_ref):
    i = pl.program_id(0)
    B = ATTN_BLOCK
    lane = lax.broadcasted_iota(I32, (B, LANES), 1)
    lo_half = lane < ROPE_DIM
    cq, sq = cq_ref[...], sq_ref[...]
    kc = _rope_tile(kvc_ref[:, 0:LANES], cq, sq)
    kp = _rope_tile(kvp_ref[:, 0:LANES], cp_ref[...], sp_ref[...])
    kcat = jnp.concatenate([kp, kc], axis=0)
    kmat = (kcat.astype(BF16), pltpu.roll(kcat, ROPE_DIM, 1).astype(BF16))
    vcat = jnp.concatenate([kvp_ref[:, LANES:2 * LANES], kvc_ref[:, LANES:2 * LANES]], axis=0)
    vrol = pltpu.roll(vcat, ROPE_DIM, 1)
    lane2 = lax.broadcasted_iota(I32, (2 * B, LANES), 1) < ROPE_DIM
    vdup = (jnp.where(lane2, vcat, vrol).astype(BF16), jnp.where(lane2, vrol, vcat).astype(BF16))
    r = lax.broadcasted_iota(I32, (B, 2 * B), 0)
    col = lax.broadcasted_iota(I32, (B, 2 * B), 1)
    mask = (col > r) & (col <= r + B) & ((i > 0) | (col >= B))
    scale = ROPE_DIM ** -0.5
    hpg = SWA_Q_HEADS // SWA_KV_HEADS
    for j in range(SWA_Q_HEADS // 2):
        g = (2 * j) // hpg
        qt = _rope_tile(q_ref[:, j * LANES:(j + 1) * LANES], cq, sq)
        outs = []
        for par in (0, 1):
            h = 2 * j + par
            qm = jnp.where(lo_half if par == 0 else jnp.logical_not(lo_half), qt, 0.0).astype(BF16)
            logit = _dot_nt(qm, kmat[0] if par == g else kmat[1]) * scale
            logit = jnp.where(mask, logit, -jnp.inf)
            sink = sink_ref[h]
            m = jnp.maximum(jnp.max(logit, axis=-1, keepdims=True), sink)
            ex = jnp.exp(logit - m)
            prob = ex / (jnp.sum(ex, axis=-1, keepdims=True) + jnp.exp(sink - m))
            outs.append(_dot(prob.astype(BF16), vdup[g]))
        o_ref[:, j * LANES:(j + 1) * LANES] = jnp.where(lo_half, outs[0], outs[1]).astype(BF16)


def _swa(xp, sinks, cos_t, sin_t):
    s = xp.shape[0]
    B = ATTN_BLOCK
    qw = SWA_Q_HEADS * ROPE_DIM
    kvw = 2 * SWA_KV_HEADS * ROPE_DIM
    prev = lambda i: (jnp.maximum(i - 1, 0), 0)
    cur = lambda i: (i, 0)
    return pl.pallas_call(
        _swa_kernel,
        grid=(s // B,),
        in_specs=[
            pl.BlockSpec(memory_space=pltpu.SMEM),
            pl.BlockSpec((B, qw), lambda i: (i, EV_Q // qw)),
            pl.BlockSpec((B, kvw), lambda i: (i, EV_KV // kvw)),
            pl.BlockSpec((B, kvw), lambda i: (jnp.maximum(i - 1, 0), EV_KV // kvw)),
            pl.BlockSpec((B, LANES), cur), pl.BlockSpec((B, LANES), cur),
            pl.BlockSpec((B, LANES), prev), pl.BlockSpec((B, LANES), prev),
        ],
        out_specs=pl.BlockSpec((B, qw), cur),
        out_shape=jax.ShapeDtypeStruct((s, qw), BF16),
        compiler_params=_cparams(("parallel",)),
        name="swa_sink",
    )(sinks, xp, xp, xp, cos_t, sin_t, cos_t, sin_t)


def _dsa_prep_kernel(qn_ref, qr_ref, qi_ref, ckv_ref, kk_ref, wi_ref, wuk_ref, kvn_ref, c_ref, s_ref,
                     qlat_ref, qrope_ref, qidx_ref, ckvn_ref, kr_ref, ki_ref, wis_ref):
    c, s = c_ref[...], s_ref[...]
    lane = lax.broadcasted_iota(I32, c.shape, 1)
    lo_half = lane < ROPE_DIM
    for h in range(MLA_HEADS):
        qn = qn_ref[:, h * MLA_NOPE:(h + 1) * MLA_NOPE].astype(BF16)
        qlat_ref[h] = _dot(qn, wuk_ref[h]).astype(BF16)
    for src, dst in ((qr_ref, qrope_ref), (qi_ref, qidx_ref)):
        for j in range(MLA_HEADS // 2):
            t = _rope_tile(src[:, j * LANES:(j + 1) * LANES], c, s)
            dst[2 * j] = jnp.where(lo_half, t, 0.0).astype(BF16)
            dst[2 * j + 1] = jnp.where(lo_half, pltpu.roll(t, ROPE_DIM, 1), 0.0).astype(BF16)
    kk = _rope_tile(kk_ref[...], c, s)
    kr_ref[...] = jnp.where(lo_half, kk, 0.0).astype(BF16)
    ki_ref[...] = jnp.where(lo_half, pltpu.roll(kk, ROPE_DIM, 1), 0.0).astype(BF16)
    ckv = ckv_ref[...]
    ms = jnp.mean(ckv * ckv, axis=-1, keepdims=True)
    ckvn_ref[...] = (ckv * lax.rsqrt(ms + NORM_EPS) * kvn_ref[...]).astype(BF16)
    wis_ref[...] = wi_ref[...] * (IDX_HEADS ** -0.5 * IDX_DIM ** -0.5)


def _dsa_prep(xp, wuk, kvn, cos_t, sin_t):
    s = xp.shape[0]
    tm = min(256, s)
    H = MLA_HEADS
    fix2 = lambda i: (0, 0)
    hrow = lambda i: (0, i, 0)
    row = lambda i: (i, 0)
    return pl.pallas_call(
        _dsa_prep_kernel,
        grid=(s // tm,),
        in_specs=[
            pl.BlockSpec((tm, 2048), lambda i: (i, OD_QN // 2048)),
            pl.BlockSpec((tm, 1024), lambda i: (i, OD_QR // 1024)),
            pl.BlockSpec((tm, 1024), lambda i: (i, OD_QI // 1024)),
            pl.BlockSpec((tm, MLA_RANK), lambda i: (i, OD_CKV // MLA_RANK)),
            pl.BlockSpec((tm, LANES), lambda i: (i, OD_KK // LANES)),
            pl.BlockSpec((tm, LANES), lambda i: (i, OD_WI // LANES)),
            pl.BlockSpec(wuk.shape, lambda i: (0, 0, 0)),
            pl.BlockSpec((1, MLA_RANK), fix2),
            pl.BlockSpec((tm, LANES), row), pl.BlockSpec((tm, LANES), row),
        ],
        out_specs=[
            pl.BlockSpec((H, tm, MLA_RANK), hrow),
            pl.BlockSpec((H, tm, LANES), hrow),
            pl.BlockSpec((H, tm, LANES), hrow),
            pl.BlockSpec((tm, MLA_RANK), row),
            pl.BlockSpec((tm, LANES), row),
            pl.BlockSpec((tm, LANES), row),
            pl.BlockSpec((tm, LANES), row),
        ],
        out_shape=[
            jax.ShapeDtypeStruct((H, s, MLA_RANK), BF16),
            jax.ShapeDtypeStruct((H, s, LANES), BF16),
            jax.ShapeDtypeStruct((H, s, LANES), BF16),
            jax.ShapeDtypeStruct((s, MLA_RANK), BF16),
            jax.ShapeDtypeStruct((s, LANES), BF16),
            jax.ShapeDtypeStruct((s, LANES), BF16),
            jax.ShapeDtypeStruct((s, LANES), F32),
        ],
        compiler_params=_cparams(("parallel",)),
        name="dsa_prep",
    )(xp, xp, xp, xp, xp, xp, wuk, kvn, cos_t, sin_t)


def _dsa_select_kernel(topk, qi_ref, wi_ref, ki_ref, bias_ref, key_ref, dig_ref):
    i = pl.program_id(0)
    Q, KC = DSA_SEL_Q, DSA_KC
    H = IDX_HEADS
    n_chunks = bias_ref.shape[0]
    n_vis = ((i + 1) * Q + KC - 1) // KC
    wi = wi_ref[...]
    qpos = i * Q + lax.broadcasted_iota(I32, (KC, Q), 1)
    kloc = lax.broadcasted_iota(I32, (KC, Q), 0)
    HG = DSA_SEL_HEADS_PER_DOT
    N_ACC = 4
    DIGIT_BITS = (11, 11, 10)
    DIGIT_SHIFT = (21, 10, 0)
    TOP_BIAS = 1 << (DIGIT_BITS[0] - 1)
    GUARD = jnp.int32(-0x7FFF8000)
    FIELD_ONES = jnp.int32(0x00010001)
    HK = KC // 2

    def pack(lo, hi):
        return lo | (hi << 16) | GUARD

    def score_chunk(c, carry):
        k = ki_ref[pl.ds(pl.multiple_of(c * KC, KC), KC), :]
        acc = jnp.zeros((KC, Q), F32)
        for g in range(H // HG):
            sc = _dot_nt(k, qi_ref[g * HG:(g + 1) * HG].reshape(HG * Q, LANES))
            for hh in range(HG):
                h = g * HG + hh
                acc = acc + jnp.maximum(sc[:, hh * Q:(hh + 1) * Q], 0.0) * wi[h:h + 1, :]
        acc = jnp.where(c * KC + kloc <= qpos, acc, -jnp.inf)
        bits = pltpu.bitcast(acc, I32)
        key = bits ^ ((bits >> 31) & jnp.int32(0x7FFFFFFF))
        key_ref[c] = key
        for d in range(len(DIGIT_BITS)):
            dig = (key >> DIGIT_SHIFT[d]) + TOP_BIAS if d == 0 else (key >> DIGIT_SHIFT[d]) & ((1 << DIGIT_BITS[d]) - 1)
            dig_ref[d, c] = pack(dig[:HK], dig[HK:])
        return carry

    lax.fori_loop(0, n_vis, score_chunk, 0)

    def count_ge(d, cand):
        cand2 = cand | (cand << 16)

        def body(c, accs):
            v = dig_ref[d, c]
            accs = list(accs)
            for r in range(HK // SUBLANES):
                w = v[r * SUBLANES:(r + 1) * SUBLANES, :] - cand2
                accs[r % N_ACC] = accs[r % N_ACC] + (lax.shift_right_logical(w, 15) & FIELD_ONES)
            return tuple(accs)
        accs = lax.fori_loop(0, n_vis, body, tuple(jnp.zeros((SUBLANES, Q), I32) for _ in range(N_ACC)))
        acc = sum(accs)
        cnt = (acc & 0xFFFF) + lax.shift_right_logical(acc, 16)
        return jnp.sum(cnt.astype(F32), axis=0, keepdims=True)

    def search(d, need):
        def bit_step(b, thr):
            cand = thr | (jnp.int32(1) << (DIGIT_BITS[d] - 1 - b))
            return jnp.where(count_ge(d, cand) >= need, cand, thr)

        return lax.fori_loop(0, DIGIT_BITS[d], bit_step, jnp.zeros((1, Q), I32))

    def drop_unless_equal(d, p):
        def body(c, carry):
            cur, nxt = dig_ref[d, c], dig_ref[d + 1, c]
            lo = jnp.where((cur & 0x7FFF) == p, nxt & 0x7FFF, 0)
            hi = jnp.where((lax.shift_right_logical(cur, 16) & 0x7FFF) == p, lax.shift_right_logical(nxt, 16) & 0x7FFF, 0)
            dig_ref[d + 1, c] = pack(lo, hi)
            return carry

        lax.fori_loop(0, n_vis, body, 0)

    need = jnp.full((1, Q), float(topk), F32)
    thr = jnp.zeros((1, Q), I32)
    for d in range(len(DIGIT_BITS)):
        p = search(d, need)
        thr = thr + ((p - (TOP_BIAS if d == 0 else 0)) << DIGIT_SHIFT[d])
        if d + 1 < len(DIGIT_BITS):
            need = need - count_ge(d, p + 1)
            drop_unless_equal(d, p)

    def emit(c, carry):
        sel = (key_ref[c] >= thr) & (c * KC + kloc <= qpos)
        bias_ref[c] = jnp.where(sel, 0.0, MASK_NEG).T
        return carry

    lax.fori_loop(0, n_vis, emit, 0)

    def fill(c, carry):
        bias_ref[c] = jnp.full((Q, KC), MASK_NEG, F32)
        return carry

    lax.fori_loop(n_vis, n_chunks, fill, 0)


def _dsa_select(qidx, wis, ki, topk):
    H, s, _ = qidx.shape
    Q, KC = DSA_SEL_Q, DSA_KC
    assert Q == LANES
    nch = s // KC
    return pl.pallas_call(
        functools.partial(_dsa_select_kernel, topk),
        grid=(s // Q,),
        in_specs=[
            pl.BlockSpec((H, Q, LANES), lambda i: (0, i, 0)),
            pl.BlockSpec((LANES, Q), lambda i: (0, i)),
            pl.BlockSpec((s, LANES), lambda i: (0, 0)),
        ],
        out_specs=pl.BlockSpec((nch, Q, KC), lambda i: (0, i, 0)),
        out_shape=jax.ShapeDtypeStruct((nch, s, KC), F32),
        scratch_shapes=[pltpu.VMEM((nch, KC, Q), I32), pltpu.VMEM((3, nch, KC // 2, Q), I32)],
        compiler_params=_cparams(("parallel",)),
        name="dsa_select",
    )(qidx, wis.T, ki)


def _dsa_attn_kernel(ql_ref, qr_ref, bias_ref, ckv_ref, kr_ref, o_ref, m_ref, l_ref, acc_ref, s_ref):
    i = pl.program_id(0)
    Q, KC, H = DSA_ATT_Q, DSA_KC, MLA_HEADS
    n_vis = ((i + 1) * Q + KC - 1) // KC
    m_ref[...] = jnp.full(m_ref.shape, -jnp.inf, F32)
    l_ref[...] = jnp.zeros(l_ref.shape, F32)
    acc_ref[...] = jnp.zeros(acc_ref.shape, F32)

    def keys(c):
        off = pl.multiple_of(c * KC, KC)
        return ckv_ref[pl.ds(off, KC), :], kr_ref[pl.ds(off, KC), :]

    def raw_logits(c):
        ck, kr = keys(c)
        ql = ql_ref[...].reshape(H * Q, MLA_RANK)
        qr = qr_ref[...].reshape(H * Q, LANES)
        return _dot_nt(ql, ck) + _dot_nt(qr, kr)

    def consume(c, slot):
        logit = s_ref[slot] * MLA_SCALE
        logit = (logit.reshape(H, Q, KC) + bias_ref[c][None]).reshape(H * Q, KC)
        m_old = m_ref[...]
        m_new = jnp.maximum(m_old, jnp.max(logit, axis=-1, keepdims=True))
        alpha = jnp.exp(m_old - m_new)
        p = jnp.exp(logit - m_new)
        l_ref[...] = alpha * l_ref[...] + jnp.sum(p, axis=-1, keepdims=True)
        acc_ref[...] = alpha * acc_ref[...] + _dot(p.astype(BF16), keys(c)[0])
        m_ref[...] = m_new

    s_ref[0] = raw_logits(0)

    def pair(j, carry):
        c = 2 * j
        s_ref[1] = raw_logits(c + 1)
        consume(c, 0)
        s_ref[0] = raw_logits(c + 2)
        consume(c + 1, 1)
        return carry

    n_pairs = (n_vis - 1) // 2
    lax.fori_loop(0, n_pairs, pair, 0)
    last = 2 * n_pairs

    @pl.when(last + 1 < n_vis)
    def _():
        s_ref[1] = raw_logits(last + 1)
        consume(last, 0)
        consume(last + 1, 1)

    @pl.when(last + 1 == n_vis)
    def _():
        consume(last, 0)

    o_ref[...] = (acc_ref[...] / l_ref[...]).reshape(H, Q, MLA_RANK).astype(BF16)


def _dsa_attn(qlat, qrope, bias, ckvn, kr):
    H, s, _ = qlat.shape
    Q, KC = DSA_ATT_Q, DSA_KC
    nch = s // KC
    hrow = lambda i: (0, i, 0)
    fix = lambda i: (0, 0)
    return pl.pallas_call(
        _dsa_attn_kernel,
        grid=(s // Q,),
        in_specs=[
            pl.BlockSpec((H, Q, MLA_RANK), hrow),
            pl.BlockSpec((H, Q, LANES), hrow),
            pl.BlockSpec((nch, Q, KC), hrow),
            pl.BlockSpec((s, MLA_RANK), fix),
            pl.BlockSpec((s, LANES), fix),
        ],
        out_specs=pl.BlockSpec((H, Q, MLA_RANK), hrow),
        out_shape=jax.ShapeDtypeStruct((H, s, MLA_RANK), BF16),
        scratch_shapes=[
            pltpu.VMEM((H * Q, 1), F32),
            pltpu.VMEM((H * Q, 1), F32),
            pltpu.VMEM((H * Q, MLA_RANK), F32),
            pltpu.VMEM((2, H * Q, KC), F32),
        ],
        compiler_params=_cparams(("parallel",)),
        name="dsa_attn",
    )(qlat, qrope, bias, ckvn, kr)


def _odd_out_kernel(ol_ref, wuv_ref, wo_ref, x_ref, g_ref, b_ref, o_ref, of_ref, u_ref):
    for h in range(MLA_HEADS):
        u_ref[:, h * MLA_V:(h + 1) * MLA_V] = _dot(ol_ref[h], wuv_ref[h]).astype(BF16)
    mix = _dot(u_ref[...], wo_ref[...])
    _store_row_layouts(_layer_norm(DEEPNORM_ALPHA * x_ref[...] + mix, g_ref[...], b_ref[...]), o_ref, of_ref)


def _odd_out(olat, wuv, wo, x, g, b):
    H, s, _ = olat.shape
    tm = min(256, s)
    row = lambda i: (i, 0)
    fix = lambda i: (0, 0)
    out_specs, out_shape = _row_layout_outputs(s, tm)
    return pl.pallas_call(
        _odd_out_kernel,
        grid=(s // tm,),
        in_specs=[
            pl.BlockSpec((H, tm, MLA_RANK), lambda i: (0, i, 0)),
            pl.BlockSpec(wuv.shape, lambda i: (0, 0, 0)),
            pl.BlockSpec(wo.shape, fix),
            pl.BlockSpec((tm, D_MODEL), row),
            pl.BlockSpec((1, D_MODEL), fix), pl.BlockSpec((1, D_MODEL), fix),
        ],
        out_specs=out_specs,
        out_shape=out_shape,
        scratch_shapes=[pltpu.VMEM((tm, MLA_HEADS * MLA_V), BF16)],
        compiler_params=_cparams(("parallel",)),
        name="odd_out_ln",
    )(olat, wuv, wo, x, g, b)


ROUTE_LANE0 = MOE_GROUPS


def _router_kernel(h_ref, wr_ref, br_ref, info_ref, cnt_ref, run_ref):
    i = pl.program_id(0)

    @pl.when(i == 0)
    def _():
        run_ref[...] = jnp.zeros(run_ref.shape, F32)

    h = h_ref[...]
    tm = h.shape[0]
    h_hi = h.astype(BF16)
    h_lo = (h - h_hi.astype(F32)).astype(BF16)
    w = wr_ref[...]
    w_hi = w.astype(BF16)
    w_lo = (w - w_hi.astype(F32)).astype(BF16)
    logits = _dot(h_hi, w_hi) + (_dot(h_hi, w_lo) + _dot(h_lo, w_hi)) + br_ref[...]
    lane = lax.broadcasted_iota(I32, (tm, LANES), 1)
    lane_f = lane.astype(F32)
    neg = -jnp.inf
    big = float(LANES)
    is_grp = lane < MOE_GROUPS
    gl = jnp.where(is_grp, logits, neg)
    gmax = jnp.max(gl, axis=-1, keepdims=True)
    gsel = jnp.min(jnp.where(gl == gmax, lane_f, big), axis=-1, keepdims=True)
    gsum = jnp.sum(jnp.where(is_grp, jnp.exp(logits - gmax), 0.0), axis=-1, keepdims=True)
    egrp = ((lane - ROUTE_LANE0) >> 3).astype(F32)
    valid = (lane >= ROUTE_LANE0) & (lane < ROUTE_LANE0 + MOE_EXPERTS) & (egrp == gsel)
    el = jnp.where(valid, logits, neg)
    v1 = jnp.max(el, axis=-1, keepdims=True)
    i1 = jnp.min(jnp.where(el == v1, lane_f, big), axis=-1, keepdims=True)
    el2 = jnp.where(lane_f == i1, neg, el)
    v2 = jnp.max(el2, axis=-1, keepdims=True)
    i2 = jnp.min(jnp.where(el2 == v2, lane_f, big), axis=-1, keepdims=True)
    t = jnp.exp(v2 - v1)
    p1 = 1.0 / (1.0 + t)
    p2 = t / (1.0 + t)
    ggate = 1.0 / gsum
    m1 = lane_f == i1
    m2 = lane_f == i2
    memb = jnp.where(m1 | m2, 1.0, 0.0)
    tri = (lax.broadcasted_iota(I32, (tm, tm), 0) > lax.broadcasted_iota(I32, (tm, tm), 1))
    cum = _dot(jnp.where(tri, 1.0, 0.0).astype(BF16), memb.astype(BF16)) + run_ref[...]
    rank1 = jnp.sum(jnp.where(m1, cum, 0.0), axis=-1, keepdims=True)
    rank2 = jnp.sum(jnp.where(m2, cum, 0.0), axis=-1, keepdims=True)
    run_ref[...] = run_ref[...] + jnp.sum(memb, axis=0, keepdims=True)
    info = jnp.where(lane == 0, i1 - ROUTE_LANE0, 0.0)
    info = jnp.where(lane == 1, i2 - ROUTE_LANE0, info)
    info = jnp.where(lane == 2, p1 * ggate, info)
    info = jnp.where(lane == 3, p2 * ggate, info)
    info = jnp.where(lane == 4, rank1, info)
    info = jnp.where(lane == 5, rank2, info)
    info_ref[...] = info
    cnt_ref[...] = run_ref[...]


def _router(h, wr, br):
    t = h.shape[0]
    tm = min(512, t)
    return pl.pallas_call(
        _router_kernel,
        grid=(t // tm,),
        in_specs=[
            pl.BlockSpec((tm, D_MODEL), lambda i: (i, 0)),
            pl.BlockSpec(wr.shape, lambda i: (0, 0)),
            pl.BlockSpec((1, LANES), lambda i: (0, 0)),
        ],
        out_specs=[pl.BlockSpec((tm, LANES), lambda i: (i, 0)), pl.BlockSpec((1, LANES), lambda i: (0, 0))],
        out_shape=[jax.ShapeDtypeStruct((t, LANES), F32), jax.ShapeDtypeStruct((1, LANES), F32)],
        scratch_shapes=[pltpu.VMEM((1, LANES), F32)],
        compiler_params=_cparams(("arbitrary",)),
        name="moe_router",
    )(h, wr, br)


def _plan_kernel(info_ref, ps_ref, pos_ref):
    info = info_ref[...]
    lane = lax.broadcasted_iota(I32, info.shape, 1)
    lane_f = lane.astype(F32)
    ps = ps_ref[...]
    pos1 = jnp.sum(jnp.where(lane_f == info[:, 0:1], ps, 0.0), axis=-1, keepdims=True) + info[:, 4:5]
    pos2 = jnp.sum(jnp.where(lane_f == info[:, 1:2], ps, 0.0), axis=-1, keepdims=True) + info[:, 5:6]
    pos_ref[...] = jnp.where(lane == 0, pos1, jnp.where(lane == 1, pos2, 0.0)).astype(I32)


def _plan(info, pad_start):
    t = info.shape[0]
    tm = min(1024, t)
    return pl.pallas_call(
        _plan_kernel,
        grid=(t // tm,),
        in_specs=[pl.BlockSpec((tm, LANES), lambda i: (i, 0)), pl.BlockSpec((1, LANES), lambda i: (0, 0))],
        out_specs=pl.BlockSpec((tm, LANES), lambda i: (i, 0)),
        out_shape=jax.ShapeDtypeStruct((t, LANES), I32),
        compiler_params=_cparams(("parallel",)),
        name="moe_plan",
    )(info, pad_start)


def _invert_kernel(pos_ref, rt_ref):
    def clear(r, carry):
        rt_ref[r] = 0
        return carry

    lax.fori_loop(0, rt_ref.shape[0], clear, 0, unroll=8)

    def put(n, carry):
        rt_ref[pos_ref[n]] = lax.shift_right_logical(n, 1)
        return carry

    lax.fori_loop(0, pos_ref.shape[0], put, 0, unroll=8)


def _invert(pos_flat, n_rows):
    return pl.pallas_call(
        _invert_kernel,
        in_specs=[pl.BlockSpec(memory_space=pltpu.SMEM)],
        out_specs=pl.BlockSpec(memory_space=pltpu.SMEM),
        out_shape=jax.ShapeDtypeStruct((n_rows,), I32),
        name="moe_invert",
    )(pos_flat)


def _expert_kernel(be_ref, nu_ref, rt_ref, h_ref, wg_ref, wu_ref, wd_ref, o_ref, xbuf0, xbuf1, xb, wgb, wub, wdb, sem):
    b = pl.program_id(0)
    n_used = nu_ref[0]
    R = EXPERT_ROWS

    bufs = (xbuf0, xbuf1)

    def row_copy(tok, r, slot):
        src = h_ref.at[pl.ds(pl.multiple_of(tok * ROW_TILES, ROW_TILES), ROW_TILES)]
        return pltpu.make_async_copy(src, bufs[slot].at[pl.ds(r * ROW_TILES, ROW_TILES)], sem.at[slot])

    def drain(slot):
        def body(r, carry):
            row_copy(0, 0, slot).wait()
            return carry
        lax.fori_loop(0, R, body, 0, unroll=8)

    @pl.when(b == 0)
    def _():
        def body(r, carry):
            row_copy(rt_ref[r], r, 0).start(priority=GATHER_DMA_PRIORITY)
            return carry
        lax.fori_loop(0, R, body, 0, unroll=8)

    def block(slot):
        drain(slot)

        @pl.when((b == 0) | (be_ref[b] != be_ref[jnp.maximum(b - 1, 0)]))
        def _():
            wgb[...] = wg_ref[0, 0].astype(BF16)
            wub[...] = wu_ref[0, 0].astype(BF16)
            wdb[...] = wd_ref[0, 0].astype(BF16)

        base = (b + 1) * R
        for r in range(R):
            row_copy(rt_ref[base + r], r, 1 - slot).start(priority=GATHER_DMA_PRIORITY)
        for j in range(ROW_TILES):
            xb[:, j * LANES:(j + 1) * LANES] = bufs[slot][pl.ds(j, R, stride=ROW_TILES), :].astype(BF16)
        x = xb[...]
        hid = _silu(_dot(x, wgb[...])) * _dot(x, wub[...])
        y = _dot(hid.astype(BF16), wdb[...])
        for j in range(ROW_TILES):
            o_ref[pl.ds(j, R, stride=ROW_TILES), :] = y[:, j * LANES:(j + 1) * LANES]

    for parity in (0, 1):
        pl.when((b < n_used) & (b % 2 == parity))(functools.partial(block, parity))

    @pl.when(b >= n_used)
    def _():
        o_ref[...] = jnp.zeros(o_ref.shape, F32)

    for parity in (0, 1):
        pl.when((b == n_used) & (b % 2 == parity))(functools.partial(drain, parity))


def _experts(blk_e, n_used, row_token, h, wg, wu, wd, layer):
    r = row_token.shape[0]
    nb = r // EXPERT_ROWS
    wmap = lambda b, be, nu, rt: (layer, be[b], 0, 0)
    return pl.pallas_call(
        _expert_kernel,
        grid_spec=pltpu.PrefetchScalarGridSpec(
            num_scalar_prefetch=3,
            grid=(nb,),
            in_specs=[
                pl.BlockSpec(memory_space=pl.ANY),
                pl.BlockSpec((1, 1, D_MODEL, MOE_FF), wmap),
                pl.BlockSpec((1, 1, D_MODEL, MOE_FF), wmap),
                pl.BlockSpec((1, 1, MOE_FF, D_MODEL), wmap),
            ],
            out_specs=pl.BlockSpec((EXPERT_ROWS * ROW_TILES, LANES), lambda b, be, nu, rt: (b, 0)),
            scratch_shapes=[
                pltpu.VMEM((EXPERT_ROWS * ROW_TILES, LANES), F32),
                pltpu.VMEM((EXPERT_ROWS * ROW_TILES, LANES), F32),
                pltpu.VMEM((EXPERT_ROWS, D_MODEL), BF16),
                pltpu.VMEM((D_MODEL, MOE_FF), BF16),
                pltpu.VMEM((D_MODEL, MOE_FF), BF16),
                pltpu.VMEM((MOE_FF, D_MODEL), BF16),
                pltpu.SemaphoreType.DMA((2,)),
            ],
        ),
        out_shape=jax.ShapeDtypeStruct((r * ROW_TILES, LANES), F32),
        compiler_params=_cparams(("arbitrary",)),
        name="moe_experts",
    )(blk_e, n_used, row_token, h, wg, wu, wd)


COMBINE_TOKENS = 256


def _combine_ple_kernel(pos_ref, y_ref, info_ref, x_ref, g_ref, b_ref, p_ref, wg_ref, bg_ref, wp_ref, o_ref,
                        buf0, buf1, ycat, sem):
    i = pl.program_id(0)
    n = pl.num_programs(0)
    TB = x_ref.shape[0]
    bufs = (buf0, buf1)

    def row_copy(src_row, k, tt, slot):
        src = y_ref.at[pl.ds(pl.multiple_of(src_row * ROW_TILES, ROW_TILES), ROW_TILES)]
        return pltpu.make_async_copy(src, bufs[slot].at[k, pl.ds(tt * ROW_TILES, ROW_TILES)], sem.at[slot])

    def drain(slot):
        def body(tt, carry):
            row_copy(0, 0, 0, slot).wait()
            row_copy(0, 1, 0, slot).wait()
            return carry
        lax.fori_loop(0, TB, body, 0, unroll=8)

    @pl.when(i == 0)
    def _():
        def body(tt, carry):
            row_copy(pos_ref[2 * tt], 0, tt, 0).start(priority=0)
            row_copy(pos_ref[2 * tt + 1], 1, tt, 0).start(priority=1)
            return carry
        lax.fori_loop(0, TB, body, 0, unroll=8)

    def block(slot):
        drain(slot)
        base = jnp.minimum(i + 1, n - 1) * (2 * TB)
        for tt in range(TB):
            row_copy(pos_ref[base + 2 * tt], 0, tt, 1 - slot).start(priority=0)
            row_copy(pos_ref[base + 2 * tt + 1], 1, tt, 1 - slot).start(priority=1)
        info = info_ref[...]
        g0, g1 = info[:, 2:3], info[:, 3:4]
        for j in range(ROW_TILES):
            rows = pl.ds(j, TB, stride=ROW_TILES)
            ycat[:, j * LANES:(j + 1) * LANES] = bufs[slot][0, rows, :] * g0 + bufs[slot][1, rows, :] * g1
        x2 = _layer_norm(DEEPNORM_ALPHA * x_ref[...] + ycat[...], g_ref[...], b_ref[...])
        gate = jax.nn.sigmoid(_dot(x2.astype(BF16), wg_ref[...]) + bg_ref[...])
        o_ref[...] = x2 + gate * _dot(p_ref[...].astype(BF16), wp_ref[...])

    for parity in (0, 1):
        pl.when(i % 2 == parity)(functools.partial(block, parity))
    for parity in (0, 1):
        pl.when((i == n - 1) & (i % 2 == parity))(functools.partial(drain, 1 - parity))


def _combine_ple(pos_flat, y_rows, info, x, g, b, p, wg, bg, wp):
    t = x.shape[0]
    TB = min(COMBINE_TOKENS, t)
    row = lambda i, pos: (i, 0)
    fix = lambda i, pos: (0, 0)
    return pl.pallas_call(
        _combine_ple_kernel,
        grid_spec=pltpu.PrefetchScalarGridSpec(
            num_scalar_prefetch=1,
            grid=(t // TB,),
            in_specs=[
                pl.BlockSpec(memory_space=pl.ANY),
                pl.BlockSpec((TB, LANES), row),
                pl.BlockSpec((TB, D_MODEL), row),
                pl.BlockSpec((1, D_MODEL), fix), pl.BlockSpec((1, D_MODEL), fix),
                pl.BlockSpec((TB, PLE_DIM), row),
                pl.BlockSpec(wg.shape, fix),
                pl.BlockSpec((1, D_MODEL), fix),
                pl.BlockSpec(wp.shape, fix),
            ],
            out_specs=pl.BlockSpec((TB, D_MODEL), row),
            scratch_shapes=[
                pltpu.VMEM((2, TB * ROW_TILES, LANES), F32),
                pltpu.VMEM((2, TB * ROW_TILES, LANES), F32),
                pltpu.VMEM((TB, D_MODEL), F32),
                pltpu.SemaphoreType.DMA((2,)),
            ],
        ),
        out_shape=jax.ShapeDtypeStruct((t, D_MODEL), F32),
        compiler_params=_cparams(("arbitrary",)),
        name="moe_combine_ln_ple",
    )(pos_flat, y_rows, info, x, g, b, p, wg, bg, wp)


def _hier_moe_ln_ple(x, x_folded, wr, br, wg, wu, wd, layer, ln_g, ln_b, p, ple_wg, ple_bg, ple_wp):
    t = x.shape[0]
    info, cnt = _router(x, wr, br)
    counts = cnt[0, ROUTE_LANE0:ROUTE_LANE0 + MOE_EXPERTS].astype(I32)
    padded = (counts + EXPERT_ROWS - 1) // EXPERT_ROWS * EXPERT_ROWS
    pad_end = jnp.cumsum(padded)
    pad_start = jnp.zeros((1, LANES), F32).at[0, :MOE_EXPERTS].set((pad_end - padded).astype(F32))
    n_rows = 2 * t + MOE_EXPERTS * EXPERT_ROWS
    n_blocks = n_rows // EXPERT_ROWS
    blk_start = jnp.arange(n_blocks, dtype=I32) * EXPERT_ROWS
    blk_e = jnp.minimum(jnp.sum(pad_end[None, :] <= blk_start[:, None], axis=1), MOE_EXPERTS - 1).astype(I32)
    n_used = (pad_end[-1:] // EXPERT_ROWS).astype(I32)
    pos = _plan(info, pad_start)[:, :2].reshape(-1)
    y_rows = _experts(blk_e, n_used, _invert(pos, n_rows), x_folded, wg, wu, wd, layer)
    return _combine_ple(pos, y_rows, info, x, ln_g, ln_b, p, ple_wg, ple_bg, ple_wp)


def _rope_tables(positions):
    inv = ROPE_THETA ** (-jnp.arange(0, ROPE_DIM, 2, dtype=F32) / ROPE_DIM)
    ang = positions.astype(F32)[:, None] * inv
    cos, sin = jnp.cos(ang), jnp.sin(ang)
    cos_t = jnp.tile(cos, (1, LANES // (ROPE_DIM // 2)))
    sin_t = jnp.tile(jnp.concatenate([-sin, sin], axis=1), (1, LANES // ROPE_DIM))
    return cos_t, sin_t


def _pad_cols(w, n):
    return jnp.pad(w, ((0, 0), (0, n - w.shape[1])))


def _even_w_in(w):
    z, xbc, dt, q, kv = jnp.split(w, [2048, 5120, 5152, 6176], axis=1)
    return _pad_cols(jnp.concatenate([z, xbc, q, kv, dt], axis=1), EV_NP).astype(BF16)


def _odd_w_in(w):
    q, ckv, krope, qi, ki, wi = jnp.split(w, [3072, 3584, 3648, 4672, 4736], axis=1)
    q = q.reshape(D_MODEL, MLA_HEADS, MLA_NOPE + MLA_ROPE)
    qn = q[:, :, :MLA_NOPE].reshape(D_MODEL, -1)
    qr = q[:, :, MLA_NOPE:].reshape(D_MODEL, -1)
    return _pad_cols(jnp.concatenate([qn, qr, qi, ckv, krope, ki, _pad_cols(wi, LANES)], axis=1), OD_NP).astype(BF16)


def _head_expand_matrix():
    e = np.zeros((LANES, SSM_INNER), np.float32)
    for h in range(SSM_HEADS):
        e[h, h * SSM_HEAD_DIM:(h + 1) * SSM_HEAD_DIM] = 1.0
    return jnp.asarray(e, BF16)


def kernel(x, p, positions, ev_w_in, ev_conv_w, ev_conv_b, ev_dt_bias, ev_a_log, ev_d_skip, ev_ssm_norm, ev_sinks, ev_w_out, od_w_in, od_kv_norm, od_w_uk, od_w_uv, od_w_out, ln1_g, ln1_b, ln2_g, ln2_b, moe_router_group, moe_router_group_b, moe_router_expert, moe_router_expert_b, moe_w_gate, moe_w_up, moe_w_down, ple_w_proj, ple_w_gate, ple_b_gate):
    batch, s, d = x.shape
    assert batch == 1 and d == D_MODEL
    xs = x[0]
    cos_t, sin_t = _rope_tables(positions[0])
    e_mat = _head_expand_matrix()
    topk = min(IDX_TOPK_MAX, s // 4)
    row = lambda v: v.reshape(1, -1)
    pad_row = lambda v: _pad_cols(v.reshape(1, -1), LANES)
    for i in range(DEPTH):
        j = i // 2
        if i % 2 == 0:
            xp = _inproj(xs, _even_w_in(ev_w_in[j]))
            y_ssm = _ssd(xp, ev_conv_w[j], row(ev_conv_b[j]), pad_row(ev_dt_bias[j]), pad_row(ev_a_log[j]),
                         row(jnp.repeat(ev_d_skip[j], SSM_HEAD_DIM)), row(ev_ssm_norm[j]), e_mat)
            y_att = _swa(xp, ev_sinks[j], cos_t, sin_t)
            w_out = ev_w_out[j].astype(BF16)
            xs, xf = _even_out(y_ssm, y_att, w_out[:SSM_INNER], w_out[SSM_INNER:], xs, row(ln1_g[i]), row(ln1_b[i]))
        else:
            xp = _inproj(xs, _odd_w_in(od_w_in[j]))
            qlat, qrope, qidx, ckvn, kr, ki, wis = _dsa_prep(
                xp, od_w_uk[j].astype(BF16), row(od_kv_norm[j]), cos_t, sin_t)
            bias = _dsa_select(qidx, wis, ki, topk)
            olat = _dsa_attn(qlat, qrope, bias, ckvn, kr)
            xs, xf = _odd_out(olat, od_w_uv[j].astype(BF16), od_w_out[j].astype(BF16), xs, row(ln1_g[i]), row(ln1_b[i]))
        wr = _pad_cols(jnp.concatenate([moe_router_group[i], moe_router_expert[i]], axis=1), LANES)
        br = pad_row(jnp.concatenate([moe_router_group_b[i], moe_router_expert_b[i]]))
        xs = _hier_moe_ln_ple(xs, xf, wr, br, moe_w_gate, moe_w_up, moe_w_down, i, row(ln2_g[i]), row(ln2_b[i]),
                              p[i, 0], ple_w_gate[i].astype(BF16), row(ple_b_gate[i]), ple_w_proj[i].astype(BF16))
    return xs[None]
```

```python
import functools

import jax
import jax.numpy as jnp
import numpy as np
from jax import lax
from jax.experimental import pallas as pl
from jax.experimental.pallas import tpu as pltpu

F32 = jnp.float32
BF16 = jnp.bfloat16
I32 = jnp.int32
I16 = jnp.int16

D_MODEL = 2048
DEPTH = 4
ROPE_THETA = 10000.0
ROPE_DIM = 64
NORM_EPS = 1e-5
SSM_HEADS = 32
SSM_HEAD_DIM = 64
SSM_INNER = SSM_HEADS * SSM_HEAD_DIM
SSM_GROUPS = 4
SSM_STATE = 128
SSM_CONV = 4
SSM_CHUNK = 128
SWA_Q_HEADS = 16
SWA_KV_HEADS = 2
ATTN_BLOCK = 128
MLA_HEADS = 16
MLA_NOPE = 128
MLA_ROPE = ROPE_DIM
MLA_V = 128
MLA_RANK = 512
MLA_SCALE = (MLA_NOPE + MLA_ROPE) ** -0.5
IDX_HEADS = 16
IDX_DIM = ROPE_DIM
IDX_TOPK_MAX = 256
MOE_GROUPS = 4
MOE_EPG = 8
MOE_EXPERTS = MOE_GROUPS * MOE_EPG
MOE_FF = 512
PLE_DIM = 256
DEEPNORM_ALPHA = (2 * DEPTH) ** 0.25

LANES = 128
ROW_TILES = D_MODEL // LANES
SUBLANES = 8
VMEM_LIMIT_BYTES = 56 * 1024 * 1024

EXPERT_ROWS = 256
GATHER_DMA_PRIORITY = 1
DSA_SEL_Q = 128
DSA_ATT_Q = 64
DSA_KC = 512
DSA_SEL_HEADS_PER_DOT = 4
MASK_NEG = -1e30

EV_Z, EV_XS, EV_BC, EV_Q, EV_KV, EV_DT = 0, 2048, 4096, 5120, 6144, 6400
EV_NP = 6656
OD_QN, OD_QR, OD_QI, OD_CKV, OD_KK, OD_WI = 0, 2048, 3072, 4096, 4608, 4736
OD_NP = 5120


def _cparams(sem, vmem=VMEM_LIMIT_BYTES):
    return pltpu.CompilerParams(dimension_semantics=sem, vmem_limit_bytes=vmem)


def _dot(a, b):
    return jnp.dot(a, b, preferred_element_type=F32)


def _dot_nt(a, b):
    return lax.dot_general(a, b, (((1,), (1,)), ((), ())), preferred_element_type=F32)


def _split3(v):
    hi = v.astype(BF16)
    r = v - hi.astype(F32)
    mid = r.astype(BF16)
    lo = (r - mid.astype(F32)).astype(BF16)
    return hi, mid, lo


def _expand(v, e):
    hi, mid, lo = _split3(v)
    return _dot(hi, e) + _dot(mid, e) + _dot(lo, e)


def _silu(v):
    return v * jax.nn.sigmoid(v)


def _layer_norm(v, g, b):
    mu = jnp.mean(v, axis=-1, keepdims=True)
    vc = v - mu
    var = jnp.mean(vc * vc, axis=-1, keepdims=True)
    return vc * lax.rsqrt(var + NORM_EPS) * g + b


def _rope_tile(t, c, s):
    lane = lax.broadcasted_iota(I32, t.shape, 1)
    first_half = (lane & 32) == 0
    swapped = jnp.where(first_half, pltpu.roll(t, LANES - 32, 1), pltpu.roll(t, 32, 1))
    return t * c + swapped * s


def _inproj_kernel(x_ref, w_ref, o_ref):
    o_ref[...] = _dot(x_ref[...].astype(BF16), w_ref[...])


def _inproj(x, w):
    m, k = x.shape
    n = w.shape[1]
    tm, tn = min(1024, m), 512
    return pl.pallas_call(
        _inproj_kernel,
        grid=(m // tm, n // tn),
        in_specs=[pl.BlockSpec((tm, k), lambda i, j: (i, 0)), pl.BlockSpec((k, tn), lambda i, j: (0, j))],
        out_specs=pl.BlockSpec((tm, tn), lambda i, j: (i, j)),
        out_shape=jax.ShapeDtypeStruct((m, n), F32),
        compiler_params=_cparams(("parallel", "arbitrary")),
        name="inproj",
    )(x, w)


def _store_row_layouts(y, o_ref, of_ref):
    o_ref[...] = y
    for j in range(ROW_TILES):
        of_ref[pl.ds(j, y.shape[0], stride=ROW_TILES), :] = y[:, j * LANES:(j + 1) * LANES]


def _row_layout_outputs(m, tm):
    specs = [pl.BlockSpec((tm, D_MODEL), lambda i: (i, 0)), pl.BlockSpec((tm * ROW_TILES, LANES), lambda i: (i, 0))]
    shapes = [jax.ShapeDtypeStruct((m, D_MODEL), F32), jax.ShapeDtypeStruct((m * ROW_TILES, LANES), F32)]
    return specs, shapes


def _even_out_kernel(a1_ref, a2_ref, w1_ref, w2_ref, x_ref, g_ref, b_ref, o_ref, of_ref):
    mix = _dot(a1_ref[...], w1_ref[...]) + _dot(a2_ref[...], w2_ref[...])
    _store_row_layouts(_layer_norm(DEEPNORM_ALPHA * x_ref[...] + mix, g_ref[...], b_ref[...]), o_ref, of_ref)


def _even_out(y_ssm, y_att, w1, w2, x, g, b):
    m = x.shape[0]
    tm = min(256, m)
    row = lambda i: (i, 0)
    fix = lambda i: (0, 0)
    out_specs, out_shape = _row_layout_outputs(m, tm)
    return pl.pallas_call(
        _even_out_kernel,
        grid=(m // tm,),
        in_specs=[
            pl.BlockSpec((tm, y_ssm.shape[1]), row),
            pl.BlockSpec((tm, y_att.shape[1]), row),
            pl.BlockSpec(w1.shape, fix),
            pl.BlockSpec(w2.shape, fix),
            pl.BlockSpec((tm, D_MODEL), row),
            pl.BlockSpec((1, D_MODEL), fix),
            pl.BlockSpec((1, D_MODEL), fix),
        ],
        out_specs=out_specs,
        out_shape=out_shape,
        compiler_params=_cparams(("parallel",)),
        name="even_out_ln",
    )(y_ssm, y_att, w1, w2, x, g, b)


def _ssd_kernel(z_ref, xs_ref, bc_ref, dt_ref, cwx_ref, cbx_ref, cwb_ref, cbb_ref, dtb_ref, alog_ref,
                dsk_ref, nrm_ref, e_ref, y_ref, xs_ext, bc_ext, st_ref):
    c = pl.program_id(0)
    L = SSM_CHUNK
    halo = SUBLANES

    @pl.when(c == 0)
    def _():
        xs_ext[0:halo, :] = jnp.zeros((halo, xs_ext.shape[1]), F32)
        bc_ext[0:halo, :] = jnp.zeros((halo, bc_ext.shape[1]), F32)
        st_ref[...] = jnp.zeros(st_ref.shape, F32)

    xs_ext[halo:halo + L, :] = xs_ref[...]
    bc_ext[halo:halo + L, :] = bc_ref[...]

    def conv(ext, w_ref, b_ref):
        acc = b_ref[...]
        for j in range(SSM_CONV):
            lo = halo - (SSM_CONV - 1) + j
            acc = acc + ext[lo:lo + L, :] * w_ref[j:j + 1, :]
        return acc

    xs = _silu(conv(xs_ext, cwx_ref, cbx_ref))
    bc = _silu(conv(bc_ext, cwb_ref, cbb_ref))
    xs_ext[0:halo, :] = xs_ext[L:L + halo, :]
    bc_ext[0:halo, :] = bc_ext[L:L + halo, :]

    pre = dt_ref[...] + dtb_ref[...]
    dt = jnp.maximum(pre, 0.0) + jnp.log1p(jnp.exp(-jnp.abs(pre)))
    a = dt * (-jnp.exp(alog_ref[...]))
    row = lax.broadcasted_iota(I32, (L, LANES), 0)
    acs = a
    s = 1
    while s < L:
        acs = acs + jnp.where(row >= s, pltpu.roll(acs, s, 0), 0.0)
        s *= 2
    a_last = acs[L - 1:L, :]
    e = e_ref[...]
    dt_x = _expand(dt, e)
    dte_x = _expand(dt * jnp.exp(a_last - acs), e)
    eacs_x = _expand(jnp.exp(acs), e)
    cd_x = _expand(jnp.broadcast_to(jnp.exp(a_last), (SUBLANES, LANES)), e)[0:1, :]
    acs_t = acs.T

    xdt = (xs * dt_x).astype(BF16)
    xd = (xs * dte_x).astype(BF16)
    tri = lax.broadcasted_iota(I32, (L, L), 0) >= lax.broadcasted_iota(I32, (L, L), 1)
    first_head = lax.broadcasted_iota(I32, (L, LANES), 1) < SSM_HEAD_DIM
    n_state = SSM_STATE
    gw = SSM_INNER // SSM_GROUPS
    ys = []
    for g in range(SSM_GROUPS):
        bg = bc[:, g * n_state:(g + 1) * n_state]
        cg = bc[:, SSM_GROUPS * n_state + g * n_state:SSM_GROUPS * n_state + (g + 1) * n_state]
        bb, cb16 = bg.astype(BF16), cg.astype(BF16)
        cbm = _dot_nt(cb16, bb)
        st = st_ref[g]
        y_off = _dot(cb16, st.astype(BF16)) * eacs_x[:, g * gw:(g + 1) * gw]
        st_ref[g] = st * cd_x[:, g * gw:(g + 1) * gw] + _dot(bg.T.astype(BF16), xd[:, g * gw:(g + 1) * gw])
        parts = []
        for j in range(gw // LANES):
            lo = g * gw + j * LANES
            xp = xdt[:, lo:lo + LANES]
            out = None
            for par in (0, 1):
                h = lo // SSM_HEAD_DIM + par
                seg = acs[:, h:h + 1] - acs_t[h:h + 1, :]
                lm = (jnp.exp(jnp.where(tri, seg, -jnp.inf)) * cbm).astype(BF16)
                xm = jnp.where(first_head if par == 0 else jnp.logical_not(first_head), xp, jnp.zeros_like(xp))
                d = _dot(lm, xm)
                out = d if out is None else out + d
            parts.append(out)
        ys.append(jnp.concatenate(parts, axis=1) + y_off)
    y = jnp.concatenate(ys, axis=1) + xs * dsk_ref[...]
    y = y * _silu(z_ref[...])
    outs = []
    for g in range(SSM_GROUPS):
        yg = y[:, g * gw:(g + 1) * gw]
        ms = jnp.mean(yg * yg, axis=-1, keepdims=True)
        outs.append(yg * lax.rsqrt(ms + NORM_EPS))
    y_ref[...] = (jnp.concatenate(outs, axis=1) * nrm_ref[...]).astype(BF16)


def _ssd(xp, cw, cb, dtb, alog, dsk, nrm, e):
    s = xp.shape[0]
    L = SSM_CHUNK
    bcw = 2 * SSM_GROUPS * SSM_STATE
    cwx, cwb = cw[:, :SSM_INNER], cw[:, SSM_INNER:]
    cbx, cbb = cb[:, :SSM_INNER], cb[:, SSM_INNER:]
    fix = lambda i: (0, 0)
    return pl.pallas_call(
        _ssd_kernel,
        grid=(s // L,),
        in_specs=[
            pl.BlockSpec((L, SSM_INNER), lambda i: (i, EV_Z // SSM_INNER)),
            pl.BlockSpec((L, SSM_INNER), lambda i: (i, EV_XS // SSM_INNER)),
            pl.BlockSpec((L, bcw), lambda i: (i, EV_BC // bcw)),
            pl.BlockSpec((L, LANES), lambda i: (i, EV_DT // LANES)),
            pl.BlockSpec(cwx.shape, fix), pl.BlockSpec(cbx.shape, fix),
            pl.BlockSpec(cwb.shape, fix), pl.BlockSpec(cbb.shape, fix),
            pl.BlockSpec((1, LANES), fix), pl.BlockSpec((1, LANES), fix),
            pl.BlockSpec((1, SSM_INNER), fix), pl.BlockSpec((1, SSM_INNER), fix),
            pl.BlockSpec(e.shape, fix),
        ],
        out_specs=pl.BlockSpec((L, SSM_INNER), lambda i: (i, 0)),
        out_shape=jax.ShapeDtypeStruct((s, SSM_INNER), BF16),
        scratch_shapes=[
            pltpu.VMEM((L + 2 * SUBLANES, SSM_INNER), F32),
            pltpu.VMEM((L + 2 * SUBLANES, bcw), F32),
            pltpu.VMEM((SSM_GROUPS, SSM_STATE, SSM_INNER // SSM_GROUPS), F32),
        ],
        compiler_params=_cparams(("arbitrary",)),
        name="ssd_scan",
    )(xp, xp, xp, xp, cwx, cbx, cwb, cbb, dtb, alog, dsk, nrm, e)


def _swa_kernel(sink_ref, q_ref, kvc_ref, kvp_ref, cq_ref, sq_ref, cp_ref, sp_ref, o_ref):
    i = pl.program_id(0)
    B = ATTN_BLOCK
    lane = lax.broadcasted_iota(I32, (B, LANES), 1)
    lo_half = lane < ROPE_DIM
    cq, sq = cq_ref[...], sq_ref[...]
    kc = _rope_tile(kvc_ref[:, 0:LANES], cq, sq)
    kp = _rope_tile(kvp_ref[:, 0:LANES], cp_ref[...], sp_ref[...])
    kcat = jnp.concatenate([kp, kc], axis=0)
    kmat = (kcat.astype(BF16), pltpu.roll(kcat, ROPE_DIM, 1).astype(BF16))
    vcat = jnp.concatenate([kvp_ref[:, LANES:2 * LANES], kvc_ref[:, LANES:2 * LANES]], axis=0)
    vrol = pltpu.roll(vcat, ROPE_DIM, 1)
    lane2 = lax.broadcasted_iota(I32, (2 * B, LANES), 1) < ROPE_DIM
    vdup = (jnp.where(lane2, vcat, vrol).astype(BF16), jnp.where(lane2, vrol, vcat).astype(BF16))
    r = lax.broadcasted_iota(I32, (B, 2 * B), 0)
    col = lax.broadcasted_iota(I32, (B, 2 * B), 1)
    mask = (col > r) & (col <= r + B) & ((i > 0) | (col >= B))
    scale = ROPE_DIM ** -0.5
    hpg = SWA_Q_HEADS // SWA_KV_HEADS
    for j in range(SWA_Q_HEADS // 2):
        g = (2 * j) // hpg
        qt = _rope_tile(q_ref[:, j * LANES:(j + 1) * LANES], cq, sq)
        outs = []
        for par in (0, 1):
            h = 2 * j + par
            qm = jnp.where(lo_half if par == 0 else jnp.logical_not(lo_half), qt, 0.0).astype(BF16)
            logit = _dot_nt(qm, kmat[0] if par == g else kmat[1]) * scale
            logit = jnp.where(mask, logit, -jnp.inf)
            sink = sink_ref[h]
            m = jnp.maximum(jnp.max(logit, axis=-1, keepdims=True), sink)
            ex = jnp.exp(logit - m)
            prob = ex / (jnp.sum(ex, axis=-1, keepdims=True) + jnp.exp(sink - m))
            outs.append(_dot(prob.astype(BF16), vdup[g]))
        o_ref[:, j * LANES:(j + 1) * LANES] = jnp.where(lo_half, outs[0], outs[1]).astype(BF16)


def _swa(xp, sinks, cos_t, sin_t):
    s = xp.shape[0]
    B = ATTN_BLOCK
    qw = SWA_Q_HEADS * ROPE_DIM
    kvw = 2 * SWA_KV_HEADS * ROPE_DIM
    prev = lambda i: (jnp.maximum(i - 1, 0), 0)
    cur = lambda i: (i, 0)
    return pl.pallas_call(
        _swa_kernel,
        grid=(s // B,),
        in_specs=[
            pl.BlockSpec(memory_space=pltpu.SMEM),
            pl.BlockSpec((B, qw), lambda i: (i, EV_Q // qw)),
            pl.BlockSpec((B, kvw), lambda i: (i, EV_KV // kvw)),
            pl.BlockSpec((B, kvw), lambda i: (jnp.maximum(i - 1, 0), EV_KV // kvw)),
            pl.BlockSpec((B, LANES), cur), pl.BlockSpec((B, LANES), cur),
            pl.BlockSpec((B, LANES), prev), pl.BlockSpec((B, LANES), prev),
        ],
        out_specs=pl.BlockSpec((B, qw), cur),
        out_shape=jax.ShapeDtypeStruct((s, qw), BF16),
        compiler_params=_cparams(("parallel",)),
        name="swa_sink",
    )(sinks, xp, xp, xp, cos_t, sin_t, cos_t, sin_t)


def _dsa_prep_kernel(qn_ref, qr_ref, qi_ref, ckv_ref, kk_ref, wi_ref, wuk_ref, kvn_ref, c_ref, s_ref,
                     qlat_ref, qrope_ref, qidx_ref, ckvn_ref, kr_ref, ki_ref, wis_ref):
    c, s = c_ref[...], s_ref[...]
    lane = lax.broadcasted_iota(I32, c.shape, 1)
    lo_half = lane < ROPE_DIM
    for h in range(MLA_HEADS):
        qn = qn_ref[:, h * MLA_NOPE:(h + 1) * MLA_NOPE].astype(BF16)
        qlat_ref[h] = _dot(qn, wuk_ref[h]).astype(BF16)
    for src, dst in ((qr_ref, qrope_ref), (qi_ref, qidx_ref)):
        for j in range(MLA_HEADS // 2):
            t = _rope_tile(src[:, j * LANES:(j + 1) * LANES], c, s)
            dst[2 * j] = jnp.where(lo_half, t, 0.0).astype(BF16)
            dst[2 * j + 1] = jnp.where(lo_half, pltpu.roll(t, ROPE_DIM, 1), 0.0).astype(BF16)
    kk = _rope_tile(kk_ref[...], c, s)
    kr_ref[...] = jnp.where(lo_half, kk, 0.0).astype(BF16)
    ki_ref[...] = jnp.where(lo_half, pltpu.roll(kk, ROPE_DIM, 1), 0.0).astype(BF16)
    ckv = ckv_ref[...]
    ms = jnp.mean(ckv * ckv, axis=-1, keepdims=True)
    ckvn_ref[...] = (ckv * lax.rsqrt(ms + NORM_EPS) * kvn_ref[...]).astype(BF16)
    wis_ref[...] = wi_ref[...] * (IDX_HEADS ** -0.5 * IDX_DIM ** -0.5)


def _dsa_prep(xp, wuk, kvn, cos_t, sin_t):
    s = xp.shape[0]
    tm = min(256, s)
    H = MLA_HEADS
    fix2 = lambda i: (0, 0)
    hrow = lambda i: (0, i, 0)
    row = lambda i: (i, 0)
    return pl.pallas_call(
        _dsa_prep_kernel,
        grid=(s // tm,),
        in_specs=[
            pl.BlockSpec((tm, 2048), lambda i: (i, OD_QN // 2048)),
            pl.BlockSpec((tm, 1024), lambda i: (i, OD_QR // 1024)),
            pl.BlockSpec((tm, 1024), lambda i: (i, OD_QI // 1024)),
            pl.BlockSpec((tm, MLA_RANK), lambda i: (i, OD_CKV // MLA_RANK)),
            pl.BlockSpec((tm, LANES), lambda i: (i, OD_KK // LANES)),
            pl.BlockSpec((tm, LANES), lambda i: (i, OD_WI // LANES)),
            pl.BlockSpec(wuk.shape, lambda i: (0, 0, 0)),
            pl.BlockSpec((1, MLA_RANK), fix2),
            pl.BlockSpec((tm, LANES), row), pl.BlockSpec((tm, LANES), row),
        ],
        out_specs=[
            pl.BlockSpec((H, tm, MLA_RANK), hrow),
            pl.BlockSpec((H, tm, LANES), hrow),
            pl.BlockSpec((H, tm, LANES), hrow),
            pl.BlockSpec((tm, MLA_RANK), row),
            pl.BlockSpec((tm, LANES), row),
            pl.BlockSpec((tm, LANES), row),
            pl.BlockSpec((tm, LANES), row),
        ],
        out_shape=[
            jax.ShapeDtypeStruct((H, s, MLA_RANK), BF16),
            jax.ShapeDtypeStruct((H, s, LANES), BF16),
            jax.ShapeDtypeStruct((H, s, LANES), BF16),
            jax.ShapeDtypeStruct((s, MLA_RANK), BF16),
            jax.ShapeDtypeStruct((s, LANES), BF16),
            jax.ShapeDtypeStruct((s, LANES), BF16),
            jax.ShapeDtypeStruct((s, LANES), F32),
        ],
        compiler_params=_cparams(("parallel",)),
        name="dsa_prep",
    )(xp, xp, xp, xp, xp, xp, wuk, kvn, cos_t, sin_t)


def _dsa_select_kernel(topk, qi_ref, wi_ref, ki_ref, bias_ref, key_ref, dig_ref):
    i = pl.program_id(0)
    Q, KC = DSA_SEL_Q, DSA_KC
    H = IDX_HEADS
    n_chunks = bias_ref.shape[0]
    n_vis = ((i + 1) * Q + KC - 1) // KC
    wi = wi_ref[...]
    qpos = i * Q + lax.broadcasted_iota(I32, (KC, Q), 1)
    kloc = lax.broadcasted_iota(I32, (KC, Q), 0)
    HG = DSA_SEL_HEADS_PER_DOT
    N_ACC = 4
    DIGIT_BITS = (11, 11, 10)
    DIGIT_SHIFT = (21, 10, 0)
    TOP_BIAS = 1 << (DIGIT_BITS[0] - 1)
    GUARD = jnp.int32(-0x7FFF8000)
    FIELD_ONES = jnp.int32(0x00010001)
    HK = KC // 2

    def pack(lo, hi):
        return lo | (hi << 16) | GUARD

    def score_chunk(c, carry):
        k = ki_ref[pl.ds(pl.multiple_of(c * KC, KC), KC), :]
        acc = jnp.zeros((KC, Q), F32)
        for g in range(H // HG):
            sc = _dot_nt(k, qi_ref[g * HG:(g + 1) * HG].reshape(HG * Q, LANES))
            for hh in range(HG):
                h = g * HG + hh
                acc = acc + jnp.maximum(sc[:, hh * Q:(hh + 1) * Q], 0.0) * wi[h:h + 1, :]
        acc = jnp.where(c * KC + kloc <= qpos, acc, -jnp.inf)
        bits = pltpu.bitcast(acc, I32)
        key = bits ^ ((bits >> 31) & jnp.int32(0x7FFFFFFF))
        key_ref[c] = key
        for d in range(len(DIGIT_BITS)):
            dig = (key >> DIGIT_SHIFT[d]) + TOP_BIAS if d == 0 else (key >> DIGIT_SHIFT[d]) & ((1 << DIGIT_BITS[d]) - 1)
            dig_ref[d, c] = pack(dig[:HK] + 1, dig[HK:] + 1)
        return carry

    lax.fori_loop(0, n_vis, score_chunk, 0)

    def count_ge(d, cand):
        cand2 = (cand + 1) | ((cand + 1) << 16)

        def body(c, accs):
            v = dig_ref[d, c]
            accs = list(accs)
            for r in range(HK // SUBLANES):
                w = v[r * SUBLANES:(r + 1) * SUBLANES, :] - cand2
                accs[r % N_ACC] = accs[r % N_ACC] + (lax.shift_right_logical(w, 15) & FIELD_ONES)
            return tuple(accs)
        accs = lax.fori_loop(0, n_vis, body, tuple(jnp.zeros((SUBLANES, Q), I32) for _ in range(N_ACC)))
        acc = sum(accs)
        cnt = (acc & 0xFFFF) + lax.shift_right_logical(acc, 16)
        return jnp.sum(cnt.astype(F32), axis=0, keepdims=True)

    def search(d, need):
        def bit_step(b, thr):
            cand = thr | (jnp.int32(1) << (DIGIT_BITS[d] - 1 - b))
            return jnp.where(count_ge(d, cand) >= need, cand, thr)

        return lax.fori_loop(0, DIGIT_BITS[d], bit_step, jnp.zeros((1, Q), I32))

    def drop_unless_equal(d, p):
        def body(c, carry):
            cur, nxt = dig_ref[d, c], dig_ref[d + 1, c]
            lo = jnp.where((cur & 0x7FFF) == p + 1, nxt & 0x7FFF, 0)
            hi = jnp.where((lax.shift_right_logical(cur, 16) & 0x7FFF) == p + 1, lax.shift_right_logical(nxt, 16) & 0x7FFF, 0)
            dig_ref[d + 1, c] = pack(lo, hi)
            return carry

        lax.fori_loop(0, n_vis, body, 0)

    need = jnp.full((1, Q), float(topk), F32)
    thr = jnp.zeros((1, Q), I32)
    for d in range(len(DIGIT_BITS)):
        p = search(d, need)
        thr = thr + ((p - (TOP_BIAS if d == 0 else 0)) << DIGIT_SHIFT[d])
        if d + 1 < len(DIGIT_BITS):
            need = need - count_ge(d, p + 1)
            drop_unless_equal(d, p)

    def emit(c, cnt):
        sel = (key_ref[c] >= thr) & (c * KC + kloc <= qpos)
        bias_ref[c] = jnp.where(sel, 0.0, MASK_NEG).T
        return cnt + jnp.sum(jnp.where(sel, 1.0, 0.0), axis=0, keepdims=True)

    n_sel = lax.fori_loop(0, n_vis, emit, jnp.zeros((1, Q), F32))

    @pl.when(jnp.max(n_sel) > topk)
    def _():
        def count(pred):
            def body(c, cnt):
                return cnt + jnp.sum(jnp.where(pred(c), 1.0, 0.0), axis=0, keepdims=True)
            return lax.fori_loop(0, n_vis, body, jnp.zeros((1, Q), F32))

        def tied(c):
            return (key_ref[c] == thr) & (c * KC + kloc <= qpos)

        need_tied = topk - count(lambda c: key_ref[c] > thr)

        def bit_step(b, last):
            cand = last | (jnp.int32(1) << (index_bits - 1 - b))
            in_front = count(lambda c: tied(c) & (c * KC + kloc < cand))
            return jnp.where(in_front < need_tied, cand, last)

        index_bits = (n_chunks * KC - 1).bit_length()
        last = lax.fori_loop(0, index_bits, bit_step, jnp.zeros((1, Q), I32))

        def emit_ties(c, carry):
            sel = (key_ref[c] > thr) | (tied(c) & (c * KC + kloc <= last))
            bias_ref[c] = jnp.where(sel, 0.0, MASK_NEG).T
            return carry

        lax.fori_loop(0, n_vis, emit_ties, 0)

    def fill(c, carry):
        bias_ref[c] = jnp.full((Q, KC), MASK_NEG, F32)
        return carry

    lax.fori_loop(n_vis, n_chunks, fill, 0)


def _dsa_select(qidx, wis, ki, topk):
    H, s, _ = qidx.shape
    Q, KC = DSA_SEL_Q, DSA_KC
    assert Q == LANES
    nch = s // KC
    return pl.pallas_call(
        functools.partial(_dsa_select_kernel, topk),
        grid=(s // Q,),
        in_specs=[
            pl.BlockSpec((H, Q, LANES), lambda i: (0, i, 0)),
            pl.BlockSpec((LANES, Q), lambda i: (0, i)),
            pl.BlockSpec((s, LANES), lambda i: (0, 0)),
        ],
        out_specs=pl.BlockSpec((nch, Q, KC), lambda i: (0, i, 0)),
        out_shape=jax.ShapeDtypeStruct((nch, s, KC), F32),
        scratch_shapes=[pltpu.VMEM((nch, KC, Q), I32), pltpu.VMEM((3, nch, KC // 2, Q), I32)],
        compiler_params=_cparams(("parallel",)),
        name="dsa_select",
    )(qidx, wis.T, ki)


def _dsa_attn_kernel(ql_ref, qr_ref, bias_ref, ckv_ref, kr_ref, o_ref, m_ref, l_ref, acc_ref, s_ref):
    i = pl.program_id(0)
    Q, KC, H = DSA_ATT_Q, DSA_KC, MLA_HEADS
    n_vis = ((i + 1) * Q + KC - 1) // KC
    m_ref[...] = jnp.full(m_ref.shape, -jnp.inf, F32)
    l_ref[...] = jnp.zeros(l_ref.shape, F32)
    acc_ref[...] = jnp.zeros(acc_ref.shape, F32)

    def keys(c):
        off = pl.multiple_of(c * KC, KC)
        return ckv_ref[pl.ds(off, KC), :], kr_ref[pl.ds(off, KC), :]

    def raw_logits(c):
        ck, kr = keys(c)
        ql = ql_ref[...].reshape(H * Q, MLA_RANK)
        qr = qr_ref[...].reshape(H * Q, LANES)
        return _dot_nt(ql, ck) + _dot_nt(qr, kr)

    def consume(c, slot):
        logit = s_ref[slot] * MLA_SCALE
        logit = (logit.reshape(H, Q, KC) + bias_ref[c][None]).reshape(H * Q, KC)
        m_old = m_ref[...]
        m_new = jnp.maximum(m_old, jnp.max(logit, axis=-1, keepdims=True))
        alpha = jnp.exp(m_old - m_new)
        p = jnp.exp(logit - m_new)
        l_ref[...] = alpha * l_ref[...] + jnp.sum(p, axis=-1, keepdims=True)
        acc_ref[...] = alpha * acc_ref[...] + _dot(p.astype(BF16), keys(c)[0])
        m_ref[...] = m_new

    s_ref[0] = raw_logits(0)

    def pair(j, carry):
        c = 2 * j
        s_ref[1] = raw_logits(c + 1)
        consume(c, 0)
        s_ref[0] = raw_logits(c + 2)
        consume(c + 1, 1)
        return carry

    n_pairs = (n_vis - 1) // 2
    lax.fori_loop(0, n_pairs, pair, 0)
    last = 2 * n_pairs

    @pl.when(last + 1 < n_vis)
    def _():
        s_ref[1] = raw_logits(last + 1)
        consume(last, 0)
        consume(last + 1, 1)

    @pl.when(last + 1 == n_vis)
    def _():
        consume(last, 0)

    o_ref[...] = (acc_ref[...] / l_ref[...]).reshape(H, Q, MLA_RANK).astype(BF16)


def _dsa_attn(qlat, qrope, bias, ckvn, kr):
    H, s, _ = qlat.shape
    Q, KC = DSA_ATT_Q, DSA_KC
    nch = s // KC
    hrow = lambda i: (0, i, 0)
    fix = lambda i: (0, 0)
    return pl.pallas_call(
        _dsa_attn_kernel,
        grid=(s // Q,),
        in_specs=[
            pl.BlockSpec((H, Q, MLA_RANK), hrow),
            pl.BlockSpec((H, Q, LANES), hrow),
            pl.BlockSpec((nch, Q, KC), hrow),
            pl.BlockSpec((s, MLA_RANK), fix),
            pl.BlockSpec((s, LANES), fix),
        ],
        out_specs=pl.BlockSpec((H, Q, MLA_RANK), hrow),
        out_shape=jax.ShapeDtypeStruct((H, s, MLA_RANK), BF16),
        scratch_shapes=[
            pltpu.VMEM((H * Q, 1), F32),
            pltpu.VMEM((H * Q, 1), F32),
            pltpu.VMEM((H * Q, MLA_RANK), F32),
            pltpu.VMEM((2, H * Q, KC), F32),
        ],
        compiler_params=_cparams(("parallel",)),
        name="dsa_attn",
    )(qlat, qrope, bias, ckvn, kr)


def _odd_out_kernel(ol_ref, wuv_ref, wo_ref, x_ref, g_ref, b_ref, o_ref, of_ref, u_ref):
    for h in range(MLA_HEADS):
        u_ref[:, h * MLA_V:(h + 1) * MLA_V] = _dot(ol_ref[h], wuv_ref[h]).astype(BF16)
    mix = _dot(u_ref[...], wo_ref[...])
    _store_row_layouts(_layer_norm(DEEPNORM_ALPHA * x_ref[...] + mix, g_ref[...], b_ref[...]), o_ref, of_ref)


def _odd_out(olat, wuv, wo, x, g, b):
    H, s, _ = olat.shape
    tm = min(256, s)
    row = lambda i: (i, 0)
    fix = lambda i: (0, 0)
    out_specs, out_shape = _row_layout_outputs(s, tm)
    return pl.pallas_call(
        _odd_out_kernel,
        grid=(s // tm,),
        in_specs=[
            pl.BlockSpec((H, tm, MLA_RANK), lambda i: (0, i, 0)),
            pl.BlockSpec(wuv.shape, lambda i: (0, 0, 0)),
            pl.BlockSpec(wo.shape, fix),
            pl.BlockSpec((tm, D_MODEL), row),
            pl.BlockSpec((1, D_MODEL), fix), pl.BlockSpec((1, D_MODEL), fix),
        ],
        out_specs=out_specs,
        out_shape=out_shape,
        scratch_shapes=[pltpu.VMEM((tm, MLA_HEADS * MLA_V), BF16)],
        compiler_params=_cparams(("parallel",)),
        name="odd_out_ln",
    )(olat, wuv, wo, x, g, b)


ROUTE_LANE0 = MOE_GROUPS


def _router_kernel(h_ref, wr_ref, br_ref, info_ref, cnt_ref, run_ref):
    i = pl.program_id(0)

    @pl.when(i == 0)
    def _():
        run_ref[...] = jnp.zeros(run_ref.shape, F32)

    h = h_ref[...]
    tm = h.shape[0]
    h_hi = h.astype(BF16)
    h_lo = (h - h_hi.astype(F32)).astype(BF16)
    w = wr_ref[...]
    w_hi = w.astype(BF16)
    w_lo = (w - w_hi.astype(F32)).astype(BF16)
    logits = _dot(h_hi, w_hi) + (_dot(h_hi, w_lo) + _dot(h_lo, w_hi)) + br_ref[...]
    lane = lax.broadcasted_iota(I32, (tm, LANES), 1)
    lane_f = lane.astype(F32)
    neg = -jnp.inf
    big = float(LANES)
    is_grp = lane < MOE_GROUPS
    gl = jnp.where(is_grp, logits, neg)
    gmax = jnp.max(gl, axis=-1, keepdims=True)
    gsel = jnp.min(jnp.where(gl == gmax, lane_f, big), axis=-1, keepdims=True)
    gsum = jnp.sum(jnp.where(is_grp, jnp.exp(logits - gmax), 0.0), axis=-1, keepdims=True)
    egrp = ((lane - ROUTE_LANE0) >> 3).astype(F32)
    valid = (lane >= ROUTE_LANE0) & (lane < ROUTE_LANE0 + MOE_EXPERTS) & (egrp == gsel)
    el = jnp.where(valid, logits, neg)
    v1 = jnp.max(el, axis=-1, keepdims=True)
    i1 = jnp.min(jnp.where(el == v1, lane_f, big), axis=-1, keepdims=True)
    el2 = jnp.where(lane_f == i1, neg, el)
    v2 = jnp.max(el2, axis=-1, keepdims=True)
    i2 = jnp.min(jnp.where(el2 == v2, lane_f, big), axis=-1, keepdims=True)
    t = jnp.exp(v2 - v1)
    p1 = 1.0 / (1.0 + t)
    p2 = t / (1.0 + t)
    ggate = 1.0 / gsum
    m1 = lane_f == i1
    m2 = lane_f == i2
    memb = jnp.where(m1 | m2, 1.0, 0.0)
    tri = (lax.broadcasted_iota(I32, (tm, tm), 0) > lax.broadcasted_iota(I32, (tm, tm), 1))
    cum = _dot(jnp.where(tri, 1.0, 0.0).astype(BF16), memb.astype(BF16)) + run_ref[...]
    rank1 = jnp.sum(jnp.where(m1, cum, 0.0), axis=-1, keepdims=True)
    rank2 = jnp.sum(jnp.where(m2, cum, 0.0), axis=-1, keepdims=True)
    run_ref[...] = run_ref[...] + jnp.sum(memb, axis=0, keepdims=True)
    info = jnp.where(lane == 0, i1 - ROUTE_LANE0, 0.0)
    info = jnp.where(lane == 1, i2 - ROUTE_LANE0, info)
    info = jnp.where(lane == 2, p1 * ggate, info)
    info = jnp.where(lane == 3, p2 * ggate, info)
    info = jnp.where(lane == 4, rank1, info)
    info = jnp.where(lane == 5, rank2, info)
    info_ref[...] = info
    cnt_ref[...] = run_ref[...]


def _router(h, wr, br):
    t = h.shape[0]
    tm = min(512, t)
    return pl.pallas_call(
        _router_kernel,
        grid=(t // tm,),
        in_specs=[
            pl.BlockSpec((tm, D_MODEL), lambda i: (i, 0)),
            pl.BlockSpec(wr.shape, lambda i: (0, 0)),
            pl.BlockSpec((1, LANES), lambda i: (0, 0)),
        ],
        out_specs=[pl.BlockSpec((tm, LANES), lambda i: (i, 0)), pl.BlockSpec((1, LANES), lambda i: (0, 0))],
        out_shape=[jax.ShapeDtypeStruct((t, LANES), F32), jax.ShapeDtypeStruct((1, LANES), F32)],
        scratch_shapes=[pltpu.VMEM((1, LANES), F32)],
        compiler_params=_cparams(("arbitrary",)),
        name="moe_router",
    )(h, wr, br)


def _plan_kernel(info_ref, ps_ref, pos_ref):
    info = info_ref[...]
    lane = lax.broadcasted_iota(I32, info.shape, 1)
    lane_f = lane.astype(F32)
    ps = ps_ref[...]
    pos1 = jnp.sum(jnp.where(lane_f == info[:, 0:1], ps, 0.0), axis=-1, keepdims=True) + info[:, 4:5]
    pos2 = jnp.sum(jnp.where(lane_f == info[:, 1:2], ps, 0.0), axis=-1, keepdims=True) + info[:, 5:6]
    pos_ref[...] = jnp.where(lane == 0, pos1, jnp.where(lane == 1, pos2, 0.0)).astype(I32)


def _plan(info, pad_start):
    t = info.shape[0]
    tm = min(1024, t)
    return pl.pallas_call(
        _plan_kernel,
        grid=(t // tm,),
        in_specs=[pl.BlockSpec((tm, LANES), lambda i: (i, 0)), pl.BlockSpec((1, LANES), lambda i: (0, 0))],
        out_specs=pl.BlockSpec((tm, LANES), lambda i: (i, 0)),
        out_shape=jax.ShapeDtypeStruct((t, LANES), I32),
        compiler_params=_cparams(("parallel",)),
        name="moe_plan",
    )(info, pad_start)


def _invert_kernel(pos_ref, rt_ref):
    def clear(r, carry):
        rt_ref[r] = 0
        return carry

    lax.fori_loop(0, rt_ref.shape[0], clear, 0, unroll=8)

    def put(n, carry):
        rt_ref[pos_ref[n]] = lax.shift_right_logical(n, 1)
        return carry

    lax.fori_loop(0, pos_ref.shape[0], put, 0, unroll=8)


def _invert(pos_flat, n_rows):
    return pl.pallas_call(
        _invert_kernel,
        in_specs=[pl.BlockSpec(memory_space=pltpu.SMEM)],
        out_specs=pl.BlockSpec(memory_space=pltpu.SMEM),
        out_shape=jax.ShapeDtypeStruct((n_rows,), I32),
        name="moe_invert",
    )(pos_flat)


def _expert_kernel(be_ref, nu_ref, rt_ref, h_ref, wg_ref, wu_ref, wd_ref, o_ref, xbuf0, xbuf1, xb, wgb, wub, wdb, sem):
    b = pl.program_id(0)
    n_used = nu_ref[0]
    R = EXPERT_ROWS

    bufs = (xbuf0, xbuf1)

    def row_copy(tok, r, slot):
        src = h_ref.at[pl.ds(pl.multiple_of(tok * ROW_TILES, ROW_TILES), ROW_TILES)]
        return pltpu.make_async_copy(src, bufs[slot].at[pl.ds(r * ROW_TILES, ROW_TILES)], sem.at[slot])

    def drain(slot):
        def body(r, carry):
            row_copy(0, 0, slot).wait()
            return carry
        lax.fori_loop(0, R, body, 0, unroll=8)

    @pl.when(b == 0)
    def _():
        def body(r, carry):
            row_copy(rt_ref[r], r, 0).start(priority=GATHER_DMA_PRIORITY)
            return carry
        lax.fori_loop(0, R, body, 0, unroll=8)

    def block(slot):
        drain(slot)

        @pl.when((b == 0) | (be_ref[b] != be_ref[jnp.maximum(b - 1, 0)]))
        def _():
            wgb[...] = wg_ref[0, 0].astype(BF16)
            wub[...] = wu_ref[0, 0].astype(BF16)
            wdb[...] = wd_ref[0, 0].astype(BF16)

        base = (b + 1) * R
        for r in range(R):
            row_copy(rt_ref[base + r], r, 1 - slot).start(priority=GATHER_DMA_PRIORITY)
        for j in range(ROW_TILES):
            xb[:, j * LANES:(j + 1) * LANES] = bufs[slot][pl.ds(j, R, stride=ROW_TILES), :].astype(BF16)
        x = xb[...]
        hid = _silu(_dot(x, wgb[...])) * _dot(x, wub[...])
        y = _dot(hid.astype(BF16), wdb[...])
        for j in range(ROW_TILES):
            o_ref[pl.ds(j, R, stride=ROW_TILES), :] = y[:, j * LANES:(j + 1) * LANES]

    for parity in (0, 1):
        pl.when((b < n_used) & (b % 2 == parity))(functools.partial(block, parity))

    @pl.when(b >= n_used)
    def _():
        o_ref[...] = jnp.zeros(o_ref.shape, F32)

    for parity in (0, 1):
        pl.when((b == n_used) & (b % 2 == parity))(functools.partial(drain, parity))


def _experts(blk_e, n_used, row_token, h, wg, wu, wd, layer):
    r = row_token.shape[0]
    nb = r // EXPERT_ROWS
    wmap = lambda b, be, nu, rt: (layer, be[b], 0, 0)
    return pl.pallas_call(
        _expert_kernel,
        grid_spec=pltpu.PrefetchScalarGridSpec(
            num_scalar_prefetch=3,
            grid=(nb,),
            in_specs=[
                pl.BlockSpec(memory_space=pl.ANY),
                pl.BlockSpec((1, 1, D_MODEL, MOE_FF), wmap),
                pl.BlockSpec((1, 1, D_MODEL, MOE_FF), wmap),
                pl.BlockSpec((1, 1, MOE_FF, D_MODEL), wmap),
            ],
            out_specs=pl.BlockSpec((EXPERT_ROWS * ROW_TILES, LANES), lambda b, be, nu, rt: (b, 0)),
            scratch_shapes=[
                pltpu.VMEM((EXPERT_ROWS * ROW_TILES, LANES), F32),
                pltpu.VMEM((EXPERT_ROWS * ROW_TILES, LANES), F32),
                pltpu.VMEM((EXPERT_ROWS, D_MODEL), BF16),
                pltpu.VMEM((D_MODEL, MOE_FF), BF16),
                pltpu.VMEM((D_MODEL, MOE_FF), BF16),
                pltpu.VMEM((MOE_FF, D_MODEL), BF16),
                pltpu.SemaphoreType.DMA((2,)),
            ],
        ),
        out_shape=jax.ShapeDtypeStruct((r * ROW_TILES, LANES), F32),
        compiler_params=_cparams(("arbitrary",)),
        name="moe_experts",
    )(blk_e, n_used, row_token, h, wg, wu, wd)


COMBINE_TOKENS = 256


def _combine_ple_kernel(pos_ref, y_ref, info_ref, x_ref, g_ref, b_ref, p_ref, wg_ref, bg_ref, wp_ref, o_ref,
                        buf0, buf1, ycat, sem):
    i = pl.program_id(0)
    n = pl.num_programs(0)
    TB = x_ref.shape[0]
    bufs = (buf0, buf1)

    def row_copy(src_row, k, tt, slot):
        src = y_ref.at[pl.ds(pl.multiple_of(src_row * ROW_TILES, ROW_TILES), ROW_TILES)]
        return pltpu.make_async_copy(src, bufs[slot].at[k, pl.ds(tt * ROW_TILES, ROW_TILES)], sem.at[slot])

    def drain(slot):
        def body(tt, carry):
            row_copy(0, 0, 0, slot).wait()
            row_copy(0, 1, 0, slot).wait()
            return carry
        lax.fori_loop(0, TB, body, 0, unroll=8)

    @pl.when(i == 0)
    def _():
        def body(tt, carry):
            row_copy(pos_ref[2 * tt], 0, tt, 0).start(priority=0)
            row_copy(pos_ref[2 * tt + 1], 1, tt, 0).start(priority=1)
            return carry
        lax.fori_loop(0, TB, body, 0, unroll=8)

    def block(slot):
        drain(slot)
        base = jnp.minimum(i + 1, n - 1) * (2 * TB)
        for tt in range(TB):
            row_copy(pos_ref[base + 2 * tt], 0, tt, 1 - slot).start(priority=0)
            row_copy(pos_ref[base + 2 * tt + 1], 1, tt, 1 - slot).start(priority=1)
        info = info_ref[...]
        g0, g1 = info[:, 2:3], info[:, 3:4]
        for j in range(ROW_TILES):
            rows = pl.ds(j, TB, stride=ROW_TILES)
            ycat[:, j * LANES:(j + 1) * LANES] = bufs[slot][0, rows, :] * g0 + bufs[slot][1, rows, :] * g1
        x2 = _layer_norm(DEEPNORM_ALPHA * x_ref[...] + ycat[...], g_ref[...], b_ref[...])
        gate = jax.nn.sigmoid(_dot(x2.astype(BF16), wg_ref[...]) + bg_ref[...])
        o_ref[...] = x2 + gate * _dot(p_ref[...].astype(BF16), wp_ref[...])

    for parity in (0, 1):
        pl.when(i % 2 == parity)(functools.partial(block, parity))
    for parity in (0, 1):
        pl.when((i == n - 1) & (i % 2 == parity))(functools.partial(drain, 1 - parity))


def _combine_ple(pos_flat, y_rows, info, x, g, b, p, wg, bg, wp):
    t = x.shape[0]
    TB = min(COMBINE_TOKENS, t)
    row = lambda i, pos: (i, 0)
    fix = lambda i, pos: (0, 0)
    return pl.pallas_call(
        _combine_ple_kernel,
        grid_spec=pltpu.PrefetchScalarGridSpec(
            num_scalar_prefetch=1,
            grid=(t // TB,),
            in_specs=[
                pl.BlockSpec(memory_space=pl.ANY),
                pl.BlockSpec((TB, LANES), row),
                pl.BlockSpec((TB, D_MODEL), row),
                pl.BlockSpec((1, D_MODEL), fix), pl.BlockSpec((1, D_MODEL), fix),
                pl.BlockSpec((TB, PLE_DIM), row),
                pl.BlockSpec(wg.shape, fix),
                pl.BlockSpec((1, D_MODEL), fix),
                pl.BlockSpec(wp.shape, fix),
            ],
            out_specs=pl.BlockSpec((TB, D_MODEL), row),
            scratch_shapes=[
                pltpu.VMEM((2, TB * ROW_TILES, LANES), F32),
                pltpu.VMEM((2, TB * ROW_TILES, LANES), F32),
                pltpu.VMEM((TB, D_MODEL), F32),
                pltpu.SemaphoreType.DMA((2,)),
            ],
        ),
        out_shape=jax.ShapeDtypeStruct((t, D_MODEL), F32),
        compiler_params=_cparams(("arbitrary",)),
        name="moe_combine_ln_ple",
    )(pos_flat, y_rows, info, x, g, b, p, wg, bg, wp)


def _hier_moe_ln_ple(x, x_folded, wr, br, wg, wu, wd, layer, ln_g, ln_b, p, ple_wg, ple_bg, ple_wp):
    t = x.shape[0]
    info, cnt = _router(x, wr, br)
    counts = cnt[0, ROUTE_LANE0:ROUTE_LANE0 + MOE_EXPERTS].astype(I32)
    padded = (counts + EXPERT_ROWS - 1) // EXPERT_ROWS * EXPERT_ROWS
    pad_end = jnp.cumsum(padded)
    pad_start = jnp.zeros((1, LANES), F32).at[0, :MOE_EXPERTS].set((pad_end - padded).astype(F32))
    n_rows = 2 * t + MOE_EXPERTS * EXPERT_ROWS
    n_blocks = n_rows // EXPERT_ROWS
    blk_start = jnp.arange(n_blocks, dtype=I32) * EXPERT_ROWS
    blk_e = jnp.minimum(jnp.sum(pad_end[None, :] <= blk_start[:, None], axis=1), MOE_EXPERTS - 1).astype(I32)
    n_used = (pad_end[-1:] // EXPERT_ROWS).astype(I32)
    pos = _plan(info, pad_start)[:, :2].reshape(-1)
    y_rows = _experts(blk_e, n_used, _invert(pos, n_rows), x_folded, wg, wu, wd, layer)
    return _combine_ple(pos, y_rows, info, x, ln_g, ln_b, p, ple_wg, ple_bg, ple_wp)


def _rope_tables(positions):
    inv = ROPE_THETA ** (-jnp.arange(0, ROPE_DIM, 2, dtype=F32) / ROPE_DIM)
    ang = positions.astype(F32)[:, None] * inv
    cos, sin = jnp.cos(ang), jnp.sin(ang)
    cos_t = jnp.tile(cos, (1, LANES // (ROPE_DIM // 2)))
    sin_t = jnp.tile(jnp.concatenate([-sin, sin], axis=1), (1, LANES // ROPE_DIM))
    return cos_t, sin_t


def _pad_cols(w, n):
    return jnp.pad(w, ((0, 0), (0, n - w.shape[1])))


def _cat_bf16(parts, n):
    parts = [p.astype(BF16) for p in parts]
    used = sum(p.shape[1] for p in parts)
    return jnp.concatenate(parts + [jnp.zeros((parts[0].shape[0], n - used), BF16)], axis=1)


def _even_w_in(w):
    z, xbc, dt, q, kv = jnp.split(w, [2048, 5120, 5152, 6176], axis=1)
    return _cat_bf16([z, xbc, q, kv, dt], EV_NP)


def _odd_w_in(w):
    q, ckv, krope, qi, ki, wi = jnp.split(w, [3072, 3584, 3648, 4672, 4736], axis=1)
    q = q.reshape(D_MODEL, MLA_HEADS, MLA_NOPE + MLA_ROPE)
    qn = q[:, :, :MLA_NOPE].reshape(D_MODEL, -1)
    qr = q[:, :, MLA_NOPE:].reshape(D_MODEL, -1)
    return _cat_bf16([qn, qr, qi, ckv, krope, ki, wi], OD_NP)


def _head_expand_matrix():
    e = np.zeros((LANES, SSM_INNER), np.float32)
    for h in range(SSM_HEADS):
        e[h, h * SSM_HEAD_DIM:(h + 1) * SSM_HEAD_DIM] = 1.0
    return jnp.asarray(e, BF16)


def kernel(x, p, positions, ev_w_in, ev_conv_w, ev_conv_b, ev_dt_bias, ev_a_log, ev_d_skip, ev_ssm_norm, ev_sinks, ev_w_out, od_w_in, od_kv_norm, od_w_uk, od_w_uv, od_w_out, ln1_g, ln1_b, ln2_g, ln2_b, moe_router_group, moe_router_group_b, moe_router_expert, moe_router_expert_b, moe_w_gate, moe_w_up, moe_w_down, ple_w_proj, ple_w_gate, ple_b_gate):
    batch, s, d = x.shape
    assert batch == 1 and d == D_MODEL
    xs = x[0]
    cos_t, sin_t = _rope_tables(positions[0])
    e_mat = _head_expand_matrix()
    topk = min(IDX_TOPK_MAX, s // 4)
    row = lambda v: v.reshape(1, -1)
    pad_row = lambda v: _pad_cols(v.reshape(1, -1), LANES)
    for i in range(DEPTH):
        j = i // 2
        if i % 2 == 0:
            xp = _inproj(xs, _even_w_in(ev_w_in[j]))
            y_ssm = _ssd(xp, ev_conv_w[j], row(ev_conv_b[j]), pad_row(ev_dt_bias[j]), pad_row(ev_a_log[j]),
                         row(jnp.repeat(ev_d_skip[j], SSM_HEAD_DIM)), row(ev_ssm_norm[j]), e_mat)
            y_att = _swa(xp, ev_sinks[j], cos_t, sin_t)
            w_out = ev_w_out[j].astype(BF16)
            xs, xf = _even_out(y_ssm, y_att, w_out[:SSM_INNER], w_out[SSM_INNER:], xs, row(ln1_g[i]), row(ln1_b[i]))
        else:
            xp = _inproj(xs, _odd_w_in(od_w_in[j]))
            qlat, qrope, qidx, ckvn, kr, ki, wis = _dsa_prep(
                xp, od_w_uk[j].astype(BF16), row(od_kv_norm[j]), cos_t, sin_t)
            bias = _dsa_select(qidx, wis, ki, topk)
            olat = _dsa_attn(qlat, qrope, bias, ckvn, kr)
            xs, xf = _odd_out(olat, od_w_uv[j].astype(BF16), od_w_out[j].astype(BF16), xs, row(ln1_g[i]), row(ln1_b[i]))
        wr = _pad_cols(jnp.concatenate([moe_router_group[i], moe_router_expert[i]], axis=1), LANES)
        br = pad_row(jnp.concatenate([moe_router_group_b[i], moe_router_expert_b[i]]))
        xs = _hier_moe_ln_ple(xs, xf, wr, br, moe_w_gate, moe_w_up, moe_w_down, i, row(ln2_g[i]), row(ln2_b[i]),
                              p[i, 0], ple_w_gate[i].astype(BF16), row(ple_b_gate[i]), ple_w_proj[i].astype(BF16))
    return xs[None]
```

```python
import functools

import jax
import jax.numpy as jnp
import numpy as np
from jax import lax
from jax.experimental import pallas as pl
from jax.experimental.pallas import tpu as pltpu

F32 = jnp.float32
BF16 = jnp.bfloat16
I32 = jnp.int32

D_MODEL = 2048
DEPTH = 4
ROPE_THETA = 10000.0
ROPE_DIM = 64
NORM_EPS = 1e-5
SSM_HEADS = 32
SSM_HEAD_DIM = 64
SSM_INNER = SSM_HEADS * SSM_HEAD_DIM
SSM_GROUPS = 4
SSM_STATE = 128
SSM_CONV = 4
SSM_CHUNK = 128
SWA_Q_HEADS = 16
SWA_KV_HEADS = 2
ATTN_BLOCK = 128
MLA_HEADS = 16
MLA_NOPE = 128
MLA_ROPE = ROPE_DIM
MLA_V = 128
MLA_RANK = 512
MLA_SCALE = (MLA_NOPE + MLA_ROPE) ** -0.5
IDX_HEADS = 16
IDX_DIM = ROPE_DIM
IDX_TOPK_MAX = 256
MOE_GROUPS = 4
MOE_EPG = 8
MOE_EXPERTS = MOE_GROUPS * MOE_EPG
MOE_FF = 512
PLE_DIM = 256
DEEPNORM_ALPHA = (2 * DEPTH) ** 0.25

LANES = 128
SUBLANES = 8
V7X_VMEM_BYTES = 64 * 1024 * 1024
COMPILER_RESERVE_BYTES = 8 * 1024 * 1024
VMEM_LIMIT_BYTES = V7X_VMEM_BYTES - COMPILER_RESERVE_BYTES
ROW_TILES = D_MODEL // LANES

EXPERT_ROWS = 256
DSA_SEL_Q = 128
DSA_ATT_Q = 64
DSA_KC = 512
DSA_SEL_HEADS_PER_DOT = 4
MASK_NEG = -1e30

EV_Z, EV_XS, EV_BC, EV_Q, EV_KV, EV_DT = 0, 2048, 4096, 5120, 6144, 6400
EV_NP = 6656
OD_QN, OD_QR, OD_QI, OD_CKV, OD_KK, OD_WI = 0, 2048, 3072, 4096, 4608, 4736
OD_NP = 5120


def _cparams(sem, vmem=VMEM_LIMIT_BYTES):
    return pltpu.CompilerParams(dimension_semantics=sem, vmem_limit_bytes=vmem)


def _dot(a, b):
    return jnp.dot(a, b, preferred_element_type=F32)


def _dot_nt(a, b):
    return lax.dot_general(a, b, (((1,), (1,)), ((), ())), preferred_element_type=F32)


def _split3(v):
    hi = v.astype(BF16)
    r = v - hi.astype(F32)
    mid = r.astype(BF16)
    lo = (r - mid.astype(F32)).astype(BF16)
    return hi, mid, lo


def _expand(v, e):
    hi, mid, lo = _split3(v)
    return _dot(hi, e) + _dot(mid, e) + _dot(lo, e)


def _silu(v):
    return v * jax.nn.sigmoid(v)


def _layer_norm(v, g, b):
    mu = jnp.mean(v, axis=-1, keepdims=True)
    vc = v - mu
    var = jnp.mean(vc * vc, axis=-1, keepdims=True)
    return vc * lax.rsqrt(var + NORM_EPS) * g + b


def _rope_tile(t, c, s):
    lane = lax.broadcasted_iota(I32, t.shape, 1)
    first_half = (lane & 32) == 0
    swapped = jnp.where(first_half, pltpu.roll(t, LANES - 32, 1), pltpu.roll(t, 32, 1))
    return t * c + swapped * s


def _inproj_kernel(x_ref, w_ref, o_ref):
    o_ref[...] = _dot(x_ref[...].astype(BF16), w_ref[...])


def _inproj(x, w):
    m, k = x.shape
    n = w.shape[1]
    tm, tn = min(1024, m), 512
    return pl.pallas_call(
        _inproj_kernel,
        grid=(m // tm, n // tn),
        in_specs=[pl.BlockSpec((tm, k), lambda i, j: (i, 0)), pl.BlockSpec((k, tn), lambda i, j: (0, j))],
        out_specs=pl.BlockSpec((tm, tn), lambda i, j: (i, j)),
        out_shape=jax.ShapeDtypeStruct((m, n), F32),
        compiler_params=_cparams(("parallel", "arbitrary")),
        name="inproj",
    )(x, w)


def _store_row_layouts(y, o_ref, of_ref):
    o_ref[...] = y
    for j in range(ROW_TILES):
        of_ref[pl.ds(j, y.shape[0], stride=ROW_TILES), :] = y[:, j * LANES:(j + 1) * LANES]


def _row_layout_outputs(m, tm):
    specs = [pl.BlockSpec((tm, D_MODEL), lambda i: (i, 0)), pl.BlockSpec((tm * ROW_TILES, LANES), lambda i: (i, 0))]
    shapes = [jax.ShapeDtypeStruct((m, D_MODEL), F32), jax.ShapeDtypeStruct((m * ROW_TILES, LANES), F32)]
    return specs, shapes


def _even_out_kernel(a1_ref, a2_ref, w1_ref, w2_ref, x_ref, g_ref, b_ref, o_ref, of_ref):
    mix = _dot(a1_ref[...], w1_ref[...]) + _dot(a2_ref[...], w2_ref[...])
    _store_row_layouts(_layer_norm(DEEPNORM_ALPHA * x_ref[...] + mix, g_ref[...], b_ref[...]), o_ref, of_ref)


def _even_out(y_ssm, y_att, w1, w2, x, g, b):
    m = x.shape[0]
    tm = min(256, m)
    row = lambda i: (i, 0)
    fix = lambda i: (0, 0)
    out_specs, out_shape = _row_layout_outputs(m, tm)
    return pl.pallas_call(
        _even_out_kernel,
        grid=(m // tm,),
        in_specs=[
            pl.BlockSpec((tm, y_ssm.shape[1]), row),
            pl.BlockSpec((tm, y_att.shape[1]), row),
            pl.BlockSpec(w1.shape, fix),
            pl.BlockSpec(w2.shape, fix),
            pl.BlockSpec((tm, D_MODEL), row),
            pl.BlockSpec((1, D_MODEL), fix),
            pl.BlockSpec((1, D_MODEL), fix),
        ],
        out_specs=out_specs,
        out_shape=out_shape,
        compiler_params=_cparams(("parallel",)),
        name="even_out_ln",
    )(y_ssm, y_att, w1, w2, x, g, b)


def _ssd_kernel(z_ref, xs_ref, bc_ref, dt_ref, cwx_ref, cbx_ref, cwb_ref, cbb_ref, dtb_ref, alog_ref,
                dsk_ref, nrm_ref, e_ref, y_ref, xs_ext, bc_ext, st_ref):
    c = pl.program_id(0)
    L = SSM_CHUNK
    halo = SUBLANES

    @pl.when(c == 0)
    def _():
        xs_ext[0:halo, :] = jnp.zeros((halo, xs_ext.shape[1]), F32)
        bc_ext[0:halo, :] = jnp.zeros((halo, bc_ext.shape[1]), F32)
        st_ref[...] = jnp.zeros(st_ref.shape, F32)

    xs_ext[halo:halo + L, :] = xs_ref[...]
    bc_ext[halo:halo + L, :] = bc_ref[...]

    def conv(ext, w_ref, b_ref):
        acc = b_ref[...]
        for j in range(SSM_CONV):
            lo = halo - (SSM_CONV - 1) + j
            acc = acc + ext[lo:lo + L, :] * w_ref[j:j + 1, :]
        return acc

    xs = _silu(conv(xs_ext, cwx_ref, cbx_ref))
    bc = _silu(conv(bc_ext, cwb_ref, cbb_ref))
    xs_ext[0:halo, :] = xs_ext[L:L + halo, :]
    bc_ext[0:halo, :] = bc_ext[L:L + halo, :]

    pre = dt_ref[...] + dtb_ref[...]
    dt = jnp.maximum(pre, 0.0) + jnp.log1p(jnp.exp(-jnp.abs(pre)))
    a = dt * (-jnp.exp(alog_ref[...]))
    row = lax.broadcasted_iota(I32, (L, LANES), 0)
    acs = a
    s = 1
    while s < L:
        acs = acs + jnp.where(row >= s, pltpu.roll(acs, s, 0), 0.0)
        s *= 2
    a_last = acs[L - 1:L, :]
    e = e_ref[...]
    dt_x = _expand(dt, e)
    dte_x = _expand(dt * jnp.exp(a_last - acs), e)
    eacs_x = _expand(jnp.exp(acs), e)
    cd_x = _expand(jnp.broadcast_to(jnp.exp(a_last), (SUBLANES, LANES)), e)[0:1, :]
    acs_t = acs.T

    xdt = (xs * dt_x).astype(BF16)
    xd = (xs * dte_x).astype(BF16)
    tri = lax.broadcasted_iota(I32, (L, L), 0) >= lax.broadcasted_iota(I32, (L, L), 1)
    first_head = lax.broadcasted_iota(I32, (L, LANES), 1) < SSM_HEAD_DIM
    n_state = SSM_STATE
    gw = SSM_INNER // SSM_GROUPS
    ys = []
    for g in range(SSM_GROUPS):
        bg = bc[:, g * n_state:(g + 1) * n_state]
        cg = bc[:, SSM_GROUPS * n_state + g * n_state:SSM_GROUPS * n_state + (g + 1) * n_state]
        bb, cb16 = bg.astype(BF16), cg.astype(BF16)
        cbm = _dot_nt(cb16, bb)
        st = st_ref[g]
        y_off = _dot(cb16, st.astype(BF16)) * eacs_x[:, g * gw:(g + 1) * gw]
        st_ref[g] = st * cd_x[:, g * gw:(g + 1) * gw] + _dot(bg.T.astype(BF16), xd[:, g * gw:(g + 1) * gw])
        parts = []
        for j in range(gw // LANES):
            lo = g * gw + j * LANES
            xp = xdt[:, lo:lo + LANES]
            out = None
            for par in (0, 1):
                h = lo // SSM_HEAD_DIM + par
                seg = acs[:, h:h + 1] - acs_t[h:h + 1, :]
                lm = (jnp.exp(jnp.where(tri, seg, -jnp.inf)) * cbm).astype(BF16)
                xm = jnp.where(first_head if par == 0 else jnp.logical_not(first_head), xp, jnp.zeros_like(xp))
                d = _dot(lm, xm)
                out = d if out is None else out + d
            parts.append(out)
        ys.append(jnp.concatenate(parts, axis=1) + y_off)
    y = jnp.concatenate(ys, axis=1) + xs * dsk_ref[...]
    y = y * _silu(z_ref[...])
    outs = []
    for g in range(SSM_GROUPS):
        yg = y[:, g * gw:(g + 1) * gw]
        ms = jnp.mean(yg * yg, axis=-1, keepdims=True)
        outs.append(yg * lax.rsqrt(ms + NORM_EPS))
    y_ref[...] = (jnp.concatenate(outs, axis=1) * nrm_ref[...]).astype(BF16)


def _ssd(xp, cw, cb, dtb, alog, dsk, nrm, e):
    s = xp.shape[0]
    L = SSM_CHUNK
    bcw = 2 * SSM_GROUPS * SSM_STATE
    cwx, cwb = cw[:, :SSM_INNER], cw[:, SSM_INNER:]
    cbx, cbb = cb[:, :SSM_INNER], cb[:, SSM_INNER:]
    fix = lambda i: (0, 0)
    return pl.pallas_call(
        _ssd_kernel,
        grid=(s // L,),
        in_specs=[
            pl.BlockSpec((L, SSM_INNER), lambda i: (i, EV_Z // SSM_INNER)),
            pl.BlockSpec((L, SSM_INNER), lambda i: (i, EV_XS // SSM_INNER)),
            pl.BlockSpec((L, bcw), lambda i: (i, EV_BC // bcw)),
            pl.BlockSpec((L, LANES), lambda i: (i, EV_DT // LANES)),
            pl.BlockSpec(cwx.shape, fix), pl.BlockSpec(cbx.shape, fix),
            pl.BlockSpec(cwb.shape, fix), pl.BlockSpec(cbb.shape, fix),
            pl.BlockSpec((1, LANES), fix), pl.BlockSpec((1, LANES), fix),
            pl.BlockSpec((1, SSM_INNER), fix), pl.BlockSpec((1, SSM_INNER), fix),
            pl.BlockSpec(e.shape, fix),
        ],
        out_specs=pl.BlockSpec((L, SSM_INNER), lambda i: (i, 0)),
        out_shape=jax.ShapeDtypeStruct((s, SSM_INNER), BF16),
        scratch_shapes=[
            pltpu.VMEM((L + 2 * SUBLANES, SSM_INNER), F32),
            pltpu.VMEM((L + 2 * SUBLANES, bcw), F32),
            pltpu.VMEM((SSM_GROUPS, SSM_STATE, SSM_INNER // SSM_GROUPS), F32),
        ],
        compiler_params=_cparams(("arbitrary",)),
        name="ssd_scan",
    )(xp, xp, xp, xp, cwx, cbx, cwb, cbb, dtb, alog, dsk, nrm, e)


def _swa_kernel(sink_ref, q_ref, kvc_ref, kvp_ref, cq_ref, sq_ref, cp_ref, sp_ref, o_ref):
    i = pl.program_id(0)
    B = ATTN_BLOCK
    lane = lax.broadcasted_iota(I32, (B, LANES), 1)
    lo_half = lane < ROPE_DIM
    cq, sq = cq_ref[...], sq_ref[...]
    kc = _rope_tile(kvc_ref[:, 0:LANES], cq, sq)
    kp = _rope_tile(kvp_ref[:, 0:LANES], cp_ref[...], sp_ref[...])
    kcat = jnp.concatenate([kp, kc], axis=0)
    kmat = (kcat.astype(BF16), pltpu.roll(kcat, ROPE_DIM, 1).astype(BF16))
    vcat = jnp.concatenate([kvp_ref[:, LANES:2 * LANES], kvc_ref[:, LANES:2 * LANES]], axis=0)
    vrol = pltpu.roll(vcat, ROPE_DIM, 1)
    lane2 = lax.broadcasted_iota(I32, (2 * B, LANES), 1) < ROPE_DIM
    vdup = (jnp.where(lane2, vcat, vrol).astype(BF16), jnp.where(lane2, vrol, vcat).astype(BF16))
    r = lax.broadcasted_iota(I32, (B, 2 * B), 0)
    col = lax.broadcasted_iota(I32, (B, 2 * B), 1)
    mask = (col > r) & (col <= r + B) & ((i > 0) | (col >= B))
    scale = ROPE_DIM ** -0.5
    hpg = SWA_Q_HEADS // SWA_KV_HEADS
    for j in range(SWA_Q_HEADS // 2):
        g = (2 * j) // hpg
        qt = _rope_tile(q_ref[:, j * LANES:(j + 1) * LANES], cq, sq)
        outs = []
        for par in (0, 1):
            h = 2 * j + par
            qm = jnp.where(lo_half if par == 0 else jnp.logical_not(lo_half), qt, 0.0).astype(BF16)
            logit = _dot_nt(qm, kmat[0] if par == g else kmat[1]) * scale
            logit = jnp.where(mask, logit, -jnp.inf)
            sink = sink_ref[h]
            m = jnp.maximum(jnp.max(logit, axis=-1, keepdims=True), sink)
            ex = jnp.exp(logit - m)
            prob = ex / (jnp.sum(ex, axis=-1, keepdims=True) + jnp.exp(sink - m))
            outs.append(_dot(prob.astype(BF16), vdup[g]))
        o_ref[:, j * LANES:(j + 1) * LANES] = jnp.where(lo_half, outs[0], outs[1]).astype(BF16)


def _swa(xp, sinks, cos_t, sin_t):
    s = xp.shape[0]
    B = ATTN_BLOCK
    qw = SWA_Q_HEADS * ROPE_DIM
    kvw = 2 * SWA_KV_HEADS * ROPE_DIM
    prev = lambda i: (jnp.maximum(i - 1, 0), 0)
    cur = lambda i: (i, 0)
    return pl.pallas_call(
        _swa_kernel,
        grid=(s // B,),
        in_specs=[
            pl.BlockSpec(memory_space=pltpu.SMEM),
            pl.BlockSpec((B, qw), lambda i: (i, EV_Q // qw)),
            pl.BlockSpec((B, kvw), lambda i: (i, EV_KV // kvw)),
            pl.BlockSpec((B, kvw), lambda i: (jnp.maximum(i - 1, 0), EV_KV // kvw)),
            pl.BlockSpec((B, LANES), cur), pl.BlockSpec((B, LANES), cur),
            pl.BlockSpec((B, LANES), prev), pl.BlockSpec((B, LANES), prev),
        ],
        out_specs=pl.BlockSpec((B, qw), cur),
        out_shape=jax.ShapeDtypeStruct((s, qw), BF16),
        compiler_params=_cparams(("parallel",)),
        name="swa_sink",
    )(sinks, xp, xp, xp, cos_t, sin_t, cos_t, sin_t)


def _dsa_prep_kernel(qn_ref, qr_ref, qi_ref, ckv_ref, kk_ref, wi_ref, wuk_ref, kvn_ref, c_ref, s_ref,
                     qlat_ref, qrope_ref, qidx_ref, ckvn_ref, kr_ref, ki_ref, wis_ref):
    c, s = c_ref[...], s_ref[...]
    lane = lax.broadcasted_iota(I32, c.shape, 1)
    lo_half = lane < ROPE_DIM
    for h in range(MLA_HEADS):
        qn = qn_ref[:, h * MLA_NOPE:(h + 1) * MLA_NOPE].astype(BF16)
        qlat_ref[h] = _dot(qn, wuk_ref[h]).astype(BF16)
    for src, dst in ((qr_ref, qrope_ref), (qi_ref, qidx_ref)):
        for j in range(MLA_HEADS // 2):
            t = _rope_tile(src[:, j * LANES:(j + 1) * LANES], c, s)
            dst[2 * j] = jnp.where(lo_half, t, 0.0).astype(BF16)
            dst[2 * j + 1] = jnp.where(lo_half, pltpu.roll(t, ROPE_DIM, 1), 0.0).astype(BF16)
    kk = _rope_tile(kk_ref[...], c, s)
    kr_ref[...] = jnp.where(lo_half, kk, 0.0).astype(BF16)
    ki_ref[...] = jnp.where(lo_half, pltpu.roll(kk, ROPE_DIM, 1), 0.0).astype(BF16)
    ckv = ckv_ref[...]
    ms = jnp.mean(ckv * ckv, axis=-1, keepdims=True)
    ckvn_ref[...] = (ckv * lax.rsqrt(ms + NORM_EPS) * kvn_ref[...]).astype(BF16)
    wis_ref[...] = wi_ref[...] * (IDX_HEADS ** -0.5 * IDX_DIM ** -0.5)


def _dsa_prep(xp, wuk, kvn, cos_t, sin_t):
    s = xp.shape[0]
    tm = min(256, s)
    H = MLA_HEADS
    fix2 = lambda i: (0, 0)
    hrow = lambda i: (0, i, 0)
    row = lambda i: (i, 0)
    return pl.pallas_call(
        _dsa_prep_kernel,
        grid=(s // tm,),
        in_specs=[
            pl.BlockSpec((tm, 2048), lambda i: (i, OD_QN // 2048)),
            pl.BlockSpec((tm, 1024), lambda i: (i, OD_QR // 1024)),
            pl.BlockSpec((tm, 1024), lambda i: (i, OD_QI // 1024)),
            pl.BlockSpec((tm, MLA_RANK), lambda i: (i, OD_CKV // MLA_RANK)),
            pl.BlockSpec((tm, LANES), lambda i: (i, OD_KK // LANES)),
            pl.BlockSpec((tm, LANES), lambda i: (i, OD_WI // LANES)),
            pl.BlockSpec(wuk.shape, lambda i: (0, 0, 0)),
            pl.BlockSpec((1, MLA_RANK), fix2),
            pl.BlockSpec((tm, LANES), row), pl.BlockSpec((tm, LANES), row),
        ],
        out_specs=[
            pl.BlockSpec((H, tm, MLA_RANK), hrow),
            pl.BlockSpec((H, tm, LANES), hrow),
            pl.BlockSpec((H, tm, LANES), hrow),
            pl.BlockSpec((tm, MLA_RANK), row),
            pl.BlockSpec((tm, LANES), row),
            pl.BlockSpec((tm, LANES), row),
            pl.BlockSpec((tm, LANES), row),
        ],
        out_shape=[
            jax.ShapeDtypeStruct((H, s, MLA_RANK), BF16),
            jax.ShapeDtypeStruct((H, s, LANES), BF16),
            jax.ShapeDtypeStruct((H, s, LANES), BF16),
            jax.ShapeDtypeStruct((s, MLA_RANK), BF16),
            jax.ShapeDtypeStruct((s, LANES), BF16),
            jax.ShapeDtypeStruct((s, LANES), BF16),
            jax.ShapeDtypeStruct((s, LANES), F32),
        ],
        compiler_params=_cparams(("parallel",)),
        name="dsa_prep",
    )(xp, xp, xp, xp, xp, xp, wuk, kvn, cos_t, sin_t)


def _dsa_select_kernel(topk, qi_ref, wi_ref, ki_ref, bias_ref, key_ref, dig_ref):
    i = pl.program_id(0)
    Q, KC = DSA_SEL_Q, DSA_KC
    H = IDX_HEADS
    n_chunks = bias_ref.shape[0]
    n_vis = ((i + 1) * Q + KC - 1) // KC
    wi = wi_ref[...]
    qpos = i * Q + lax.broadcasted_iota(I32, (KC, Q), 1)
    kloc = lax.broadcasted_iota(I32, (KC, Q), 0)
    HG = DSA_SEL_HEADS_PER_DOT
    N_ACC = 4
    DIGIT_BITS = (11, 11, 10)
    DIGIT_SHIFT = (21, 10, 0)
    TOP_BIAS = 1 << (DIGIT_BITS[0] - 1)
    GUARD = jnp.int32(-0x7FFF8000)
    FIELD_ONES = jnp.int32(0x00010001)
    HK = KC // 2

    def pack(lo, hi):
        return lo | (hi << 16) | GUARD

    def score_chunk(c, carry):
        k = ki_ref[pl.ds(pl.multiple_of(c * KC, KC), KC), :]
        acc = jnp.zeros((KC, Q), F32)
        for g in range(H // HG):
            sc = _dot_nt(k, qi_ref[g * HG:(g + 1) * HG].reshape(HG * Q, LANES))
            for hh in range(HG):
                h = g * HG + hh
                acc = acc + jnp.maximum(sc[:, hh * Q:(hh + 1) * Q], 0.0) * wi[h:h + 1, :]
        acc = jnp.where(c * KC + kloc <= qpos, acc, -jnp.inf)
        bits = pltpu.bitcast(acc, I32)
        key = bits ^ ((bits >> 31) & jnp.int32(0x7FFFFFFF))
        key_ref[c] = key
        for d in range(len(DIGIT_BITS)):
            dig = (key >> DIGIT_SHIFT[d]) + TOP_BIAS if d == 0 else (key >> DIGIT_SHIFT[d]) & ((1 << DIGIT_BITS[d]) - 1)
            dig_ref[d, c] = pack(dig[:HK] + 1, dig[HK:] + 1)
        return carry

    lax.fori_loop(0, n_vis, score_chunk, 0)

    def count_ge(d, cand):
        cand2 = (cand + 1) | ((cand + 1) << 16)

        def body(c, accs):
            v = dig_ref[d, c]
            accs = list(accs)
            for r in range(HK // SUBLANES):
                w = v[r * SUBLANES:(r + 1) * SUBLANES, :] - cand2
                accs[r % N_ACC] = accs[r % N_ACC] + (lax.shift_right_logical(w, 15) & FIELD_ONES)
            return tuple(accs)
        accs = lax.fori_loop(0, n_vis, body, tuple(jnp.zeros((SUBLANES, Q), I32) for _ in range(N_ACC)))
        acc = sum(accs)
        cnt = (acc & 0xFFFF) + lax.shift_right_logical(acc, 16)
        return jnp.sum(cnt.astype(F32), axis=0, keepdims=True)

    def search(d, need):
        def bit_step(b, thr):
            cand = thr | (jnp.int32(1) << (DIGIT_BITS[d] - 1 - b))
            return jnp.where(count_ge(d, cand) >= need, cand, thr)

        return lax.fori_loop(0, DIGIT_BITS[d], bit_step, jnp.zeros((1, Q), I32))

    def drop_unless_equal(d, p):
        def body(c, carry):
            cur, nxt = dig_ref[d, c], dig_ref[d + 1, c]
            lo = jnp.where((cur & 0x7FFF) == p + 1, nxt & 0x7FFF, 0)
            hi = jnp.where((lax.shift_right_logical(cur, 16) & 0x7FFF) == p + 1, lax.shift_right_logical(nxt, 16) & 0x7FFF, 0)
            dig_ref[d + 1, c] = pack(lo, hi)
            return carry

        lax.fori_loop(0, n_vis, body, 0)

    need = jnp.full((1, Q), float(topk), F32)
    thr = jnp.zeros((1, Q), I32)
    for d in range(len(DIGIT_BITS)):
        p = search(d, need)
        thr = thr + ((p - (TOP_BIAS if d == 0 else 0)) << DIGIT_SHIFT[d])
        if d + 1 < len(DIGIT_BITS):
            need = need - count_ge(d, p + 1)
            drop_unless_equal(d, p)

    def emit(c, cnt):
        sel = (key_ref[c] >= thr) & (c * KC + kloc <= qpos)
        bias_ref[c] = jnp.where(sel, 0.0, MASK_NEG).T
        return cnt + jnp.sum(jnp.where(sel, 1.0, 0.0), axis=0, keepdims=True)

    n_sel = lax.fori_loop(0, n_vis, emit, jnp.zeros((1, Q), F32))

    @pl.when(jnp.max(n_sel) > topk)
    def _():
        def count(pred):
            def body(c, cnt):
                return cnt + jnp.sum(jnp.where(pred(c), 1.0, 0.0), axis=0, keepdims=True)
            return lax.fori_loop(0, n_vis, body, jnp.zeros((1, Q), F32))

        def tied(c):
            return (key_ref[c] == thr) & (c * KC + kloc <= qpos)

        need_tied = topk - count(lambda c: key_ref[c] > thr)

        def bit_step(b, last):
            cand = last | (jnp.int32(1) << (index_bits - 1 - b))
            in_front = count(lambda c: tied(c) & (c * KC + kloc < cand))
            return jnp.where(in_front < need_tied, cand, last)

        index_bits = (n_chunks * KC - 1).bit_length()
        last = lax.fori_loop(0, index_bits, bit_step, jnp.zeros((1, Q), I32))

        def emit_ties(c, carry):
            sel = (key_ref[c] > thr) | (tied(c) & (c * KC + kloc <= last))
            bias_ref[c] = jnp.where(sel, 0.0, MASK_NEG).T
            return carry

        lax.fori_loop(0, n_vis, emit_ties, 0)

    def fill(c, carry):
        bias_ref[c] = jnp.full((Q, KC), MASK_NEG, F32)
        return carry

    lax.fori_loop(n_vis, n_chunks, fill, 0)


def _dsa_select(qidx, wis, ki, topk):
    H, s, _ = qidx.shape
    Q, KC = DSA_SEL_Q, DSA_KC
    assert Q == LANES
    nch = s // KC
    return pl.pallas_call(
        functools.partial(_dsa_select_kernel, topk),
        grid=(s // Q,),
        in_specs=[
            pl.BlockSpec((H, Q, LANES), lambda i: (0, i, 0)),
            pl.BlockSpec((LANES, Q), lambda i: (0, i)),
            pl.BlockSpec((s, LANES), lambda i: (0, 0)),
        ],
        out_specs=pl.BlockSpec((nch, Q, KC), lambda i: (0, i, 0)),
        out_shape=jax.ShapeDtypeStruct((nch, s, KC), F32),
        scratch_shapes=[pltpu.VMEM((nch, KC, Q), I32), pltpu.VMEM((3, nch, KC // 2, Q), I32)],
        compiler_params=_cparams(("parallel",)),
        name="dsa_select",
    )(qidx, wis.T, ki)


def _dsa_attn_kernel(ql_ref, qr_ref, bias_ref, ckv_ref, kr_ref, o_ref, m_ref, l_ref, acc_ref, s_ref):
    i = pl.program_id(0)
    Q, KC, H = DSA_ATT_Q, DSA_KC, MLA_HEADS
    n_vis = ((i + 1) * Q + KC - 1) // KC
    m_ref[...] = jnp.full(m_ref.shape, -jnp.inf, F32)
    l_ref[...] = jnp.zeros(l_ref.shape, F32)
    acc_ref[...] = jnp.zeros(acc_ref.shape, F32)

    def keys(c):
        off = pl.multiple_of(c * KC, KC)
        return ckv_ref[pl.ds(off, KC), :], kr_ref[pl.ds(off, KC), :]

    def raw_logits(c):
        ck, kr = keys(c)
        ql = ql_ref[...].reshape(H * Q, MLA_RANK)
        qr = qr_ref[...].reshape(H * Q, LANES)
        return _dot_nt(ql, ck) + _dot_nt(qr, kr)

    def consume(c, slot):
        logit = s_ref[slot] * MLA_SCALE
        logit = (logit.reshape(H, Q, KC) + bias_ref[c][None]).reshape(H * Q, KC)
        m_old = m_ref[...]
        m_new = jnp.maximum(m_old, jnp.max(logit, axis=-1, keepdims=True))
        alpha = jnp.exp(m_old - m_new)
        p = jnp.exp(logit - m_new)
        l_ref[...] = alpha * l_ref[...] + jnp.sum(p, axis=-1, keepdims=True)
        acc_ref[...] = alpha * acc_ref[...] + _dot(p.astype(BF16), keys(c)[0])
        m_ref[...] = m_new

    s_ref[0] = raw_logits(0)

    def pair(j, carry):
        c = 2 * j
        s_ref[1] = raw_logits(c + 1)
        consume(c, 0)
        s_ref[0] = raw_logits(c + 2)
        consume(c + 1, 1)
        return carry

    n_pairs = (n_vis - 1) // 2
    lax.fori_loop(0, n_pairs, pair, 0)
    last = 2 * n_pairs

    @pl.when(last + 1 < n_vis)
    def _():
        s_ref[1] = raw_logits(last + 1)
        consume(last, 0)
        consume(last + 1, 1)

    @pl.when(last + 1 == n_vis)
    def _():
        consume(last, 0)

    o_ref[...] = (acc_ref[...] / l_ref[...]).reshape(H, Q, MLA_RANK).astype(BF16)


def _dsa_attn(qlat, qrope, bias, ckvn, kr):
    H, s, _ = qlat.shape
    Q, KC = DSA_ATT_Q, DSA_KC
    nch = s // KC
    hrow = lambda i: (0, i, 0)
    fix = lambda i: (0, 0)
    return pl.pallas_call(
        _dsa_attn_kernel,
        grid=(s // Q,),
        in_specs=[
            pl.BlockSpec((H, Q, MLA_RANK), hrow),
            pl.BlockSpec((H, Q, LANES), hrow),
            pl.BlockSpec((nch, Q, KC), hrow),
            pl.BlockSpec((s, MLA_RANK), fix),
            pl.BlockSpec((s, LANES), fix),
        ],
        out_specs=pl.BlockSpec((H, Q, MLA_RANK), hrow),
        out_shape=jax.ShapeDtypeStruct((H, s, MLA_RANK), BF16),
        scratch_shapes=[
            pltpu.VMEM((H * Q, 1), F32),
            pltpu.VMEM((H * Q, 1), F32),
            pltpu.VMEM((H * Q, MLA_RANK), F32),
            pltpu.VMEM((2, H * Q, KC), F32),
        ],
        compiler_params=_cparams(("parallel",)),
        name="dsa_attn",
    )(qlat, qrope, bias, ckvn, kr)


def _odd_out_kernel(ol_ref, wuv_ref, wo_ref, x_ref, g_ref, b_ref, o_ref, of_ref, u_ref):
    for h in range(MLA_HEADS):
        u_ref[:, h * MLA_V:(h + 1) * MLA_V] = _dot(ol_ref[h], wuv_ref[h]).astype(BF16)
    mix = _dot(u_ref[...], wo_ref[...])
    _store_row_layouts(_layer_norm(DEEPNORM_ALPHA * x_ref[...] + mix, g_ref[...], b_ref[...]), o_ref, of_ref)


def _odd_out(olat, wuv, wo, x, g, b):
    H, s, _ = olat.shape
    tm = min(256, s)
    row = lambda i: (i, 0)
    fix = lambda i: (0, 0)
    out_specs, out_shape = _row_layout_outputs(s, tm)
    return pl.pallas_call(
        _odd_out_kernel,
        grid=(s // tm,),
        in_specs=[
            pl.BlockSpec((H, tm, MLA_RANK), lambda i: (0, i, 0)),
            pl.BlockSpec(wuv.shape, lambda i: (0, 0, 0)),
            pl.BlockSpec(wo.shape, fix),
            pl.BlockSpec((tm, D_MODEL), row),
            pl.BlockSpec((1, D_MODEL), fix), pl.BlockSpec((1, D_MODEL), fix),
        ],
        out_specs=out_specs,
        out_shape=out_shape,
        scratch_shapes=[pltpu.VMEM((tm, MLA_HEADS * MLA_V), BF16)],
        compiler_params=_cparams(("parallel",)),
        name="odd_out_ln",
    )(olat, wuv, wo, x, g, b)


ROUTE_LANE0 = MOE_GROUPS


def _router_kernel(h_ref, wr_ref, br_ref, info_ref, cnt_ref, run_ref):
    i = pl.program_id(0)

    @pl.when(i == 0)
    def _():
        run_ref[...] = jnp.zeros(run_ref.shape, F32)

    h = h_ref[...]
    tm = h.shape[0]
    h_hi = h.astype(BF16)
    h_lo = (h - h_hi.astype(F32)).astype(BF16)
    w = wr_ref[...]
    w_hi = w.astype(BF16)
    w_lo = (w - w_hi.astype(F32)).astype(BF16)
    logits = _dot(h_hi, w_hi) + (_dot(h_hi, w_lo) + _dot(h_lo, w_hi)) + br_ref[...]
    lane = lax.broadcasted_iota(I32, (tm, LANES), 1)
    lane_f = lane.astype(F32)
    neg = -jnp.inf
    big = float(LANES)
    is_grp = lane < MOE_GROUPS
    gl = jnp.where(is_grp, logits, neg)
    gmax = jnp.max(gl, axis=-1, keepdims=True)
    gsel = jnp.min(jnp.where(gl == gmax, lane_f, big), axis=-1, keepdims=True)
    gsum = jnp.sum(jnp.where(is_grp, jnp.exp(logits - gmax), 0.0), axis=-1, keepdims=True)
    egrp = ((lane - ROUTE_LANE0) >> 3).astype(F32)
    valid = (lane >= ROUTE_LANE0) & (lane < ROUTE_LANE0 + MOE_EXPERTS) & (egrp == gsel)
    el = jnp.where(valid, logits, neg)
    v1 = jnp.max(el, axis=-1, keepdims=True)
    i1 = jnp.min(jnp.where(el == v1, lane_f, big), axis=-1, keepdims=True)
    el2 = jnp.where(lane_f == i1, neg, el)
    v2 = jnp.max(el2, axis=-1, keepdims=True)
    i2 = jnp.min(jnp.where(el2 == v2, lane_f, big), axis=-1, keepdims=True)
    t = jnp.exp(v2 - v1)
    p1 = 1.0 / (1.0 + t)
    p2 = t / (1.0 + t)
    ggate = 1.0 / gsum
    m1 = lane_f == i1
    m2 = lane_f == i2
    memb = jnp.where(m1 | m2, 1.0, 0.0)
    tri = (lax.broadcasted_iota(I32, (tm, tm), 0) > lax.broadcasted_iota(I32, (tm, tm), 1))
    cum = _dot(jnp.where(tri, 1.0, 0.0).astype(BF16), memb.astype(BF16)) + run_ref[...]
    rank1 = jnp.sum(jnp.where(m1, cum, 0.0), axis=-1, keepdims=True)
    rank2 = jnp.sum(jnp.where(m2, cum, 0.0), axis=-1, keepdims=True)
    run_ref[...] = run_ref[...] + jnp.sum(memb, axis=0, keepdims=True)
    info = jnp.where(lane == 0, i1 - ROUTE_LANE0, 0.0)
    info = jnp.where(lane == 1, i2 - ROUTE_LANE0, info)
    info = jnp.where(lane == 2, p1 * ggate, info)
    info = jnp.where(lane == 3, p2 * ggate, info)
    info = jnp.where(lane == 4, rank1, info)
    info = jnp.where(lane == 5, rank2, info)
    info_ref[...] = info
    cnt_ref[...] = run_ref[...]


def _router(h, wr, br):
    t = h.shape[0]
    tm = min(512, t)
    return pl.pallas_call(
        _router_kernel,
        grid=(t // tm,),
        in_specs=[
            pl.BlockSpec((tm, D_MODEL), lambda i: (i, 0)),
            pl.BlockSpec(wr.shape, lambda i: (0, 0)),
            pl.BlockSpec((1, LANES), lambda i: (0, 0)),
        ],
        out_specs=[pl.BlockSpec((tm, LANES), lambda i: (i, 0)), pl.BlockSpec((1, LANES), lambda i: (0, 0))],
        out_shape=[jax.ShapeDtypeStruct((t, LANES), F32), jax.ShapeDtypeStruct((1, LANES), F32)],
        scratch_shapes=[pltpu.VMEM((1, LANES), F32)],
        compiler_params=_cparams(("arbitrary",)),
        name="moe_router",
    )(h, wr, br)


def _plan_kernel(info_ref, ps_ref, pos_ref):
    info = info_ref[...]
    lane = lax.broadcasted_iota(I32, info.shape, 1)
    lane_f = lane.astype(F32)
    ps = ps_ref[...]
    pos1 = jnp.sum(jnp.where(lane_f == info[:, 0:1], ps, 0.0), axis=-1, keepdims=True) + info[:, 4:5]
    pos2 = jnp.sum(jnp.where(lane_f == info[:, 1:2], ps, 0.0), axis=-1, keepdims=True) + info[:, 5:6]
    pos_ref[...] = jnp.where(lane == 0, pos1, jnp.where(lane == 1, pos2, 0.0)).astype(I32)


def _plan(info, pad_start):
    t = info.shape[0]
    tm = min(1024, t)
    return pl.pallas_call(
        _plan_kernel,
        grid=(t // tm,),
        in_specs=[pl.BlockSpec((tm, LANES), lambda i: (i, 0)), pl.BlockSpec((1, LANES), lambda i: (0, 0))],
        out_specs=pl.BlockSpec((tm, LANES), lambda i: (i, 0)),
        out_shape=jax.ShapeDtypeStruct((t, LANES), I32),
        compiler_params=_cparams(("parallel",)),
        name="moe_plan",
    )(info, pad_start)


def _invert_kernel(pos_ref, rt_ref):
    def clear(r, carry):
        rt_ref[r] = 0
        return carry

    lax.fori_loop(0, rt_ref.shape[0], clear, 0, unroll=8)

    def put(n, carry):
        rt_ref[pos_ref[n]] = lax.shift_right_logical(n, 1)
        return carry

    lax.fori_loop(0, pos_ref.shape[0], put, 0, unroll=8)


def _invert(pos_flat, n_rows):
    return pl.pallas_call(
        _invert_kernel,
        in_specs=[pl.BlockSpec(memory_space=pltpu.SMEM)],
        out_specs=pl.BlockSpec(memory_space=pltpu.SMEM),
        out_shape=jax.ShapeDtypeStruct((n_rows,), I32),
        name="moe_invert",
    )(pos_flat)


def _expert_kernel(be_ref, nu_ref, rt_ref, h_ref, wg_ref, wu_ref, wd_ref, o_ref, xbuf0, xbuf1, xb, wgb, wub, wdb, sem):
    b = pl.program_id(0)
    n_used = nu_ref[0]
    R = EXPERT_ROWS

    bufs = (xbuf0, xbuf1)

    def row_copy(tok, r, slot):
        src = h_ref.at[pl.ds(pl.multiple_of(tok * ROW_TILES, ROW_TILES), ROW_TILES)]
        return pltpu.make_async_copy(src, bufs[slot].at[pl.ds(r * ROW_TILES, ROW_TILES)], sem.at[slot])

    def drain(slot):
        def body(r, carry):
            row_copy(0, 0, slot).wait()
            return carry
        lax.fori_loop(0, R, body, 0, unroll=8)

    @pl.when(b == 0)
    def _():
        def body(r, carry):
            row_copy(rt_ref[r], r, 0).start()
            return carry
        lax.fori_loop(0, R, body, 0, unroll=8)

    def block(slot):
        drain(slot)

        @pl.when((b == 0) | (be_ref[b] != be_ref[jnp.maximum(b - 1, 0)]))
        def _():
            wgb[...] = wg_ref[0, 0].astype(BF16)
            wub[...] = wu_ref[0, 0].astype(BF16)
            wdb[...] = wd_ref[0, 0].astype(BF16)

        base = (b + 1) * R
        for r in range(R):
            row_copy(rt_ref[base + r], r, 1 - slot).start()
        for j in range(ROW_TILES):
            xb[:, j * LANES:(j + 1) * LANES] = bufs[slot][pl.ds(j, R, stride=ROW_TILES), :].astype(BF16)
        x = xb[...]
        hid = _silu(_dot(x, wgb[...])) * _dot(x, wub[...])
        y = _dot(hid.astype(BF16), wdb[...])
        for j in range(ROW_TILES):
            o_ref[pl.ds(j, R, stride=ROW_TILES), :] = y[:, j * LANES:(j + 1) * LANES]

    for parity in (0, 1):
        pl.when((b < n_used) & (b % 2 == parity))(functools.partial(block, parity))

    @pl.when(b >= n_used)
    def _():
        o_ref[...] = jnp.zeros(o_ref.shape, F32)

    for parity in (0, 1):
        pl.when((b == n_used) & (b % 2 == parity))(functools.partial(drain, parity))


def _experts(blk_e, n_used, row_token, h, wg, wu, wd, layer):
    r = row_token.shape[0]
    nb = r // EXPERT_ROWS
    wmap = lambda b, be, nu, rt: (layer, be[b], 0, 0)
    return pl.pallas_call(
        _expert_kernel,
        grid_spec=pltpu.PrefetchScalarGridSpec(
            num_scalar_prefetch=3,
            grid=(nb,),
            in_specs=[
                pl.BlockSpec(memory_space=pl.ANY),
                pl.BlockSpec((1, 1, D_MODEL, MOE_FF), wmap),
                pl.BlockSpec((1, 1, D_MODEL, MOE_FF), wmap),
                pl.BlockSpec((1, 1, MOE_FF, D_MODEL), wmap),
            ],
            out_specs=pl.BlockSpec((EXPERT_ROWS * ROW_TILES, LANES), lambda b, be, nu, rt: (b, 0)),
            scratch_shapes=[
                pltpu.VMEM((EXPERT_ROWS * ROW_TILES, LANES), F32),
                pltpu.VMEM((EXPERT_ROWS * ROW_TILES, LANES), F32),
                pltpu.VMEM((EXPERT_ROWS, D_MODEL), BF16),
                pltpu.VMEM((D_MODEL, MOE_FF), BF16),
                pltpu.VMEM((D_MODEL, MOE_FF), BF16),
                pltpu.VMEM((MOE_FF, D_MODEL), BF16),
                pltpu.SemaphoreType.DMA((2,)),
            ],
        ),
        out_shape=jax.ShapeDtypeStruct((r * ROW_TILES, LANES), F32),
        compiler_params=_cparams(("arbitrary",)),
        name="moe_experts",
    )(blk_e, n_used, row_token, h, wg, wu, wd)


COMBINE_TOKENS = 256


def _combine_ple_kernel(pos_ref, y_ref, info_ref, x_ref, g_ref, b_ref, p_ref, wg_ref, bg_ref, wp_ref, o_ref,
                        buf0, buf1, ycat, sem):
    i = pl.program_id(0)
    n = pl.num_programs(0)
    TB = x_ref.shape[0]
    bufs = (buf0, buf1)

    def row_copy(src_row, k, tt, slot):
        src = y_ref.at[pl.ds(pl.multiple_of(src_row * ROW_TILES, ROW_TILES), ROW_TILES)]
        return pltpu.make_async_copy(src, bufs[slot].at[k, pl.ds(tt * ROW_TILES, ROW_TILES)], sem.at[slot])

    def drain(slot):
        def body(tt, carry):
            row_copy(0, 0, 0, slot).wait()
            row_copy(0, 1, 0, slot).wait()
            return carry
        lax.fori_loop(0, TB, body, 0, unroll=8)

    @pl.when(i == 0)
    def _():
        def body(tt, carry):
            row_copy(pos_ref[2 * tt], 0, tt, 0).start()
            row_copy(pos_ref[2 * tt + 1], 1, tt, 0).start()
            return carry
        lax.fori_loop(0, TB, body, 0, unroll=8)

    def block(slot):
        drain(slot)
        base = jnp.minimum(i + 1, n - 1) * (2 * TB)
        for tt in range(TB):
            row_copy(pos_ref[base + 2 * tt], 0, tt, 1 - slot).start()
            row_copy(pos_ref[base + 2 * tt + 1], 1, tt, 1 - slot).start()
        info = info_ref[...]
        g0, g1 = info[:, 2:3], info[:, 3:4]
        for j in range(ROW_TILES):
            rows = pl.ds(j, TB, stride=ROW_TILES)
            ycat[:, j * LANES:(j + 1) * LANES] = bufs[slot][0, rows, :] * g0 + bufs[slot][1, rows, :] * g1
        x2 = _layer_norm(DEEPNORM_ALPHA * x_ref[...] + ycat[...], g_ref[...], b_ref[...])
        gate = jax.nn.sigmoid(_dot(x2.astype(BF16), wg_ref[...]) + bg_ref[...])
        o_ref[...] = x2 + gate * _dot(p_ref[...].astype(BF16), wp_ref[...])

    for parity in (0, 1):
        pl.when(i % 2 == parity)(functools.partial(block, parity))
    for parity in (0, 1):
        pl.when((i == n - 1) & (i % 2 == parity))(functools.partial(drain, 1 - parity))


def _combine_ple(pos_flat, y_rows, info, x, g, b, p, wg, bg, wp):
    t = x.shape[0]
    TB = min(COMBINE_TOKENS, t)
    row = lambda i, pos: (i, 0)
    fix = lambda i, pos: (0, 0)
    return pl.pallas_call(
        _combine_ple_kernel,
        grid_spec=pltpu.PrefetchScalarGridSpec(
            num_scalar_prefetch=1,
            grid=(t // TB,),
            in_specs=[
                pl.BlockSpec(memory_space=pl.ANY),
                pl.BlockSpec((TB, LANES), row),
                pl.BlockSpec((TB, D_MODEL), row),
                pl.BlockSpec((1, D_MODEL), fix), pl.BlockSpec((1, D_MODEL), fix),
                pl.BlockSpec((TB, PLE_DIM), row),
                pl.BlockSpec(wg.shape, fix),
                pl.BlockSpec((1, D_MODEL), fix),
                pl.BlockSpec(wp.shape, fix),
            ],
            out_specs=pl.BlockSpec((TB, D_MODEL), row),
            scratch_shapes=[
                pltpu.VMEM((2, TB * ROW_TILES, LANES), F32),
                pltpu.VMEM((2, TB * ROW_TILES, LANES), F32),
                pltpu.VMEM((TB, D_MODEL), F32),
                pltpu.SemaphoreType.DMA((2,)),
            ],
        ),
        out_shape=jax.ShapeDtypeStruct((t, D_MODEL), F32),
        compiler_params=_cparams(("arbitrary",)),
        name="moe_combine_ln_ple",
    )(pos_flat, y_rows, info, x, g, b, p, wg, bg, wp)


def _hier_moe_ln_ple(x, x_folded, wr, br, wg, wu, wd, layer, ln_g, ln_b, p, ple_wg, ple_bg, ple_wp):
    t = x.shape[0]
    info, cnt = _router(x, wr, br)
    counts = cnt[0, ROUTE_LANE0:ROUTE_LANE0 + MOE_EXPERTS].astype(I32)
    padded = (counts + EXPERT_ROWS - 1) // EXPERT_ROWS * EXPERT_ROWS
    pad_end = jnp.cumsum(padded)
    pad_start = jnp.zeros((1, LANES), F32).at[0, :MOE_EXPERTS].set((pad_end - padded).astype(F32))
    n_rows = 2 * t + MOE_EXPERTS * EXPERT_ROWS
    n_blocks = n_rows // EXPERT_ROWS
    blk_start = jnp.arange(n_blocks, dtype=I32) * EXPERT_ROWS
    blk_e = jnp.minimum(jnp.sum(pad_end[None, :] <= blk_start[:, None], axis=1), MOE_EXPERTS - 1).astype(I32)
    n_used = (pad_end[-1:] // EXPERT_ROWS).astype(I32)
    pos = _plan(info, pad_start)[:, :2].reshape(-1)
    y_rows = _experts(blk_e, n_used, _invert(pos, n_rows), x_folded, wg, wu, wd, layer)
    return _combine_ple(pos, y_rows, info, x, ln_g, ln_b, p, ple_wg, ple_bg, ple_wp)


def _rope_tables(positions):
    inv = ROPE_THETA ** (-jnp.arange(0, ROPE_DIM, 2, dtype=F32) / ROPE_DIM)
    ang = positions.astype(F32)[:, None] * inv
    cos, sin = jnp.cos(ang), jnp.sin(ang)
    cos_t = jnp.tile(cos, (1, LANES // (ROPE_DIM // 2)))
    sin_t = jnp.tile(jnp.concatenate([-sin, sin], axis=1), (1, LANES // ROPE_DIM))
    return cos_t, sin_t


def _pad_cols(w, n):
    return jnp.pad(w, ((0, 0), (0, n - w.shape[1])))


def _cat_bf16(parts, n):
    parts = [p.astype(BF16) for p in parts]
    used = sum(p.shape[1] for p in parts)
    return jnp.concatenate(parts + [jnp.zeros((parts[0].shape[0], n - used), BF16)], axis=1)


def _even_w_in(w):
    z, xbc, dt, q, kv = jnp.split(w, [2048, 5120, 5152, 6176], axis=1)
    return _cat_bf16([z, xbc, q, kv, dt], EV_NP)


def _odd_w_in(w):
    q, ckv, krope, qi, ki, wi = jnp.split(w, [3072, 3584, 3648, 4672, 4736], axis=1)
    q = q.reshape(D_MODEL, MLA_HEADS, MLA_NOPE + MLA_ROPE)
    qn = q[:, :, :MLA_NOPE].reshape(D_MODEL, -1)
    qr = q[:, :, MLA_NOPE:].reshape(D_MODEL, -1)
    return _cat_bf16([qn, qr, qi, ckv, krope, ki, wi], OD_NP)


def _head_expand_matrix():
    e = np.zeros((LANES, SSM_INNER), np.float32)
    for h in range(SSM_HEADS):
        e[h, h * SSM_HEAD_DIM:(h + 1) * SSM_HEAD_DIM] = 1.0
    return jnp.asarray(e, BF16)


def kernel(x, p, positions, ev_w_in, ev_conv_w, ev_conv_b, ev_dt_bias, ev_a_log, ev_d_skip, ev_ssm_norm, ev_sinks, ev_w_out, od_w_in, od_kv_norm, od_w_uk, od_w_uv, od_w_out, ln1_g, ln1_b, ln2_g, ln2_b, moe_router_group, moe_router_group_b, moe_router_expert, moe_router_expert_b, moe_w_gate, moe_w_up, moe_w_down, ple_w_proj, ple_w_gate, ple_b_gate):
    batch, s, d = x.shape
    assert batch == 1 and d == D_MODEL
    xs = x[0]
    cos_t, sin_t = _rope_tables(positions[0])
    e_mat = _head_expand_matrix()
    topk = min(IDX_TOPK_MAX, s // 4)
    row = lambda v: v.reshape(1, -1)
    pad_row = lambda v: _pad_cols(v.reshape(1, -1), LANES)
    for i in range(DEPTH):
        j = i // 2
        if i % 2 == 0:
            xp = _inproj(xs, _even_w_in(ev_w_in[j]))
            y_ssm = _ssd(xp, ev_conv_w[j], row(ev_conv_b[j]), pad_row(ev_dt_bias[j]), pad_row(ev_a_log[j]),
                         row(jnp.repeat(ev_d_skip[j], SSM_HEAD_DIM)), row(ev_ssm_norm[j]), e_mat)
            y_att = _swa(xp, ev_sinks[j], cos_t, sin_t)
            w_out = ev_w_out[j].astype(BF16)
            xs, xf = _even_out(y_ssm, y_att, w_out[:SSM_INNER], w_out[SSM_INNER:], xs, row(ln1_g[i]), row(ln1_b[i]))
        else:
            xp = _inproj(xs, _odd_w_in(od_w_in[j]))
            qlat, qrope, qidx, ckvn, kr, ki, wis = _dsa_prep(
                xp, od_w_uk[j].astype(BF16), row(od_kv_norm[j]), cos_t, sin_t)
            bias = _dsa_select(qidx, wis, ki, topk)
            olat = _dsa_attn(qlat, qrope, bias, ckvn, kr)
            xs, xf = _odd_out(olat, od_w_uv[j].astype(BF16), od_w_out[j].astype(BF16), xs, row(ln1_g[i]), row(ln1_b[i]))
        wr = _pad_cols(jnp.concatenate([moe_router_group[i], moe_router_expert[i]], axis=1), LANES)
        br = pad_row(jnp.concatenate([moe_router_group_b[i], moe_router_expert_b[i]]))
        xs = _hier_moe_ln_ple(xs, xf, wr, br, moe_w_gate, moe_w_up, moe_w_down, i, row(ln2_g[i]), row(ln2_b[i]),
                              p[i, 0], ple_w_gate[i].astype(BF16), row(ple_b_gate[i]), ple_w_proj[i].astype(BF16))
    return xs[None]
```

```python
import functools

import jax
import jax.numpy as jnp
import numpy as np
from jax import lax
from jax.experimental import pallas as pl
from jax.experimental.pallas import tpu as pltpu

F32 = jnp.float32
BF16 = jnp.bfloat16
I32 = jnp.int32

D_MODEL = 2048
DEPTH = 4
ROPE_THETA = 10000.0
ROPE_DIM = 64
NORM_EPS = 1e-5
SSM_HEADS = 32
SSM_HEAD_DIM = 64
SSM_INNER = SSM_HEADS * SSM_HEAD_DIM
SSM_GROUPS = 4
SSM_STATE = 128
SSM_CONV = 4
SSM_CHUNK = 128
SWA_Q_HEADS = 16
SWA_KV_HEADS = 2
ATTN_BLOCK = 128
MLA_HEADS = 16
MLA_NOPE = 128
MLA_ROPE = ROPE_DIM
MLA_V = 128
MLA_RANK = 512
MLA_SCALE = (MLA_NOPE + MLA_ROPE) ** -0.5
IDX_HEADS = 16
IDX_DIM = ROPE_DIM
IDX_TOPK_MAX = 256
MOE_GROUPS = 4
MOE_EPG = 8
MOE_EXPERTS = MOE_GROUPS * MOE_EPG
MOE_FF = 512
PLE_DIM = 256
DEEPNORM_ALPHA = (2 * DEPTH) ** 0.25

LANES = 128
SUBLANES = 8
V7X_VMEM_BYTES = 64 * 1024 * 1024
COMPILER_RESERVE_BYTES = 8 * 1024 * 1024
VMEM_LIMIT_BYTES = V7X_VMEM_BYTES - COMPILER_RESERVE_BYTES
ROW_TILES = D_MODEL // LANES

EXPERT_ROWS = 256
DSA_SEL_Q = 128
DSA_ATT_Q = 64
DSA_KC = 512
DSA_SEL_HEADS_PER_DOT = 4
MASK_NEG = -1e30

EV_Z, EV_XS, EV_BC, EV_Q, EV_KV, EV_DT = 0, 2048, 4096, 5120, 6144, 6400
EV_NP = 6656
OD_QN, OD_QR, OD_QI, OD_CKV, OD_KK, OD_WI = 0, 2048, 3072, 4096, 4608, 4736
OD_NP = 5120


def _cparams(sem, vmem=VMEM_LIMIT_BYTES):
    return pltpu.CompilerParams(dimension_semantics=sem, vmem_limit_bytes=vmem)


def _dot(a, b):
    return jnp.dot(a, b, preferred_element_type=F32)


def _dot_nt(a, b):
    return lax.dot_general(a, b, (((1,), (1,)), ((), ())), preferred_element_type=F32)


def _split3(v):
    hi = v.astype(BF16)
    r = v - hi.astype(F32)
    mid = r.astype(BF16)
    lo = (r - mid.astype(F32)).astype(BF16)
    return hi, mid, lo


def _expand(v, e3):
    return _dot(jnp.concatenate(_split3(v), axis=1), e3)


def _silu(v):
    return v * jax.nn.sigmoid(v)


def _layer_norm(v, g, b):
    mu = jnp.mean(v, axis=-1, keepdims=True)
    vc = v - mu
    var = jnp.mean(vc * vc, axis=-1, keepdims=True)
    return vc * lax.rsqrt(var + NORM_EPS) * g + b


def _rope_tile(t, c, s):
    lane = lax.broadcasted_iota(I32, t.shape, 1)
    first_half = (lane & 32) == 0
    swapped = jnp.where(first_half, pltpu.roll(t, LANES - 32, 1), pltpu.roll(t, 32, 1))
    return t * c + swapped * s


def _inproj_kernel(x_ref, w_ref, o_ref):
    o_ref[...] = _dot(x_ref[...].astype(BF16), w_ref[...])


def _inproj(x, w):
    m, k = x.shape
    n = w.shape[1]
    tm, tn = min(1024, m), 512
    return pl.pallas_call(
        _inproj_kernel,
        grid=(m // tm, n // tn),
        in_specs=[pl.BlockSpec((tm, k), lambda i, j: (i, 0)), pl.BlockSpec((k, tn), lambda i, j: (0, j))],
        out_specs=pl.BlockSpec((tm, tn), lambda i, j: (i, j)),
        out_shape=jax.ShapeDtypeStruct((m, n), F32),
        compiler_params=_cparams(("parallel", "arbitrary")),
        name="inproj",
    )(x, w)


def _store_row_layouts(y, o_ref, of_ref):
    o_ref[...] = y
    for j in range(ROW_TILES):
        of_ref[pl.ds(j, y.shape[0], stride=ROW_TILES), :] = y[:, j * LANES:(j + 1) * LANES]


def _row_layout_outputs(m, tm):
    specs = [pl.BlockSpec((tm, D_MODEL), lambda i: (i, 0)), pl.BlockSpec((tm * ROW_TILES, LANES), lambda i: (i, 0))]
    shapes = [jax.ShapeDtypeStruct((m, D_MODEL), F32), jax.ShapeDtypeStruct((m * ROW_TILES, LANES), F32)]
    return specs, shapes


def _even_out_kernel(a1_ref, a2_ref, w1_ref, w2_ref, x_ref, g_ref, b_ref, o_ref, of_ref):
    mix = _dot(a1_ref[...], w1_ref[...]) + _dot(a2_ref[...], w2_ref[...])
    _store_row_layouts(_layer_norm(DEEPNORM_ALPHA * x_ref[...] + mix, g_ref[...], b_ref[...]), o_ref, of_ref)


def _even_out(y_ssm, y_att, w1, w2, x, g, b):
    m = x.shape[0]
    tm = min(256, m)
    row = lambda i: (i, 0)
    fix = lambda i: (0, 0)
    out_specs, out_shape = _row_layout_outputs(m, tm)
    return pl.pallas_call(
        _even_out_kernel,
        grid=(m // tm,),
        in_specs=[
            pl.BlockSpec((tm, y_ssm.shape[1]), row),
            pl.BlockSpec((tm, y_att.shape[1]), row),
            pl.BlockSpec(w1.shape, fix),
            pl.BlockSpec(w2.shape, fix),
            pl.BlockSpec((tm, D_MODEL), row),
            pl.BlockSpec((1, D_MODEL), fix),
            pl.BlockSpec((1, D_MODEL), fix),
        ],
        out_specs=out_specs,
        out_shape=out_shape,
        compiler_params=_cparams(("parallel",)),
        name="even_out_ln",
    )(y_ssm, y_att, w1, w2, x, g, b)


def _ssd_kernel(z_ref, xs_ref, bc_ref, dt_ref, cwx_ref, cbx_ref, cwb_ref, cbb_ref, dtb_ref, alog_ref,
                dsk_ref, nrm_ref, e_ref, y_ref, xs_ext, bc_ext, st_ref):
    c = pl.program_id(0)
    L = SSM_CHUNK
    halo = SUBLANES

    @pl.when(c == 0)
    def _():
        xs_ext[0:halo, :] = jnp.zeros((halo, xs_ext.shape[1]), F32)
        bc_ext[0:halo, :] = jnp.zeros((halo, bc_ext.shape[1]), F32)
        st_ref[...] = jnp.zeros(st_ref.shape, F32)

    xs_ext[halo:halo + L, :] = xs_ref[...]
    bc_ext[halo:halo + L, :] = bc_ref[...]

    def conv(ext, w_ref, b_ref):
        acc = b_ref[...]
        for j in range(SSM_CONV):
            lo = halo - (SSM_CONV - 1) + j
            acc = acc + ext[lo:lo + L, :] * w_ref[j:j + 1, :]
        return acc

    xs = _silu(conv(xs_ext, cwx_ref, cbx_ref))
    bc = _silu(conv(bc_ext, cwb_ref, cbb_ref))
    xs_ext[0:halo, :] = xs_ext[L:L + halo, :]
    bc_ext[0:halo, :] = bc_ext[L:L + halo, :]

    pre = dt_ref[...] + dtb_ref[...]
    dt = jnp.maximum(pre, 0.0) + jnp.log1p(jnp.exp(-jnp.abs(pre)))
    a = dt * (-jnp.exp(alog_ref[...]))
    row = lax.broadcasted_iota(I32, (L, LANES), 0)
    acs = a
    s = 1
    while s < L:
        acs = acs + jnp.where(row >= s, pltpu.roll(acs, s, 0), 0.0)
        s *= 2
    a_last = acs[L - 1:L, :]
    e = e_ref[...]
    dt_x = _expand(dt, e)
    dte_x = _expand(dt * jnp.exp(a_last - acs), e)
    eacs_x = _expand(jnp.exp(acs), e)
    cd_x = _expand(jnp.broadcast_to(jnp.exp(a_last), (SUBLANES, LANES)), e)[0:1, :]
    acs_t = acs.T

    xdt = (xs * dt_x).astype(BF16)
    xd = (xs * dte_x).astype(BF16)
    tri = lax.broadcasted_iota(I32, (L, L), 0) >= lax.broadcasted_iota(I32, (L, L), 1)
    first_head = lax.broadcasted_iota(I32, (L, LANES), 1) < SSM_HEAD_DIM
    n_state = SSM_STATE
    gw = SSM_INNER // SSM_GROUPS
    ys = []
    for g in range(SSM_GROUPS):
        bg = bc[:, g * n_state:(g + 1) * n_state]
        cg = bc[:, SSM_GROUPS * n_state + g * n_state:SSM_GROUPS * n_state + (g + 1) * n_state]
        bb, cb16 = bg.astype(BF16), cg.astype(BF16)
        cbm = _dot_nt(cb16, bb)
        st = st_ref[g]
        y_off = _dot(cb16, st.astype(BF16)) * eacs_x[:, g * gw:(g + 1) * gw]
        st_ref[g] = st * cd_x[:, g * gw:(g + 1) * gw] + _dot(bg.T.astype(BF16), xd[:, g * gw:(g + 1) * gw])
        parts = []
        for j in range(gw // LANES):
            lo = g * gw + j * LANES
            xp = xdt[:, lo:lo + LANES]
            out = None
            for par in (0, 1):
                h = lo // SSM_HEAD_DIM + par
                seg = acs[:, h:h + 1] - acs_t[h:h + 1, :]
                lm = (jnp.exp(jnp.where(tri, seg, -jnp.inf)) * cbm).astype(BF16)
                xm = jnp.where(first_head if par == 0 else jnp.logical_not(first_head), xp, jnp.zeros_like(xp))
                d = _dot(lm, xm)
                out = d if out is None else out + d
            parts.append(out)
        ys.append(jnp.concatenate(parts, axis=1) + y_off)
    y = jnp.concatenate(ys, axis=1) + xs * dsk_ref[...]
    y = y * _silu(z_ref[...])
    outs = []
    for g in range(SSM_GROUPS):
        yg = y[:, g * gw:(g + 1) * gw]
        ms = jnp.mean(yg * yg, axis=-1, keepdims=True)
        outs.append(yg * lax.rsqrt(ms + NORM_EPS))
    y_ref[...] = (jnp.concatenate(outs, axis=1) * nrm_ref[...]).astype(BF16)


def _ssd(xp, cw, cb, dtb, alog, dsk, nrm, e):
    s = xp.shape[0]
    L = SSM_CHUNK
    bcw = 2 * SSM_GROUPS * SSM_STATE
    cwx, cwb = cw[:, :SSM_INNER], cw[:, SSM_INNER:]
    cbx, cbb = cb[:, :SSM_INNER], cb[:, SSM_INNER:]
    fix = lambda i: (0, 0)
    return pl.pallas_call(
        _ssd_kernel,
        grid=(s // L,),
        in_specs=[
            pl.BlockSpec((L, SSM_INNER), lambda i: (i, EV_Z // SSM_INNER)),
            pl.BlockSpec((L, SSM_INNER), lambda i: (i, EV_XS // SSM_INNER)),
            pl.BlockSpec((L, bcw), lambda i: (i, EV_BC // bcw)),
            pl.BlockSpec((L, LANES), lambda i: (i, EV_DT // LANES)),
            pl.BlockSpec(cwx.shape, fix), pl.BlockSpec(cbx.shape, fix),
            pl.BlockSpec(cwb.shape, fix), pl.BlockSpec(cbb.shape, fix),
            pl.BlockSpec((1, LANES), fix), pl.BlockSpec((1, LANES), fix),
            pl.BlockSpec((1, SSM_INNER), fix), pl.BlockSpec((1, SSM_INNER), fix),
            pl.BlockSpec(e.shape, fix),
        ],
        out_specs=pl.BlockSpec((L, SSM_INNER), lambda i: (i, 0)),
        out_shape=jax.ShapeDtypeStruct((s, SSM_INNER), BF16),
        scratch_shapes=[
            pltpu.VMEM((L + 2 * SUBLANES, SSM_INNER), F32),
            pltpu.VMEM((L + 2 * SUBLANES, bcw), F32),
            pltpu.VMEM((SSM_GROUPS, SSM_STATE, SSM_INNER // SSM_GROUPS), F32),
        ],
        compiler_params=_cparams(("arbitrary",)),
        name="ssd_scan",
    )(xp, xp, xp, xp, cwx, cbx, cwb, cbb, dtb, alog, dsk, nrm, e)


def _swa_kernel(sink_ref, q_ref, kvc_ref, kvp_ref, cq_ref, sq_ref, cp_ref, sp_ref, o_ref):
    i = pl.program_id(0)
    B = ATTN_BLOCK
    lane = lax.broadcasted_iota(I32, (B, LANES), 1)
    lo_half = lane < ROPE_DIM
    cq, sq = cq_ref[...], sq_ref[...]
    kc = _rope_tile(kvc_ref[:, 0:LANES], cq, sq)
    kp = _rope_tile(kvp_ref[:, 0:LANES], cp_ref[...], sp_ref[...])
    kcat = jnp.concatenate([kp, kc], axis=0)
    kmat = (kcat.astype(BF16), pltpu.roll(kcat, ROPE_DIM, 1).astype(BF16))
    vcat = jnp.concatenate([kvp_ref[:, LANES:2 * LANES], kvc_ref[:, LANES:2 * LANES]], axis=0)
    vrol = pltpu.roll(vcat, ROPE_DIM, 1)
    lane2 = lax.broadcasted_iota(I32, (2 * B, LANES), 1) < ROPE_DIM
    vdup = (jnp.where(lane2, vcat, vrol).astype(BF16), jnp.where(lane2, vrol, vcat).astype(BF16))
    r = lax.broadcasted_iota(I32, (B, 2 * B), 0)
    col = lax.broadcasted_iota(I32, (B, 2 * B), 1)
    mask = (col > r) & (col <= r + B) & ((i > 0) | (col >= B))
    scale = ROPE_DIM ** -0.5
    hpg = SWA_Q_HEADS // SWA_KV_HEADS
    for j in range(SWA_Q_HEADS // 2):
        g = (2 * j) // hpg
        qt = _rope_tile(q_ref[:, j * LANES:(j + 1) * LANES], cq, sq)
        outs = []
        for par in (0, 1):
            h = 2 * j + par
            qm = jnp.where(lo_half if par == 0 else jnp.logical_not(lo_half), qt, 0.0).astype(BF16)
            logit = _dot_nt(qm, kmat[0] if par == g else kmat[1]) * scale
            logit = jnp.where(mask, logit, -jnp.inf)
            sink = sink_ref[h]
            m = jnp.maximum(jnp.max(logit, axis=-1, keepdims=True), sink)
            ex = jnp.exp(logit - m)
            prob = ex / (jnp.sum(ex, axis=-1, keepdims=True) + jnp.exp(sink - m))
            outs.append(_dot(prob.astype(BF16), vdup[g]))
        o_ref[:, j * LANES:(j + 1) * LANES] = jnp.where(lo_half, outs[0], outs[1]).astype(BF16)


def _swa(xp, sinks, cos_t, sin_t):
    s = xp.shape[0]
    B = ATTN_BLOCK
    qw = SWA_Q_HEADS * ROPE_DIM
    kvw = 2 * SWA_KV_HEADS * ROPE_DIM
    prev = lambda i: (jnp.maximum(i - 1, 0), 0)
    cur = lambda i: (i, 0)
    return pl.pallas_call(
        _swa_kernel,
        grid=(s // B,),
        in_specs=[
            pl.BlockSpec(memory_space=pltpu.SMEM),
            pl.BlockSpec((B, qw), lambda i: (i, EV_Q // qw)),
            pl.BlockSpec((B, kvw), lambda i: (i, EV_KV // kvw)),
            pl.BlockSpec((B, kvw), lambda i: (jnp.maximum(i - 1, 0), EV_KV // kvw)),
            pl.BlockSpec((B, LANES), cur), pl.BlockSpec((B, LANES), cur),
            pl.BlockSpec((B, LANES), prev), pl.BlockSpec((B, LANES), prev),
        ],
        out_specs=pl.BlockSpec((B, qw), cur),
        out_shape=jax.ShapeDtypeStruct((s, qw), BF16),
        compiler_params=_cparams(("parallel",)),
        name="swa_sink",
    )(sinks, xp, xp, xp, cos_t, sin_t, cos_t, sin_t)


def _dsa_prep_kernel(qn_ref, qr_ref, qi_ref, ckv_ref, kk_ref, wi_ref, wuk_ref, kvn_ref, c_ref, s_ref,
                     qlat_ref, qrope_ref, qidx_ref, ckvn_ref, kr_ref, ki_ref, wis_ref):
    c, s = c_ref[...], s_ref[...]
    lane = lax.broadcasted_iota(I32, c.shape, 1)
    lo_half = lane < ROPE_DIM
    for h in range(MLA_HEADS):
        qn = qn_ref[:, h * MLA_NOPE:(h + 1) * MLA_NOPE].astype(BF16)
        qlat_ref[h] = _dot(qn, wuk_ref[h]).astype(BF16)
    for src, dst in ((qr_ref, qrope_ref), (qi_ref, qidx_ref)):
        for j in range(MLA_HEADS // 2):
            t = _rope_tile(src[:, j * LANES:(j + 1) * LANES], c, s)
            dst[2 * j] = jnp.where(lo_half, t, 0.0).astype(BF16)
            dst[2 * j + 1] = jnp.where(lo_half, pltpu.roll(t, ROPE_DIM, 1), 0.0).astype(BF16)
    kk = _rope_tile(kk_ref[...], c, s)
    kr_ref[...] = jnp.where(lo_half, kk, 0.0).astype(BF16)
    ki_ref[...] = jnp.where(lo_half, pltpu.roll(kk, ROPE_DIM, 1), 0.0).astype(BF16)
    ckv = ckv_ref[...]
    ms = jnp.mean(ckv * ckv, axis=-1, keepdims=True)
    ckvn_ref[...] = (ckv * lax.rsqrt(ms + NORM_EPS) * kvn_ref[...]).astype(BF16)
    wis_ref[...] = wi_ref[...] * (IDX_HEADS ** -0.5 * IDX_DIM ** -0.5)


def _dsa_prep(xp, wuk, kvn, cos_t, sin_t):
    s = xp.shape[0]
    tm = min(256, s)
    H = MLA_HEADS
    fix2 = lambda i: (0, 0)
    hrow = lambda i: (0, i, 0)
    row = lambda i: (i, 0)
    return pl.pallas_call(
        _dsa_prep_kernel,
        grid=(s // tm,),
        in_specs=[
            pl.BlockSpec((tm, 2048), lambda i: (i, OD_QN // 2048)),
            pl.BlockSpec((tm, 1024), lambda i: (i, OD_QR // 1024)),
            pl.BlockSpec((tm, 1024), lambda i: (i, OD_QI // 1024)),
            pl.BlockSpec((tm, MLA_RANK), lambda i: (i, OD_CKV // MLA_RANK)),
            pl.BlockSpec((tm, LANES), lambda i: (i, OD_KK // LANES)),
            pl.BlockSpec((tm, LANES), lambda i: (i, OD_WI // LANES)),
            pl.BlockSpec(wuk.shape, lambda i: (0, 0, 0)),
            pl.BlockSpec((1, MLA_RANK), fix2),
            pl.BlockSpec((tm, LANES), row), pl.BlockSpec((tm, LANES), row),
        ],
        out_specs=[
            pl.BlockSpec((H, tm, MLA_RANK), hrow),
            pl.BlockSpec((H, tm, LANES), hrow),
            pl.BlockSpec((H, tm, LANES), hrow),
            pl.BlockSpec((tm, MLA_RANK), row),
            pl.BlockSpec((tm, LANES), row),
            pl.BlockSpec((tm, LANES), row),
            pl.BlockSpec((tm, LANES), row),
        ],
        out_shape=[
            jax.ShapeDtypeStruct((H, s, MLA_RANK), BF16),
            jax.ShapeDtypeStruct((H, s, LANES), BF16),
            jax.ShapeDtypeStruct((H, s, LANES), BF16),
            jax.ShapeDtypeStruct((s, MLA_RANK), BF16),
            jax.ShapeDtypeStruct((s, LANES), BF16),
            jax.ShapeDtypeStruct((s, LANES), BF16),
            jax.ShapeDtypeStruct((s, LANES), F32),
        ],
        compiler_params=_cparams(("parallel",)),
        name="dsa_prep",
    )(xp, xp, xp, xp, xp, xp, wuk, kvn, cos_t, sin_t)


def _dsa_select_kernel(topk, qi_ref, wi_ref, ki_ref, bias_ref, key_ref, dig_ref):
    i = pl.program_id(0)
    Q, KC = DSA_SEL_Q, DSA_KC
    H = IDX_HEADS
    n_chunks = bias_ref.shape[0]
    n_vis = ((i + 1) * Q + KC - 1) // KC
    wi = wi_ref[...]
    qpos = i * Q + lax.broadcasted_iota(I32, (KC, Q), 1)
    kloc = lax.broadcasted_iota(I32, (KC, Q), 0)
    HG = DSA_SEL_HEADS_PER_DOT
    N_ACC = 4
    DIGIT_BITS = (11, 11, 10)
    DIGIT_SHIFT = (21, 10, 0)
    TOP_BIAS = 1 << (DIGIT_BITS[0] - 1)
    GUARD = jnp.int32(-0x7FFF8000)
    FIELD_ONES = jnp.int32(0x00010001)
    HK = KC // 2

    def pack(lo, hi):
        return lo | (hi << 16) | GUARD

    def score_chunk(c, carry):
        k = ki_ref[pl.ds(pl.multiple_of(c * KC, KC), KC), :]
        acc = jnp.zeros((KC, Q), F32)
        for g in range(H // HG):
            sc = _dot_nt(k, qi_ref[g * HG:(g + 1) * HG].reshape(HG * Q, LANES))
            for hh in range(HG):
                h = g * HG + hh
                acc = acc + jnp.maximum(sc[:, hh * Q:(hh + 1) * Q], 0.0) * wi[h:h + 1, :]
        acc = jnp.where(c * KC + kloc <= qpos, acc, -jnp.inf)
        bits = pltpu.bitcast(acc, I32)
        key = bits ^ ((bits >> 31) & jnp.int32(0x7FFFFFFF))
        key_ref[c] = key
        for d in range(len(DIGIT_BITS)):
            dig = (key >> DIGIT_SHIFT[d]) + TOP_BIAS if d == 0 else (key >> DIGIT_SHIFT[d]) & ((1 << DIGIT_BITS[d]) - 1)
            dig_ref[d, c] = pack(dig[:HK], dig[HK:]) + FIELD_ONES
        return carry

    lax.fori_loop(0, n_vis, score_chunk, 0)

    def count_ge(d, cand):
        cand2 = (cand + 1) | ((cand + 1) << 16)

        def body(c, accs):
            v = dig_ref[d, c]
            accs = list(accs)
            for r in range(HK // SUBLANES):
                w = v[r * SUBLANES:(r + 1) * SUBLANES, :] - cand2
                accs[r % N_ACC] = accs[r % N_ACC] + (lax.shift_right_logical(w, 15) & FIELD_ONES)
            return tuple(accs)
        accs = lax.fori_loop(0, n_vis, body, tuple(jnp.zeros((SUBLANES, Q), I32) for _ in range(N_ACC)))
        acc = sum(accs)
        cnt = (acc & 0xFFFF) + lax.shift_right_logical(acc, 16)
        return jnp.sum(cnt.astype(F32), axis=0, keepdims=True)

    def search(d, need):
        def bit_step(b, thr):
            cand = thr | (jnp.int32(1) << (DIGIT_BITS[d] - 1 - b))
            return jnp.where(count_ge(d, cand) >= need, cand, thr)

        return lax.fori_loop(0, DIGIT_BITS[d], bit_step, jnp.zeros((1, Q), I32))

    def drop_unless_equal(d, p):
        def body(c, carry):
            cur, nxt = dig_ref[d, c], dig_ref[d + 1, c]
            lo = jnp.where((cur & 0x7FFF) == p + 1, nxt & 0x7FFF, 0)
            hi = jnp.where((lax.shift_right_logical(cur, 16) & 0x7FFF) == p + 1, lax.shift_right_logical(nxt, 16) & 0x7FFF, 0)
            dig_ref[d + 1, c] = pack(lo, hi)
            return carry

        lax.fori_loop(0, n_vis, body, 0)

    need = jnp.full((1, Q), float(topk), F32)
    thr = jnp.zeros((1, Q), I32)
    for d in range(len(DIGIT_BITS)):
        p = search(d, need)
        thr = thr + ((p - (TOP_BIAS if d == 0 else 0)) << DIGIT_SHIFT[d])
        if d + 1 < len(DIGIT_BITS):
            need = need - count_ge(d, p + 1)
            drop_unless_equal(d, p)

    def emit(c, cnt):
        sel = (key_ref[c] >= thr) & (c * KC + kloc <= qpos)
        bias_ref[c] = jnp.where(sel, 0.0, MASK_NEG).T
        return cnt + jnp.sum(jnp.where(sel, 1.0, 0.0), axis=0, keepdims=True)

    n_sel = lax.fori_loop(0, n_vis, emit, jnp.zeros((1, Q), F32))

    @pl.when(jnp.max(n_sel) > topk)
    def _():
        def count(pred):
            def body(c, cnt):
                return cnt + jnp.sum(jnp.where(pred(c), 1.0, 0.0), axis=0, keepdims=True)
            return lax.fori_loop(0, n_vis, body, jnp.zeros((1, Q), F32))

        def tied(c):
            return (key_ref[c] == thr) & (c * KC + kloc <= qpos)

        need_tied = topk - count(lambda c: key_ref[c] > thr)

        def bit_step(b, last):
            cand = last | (jnp.int32(1) << (index_bits - 1 - b))
            in_front = count(lambda c: tied(c) & (c * KC + kloc < cand))
            return jnp.where(in_front < need_tied, cand, last)

        index_bits = (n_chunks * KC - 1).bit_length()
        last = lax.fori_loop(0, index_bits, bit_step, jnp.zeros((1, Q), I32))

        def emit_ties(c, carry):
            sel = (key_ref[c] > thr) | (tied(c) & (c * KC + kloc <= last))
            bias_ref[c] = jnp.where(sel, 0.0, MASK_NEG).T
            return carry

        lax.fori_loop(0, n_vis, emit_ties, 0)

    def fill(c, carry):
        bias_ref[c] = jnp.full((Q, KC), MASK_NEG, F32)
        return carry

    lax.fori_loop(n_vis, n_chunks, fill, 0)


def _dsa_select(qidx, wis, ki, topk):
    H, s, _ = qidx.shape
    Q, KC = DSA_SEL_Q, DSA_KC
    assert Q == LANES
    nch = s // KC
    return pl.pallas_call(
        functools.partial(_dsa_select_kernel, topk),
        grid=(s // Q,),
        in_specs=[
            pl.BlockSpec((H, Q, LANES), lambda i: (0, i, 0)),
            pl.BlockSpec((LANES, Q), lambda i: (0, i)),
            pl.BlockSpec((s, LANES), lambda i: (0, 0)),
        ],
        out_specs=pl.BlockSpec((nch, Q, KC), lambda i: (0, i, 0)),
        out_shape=jax.ShapeDtypeStruct((nch, s, KC), F32),
        scratch_shapes=[pltpu.VMEM((nch, KC, Q), I32), pltpu.VMEM((3, nch, KC // 2, Q), I32)],
        compiler_params=_cparams(("parallel",)),
        name="dsa_select",
    )(qidx, wis.T, ki)


def _dsa_attn_kernel(ql_ref, qr_ref, bias_ref, ckv_ref, kr_ref, o_ref, m_ref, l_ref, acc_ref, s_ref):
    i = pl.program_id(0)
    Q, KC, H = DSA_ATT_Q, DSA_KC, MLA_HEADS
    n_vis = ((i + 1) * Q + KC - 1) // KC
    m_ref[...] = jnp.full(m_ref.shape, -jnp.inf, F32)
    l_ref[...] = jnp.zeros(l_ref.shape, F32)
    acc_ref[...] = jnp.zeros(acc_ref.shape, F32)

    def keys(c):
        off = pl.multiple_of(c * KC, KC)
        return ckv_ref[pl.ds(off, KC), :], kr_ref[pl.ds(off, KC), :]

    def raw_logits(c):
        ck, kr = keys(c)
        ql = ql_ref[...].reshape(H * Q, MLA_RANK)
        qr = qr_ref[...].reshape(H * Q, LANES)
        return _dot_nt(ql, ck) + _dot_nt(qr, kr)

    def consume(c, slot):
        logit = s_ref[slot] * MLA_SCALE
        logit = (logit.reshape(H, Q, KC) + bias_ref[c][None]).reshape(H * Q, KC)
        m_old = m_ref[...]
        m_new = jnp.maximum(m_old, jnp.max(logit, axis=-1, keepdims=True))
        alpha = jnp.exp(m_old - m_new)
        p = jnp.exp(logit - m_new)
        l_ref[...] = alpha * l_ref[...] + jnp.sum(p, axis=-1, keepdims=True)
        acc_ref[...] = alpha * acc_ref[...] + _dot(p.astype(BF16), keys(c)[0])
        m_ref[...] = m_new

    s_ref[0] = raw_logits(0)

    def pair(j, carry):
        c = 2 * j
        s_ref[1] = raw_logits(c + 1)
        consume(c, 0)
        s_ref[0] = raw_logits(c + 2)
        consume(c + 1, 1)
        return carry

    n_pairs = (n_vis - 1) // 2
    lax.fori_loop(0, n_pairs, pair, 0)
    last = 2 * n_pairs

    @pl.when(last + 1 < n_vis)
    def _():
        s_ref[1] = raw_logits(last + 1)
        consume(last, 0)
        consume(last + 1, 1)

    @pl.when(last + 1 == n_vis)
    def _():
        consume(last, 0)

    o_ref[...] = (acc_ref[...] / l_ref[...]).reshape(H, Q, MLA_RANK).astype(BF16)


def _dsa_attn(qlat, qrope, bias, ckvn, kr):
    H, s, _ = qlat.shape
    Q, KC = DSA_ATT_Q, DSA_KC
    nch = s // KC
    hrow = lambda i: (0, i, 0)
    fix = lambda i: (0, 0)
    return pl.pallas_call(
        _dsa_attn_kernel,
        grid=(s // Q,),
        in_specs=[
            pl.BlockSpec((H, Q, MLA_RANK), hrow),
            pl.BlockSpec((H, Q, LANES), hrow),
            pl.BlockSpec((nch, Q, KC), hrow),
            pl.BlockSpec((s, MLA_RANK), fix),
            pl.BlockSpec((s, LANES), fix),
        ],
        out_specs=pl.BlockSpec((H, Q, MLA_RANK), hrow),
        out_shape=jax.ShapeDtypeStruct((H, s, MLA_RANK), BF16),
        scratch_shapes=[
            pltpu.VMEM((H * Q, 1), F32),
            pltpu.VMEM((H * Q, 1), F32),
            pltpu.VMEM((H * Q, MLA_RANK), F32),
            pltpu.VMEM((2, H * Q, KC), F32),
        ],
        compiler_params=_cparams(("parallel",)),
        name="dsa_attn",
    )(qlat, qrope, bias, ckvn, kr)


def _odd_out_kernel(ol_ref, wuv_ref, wo_ref, x_ref, g_ref, b_ref, o_ref, of_ref, u_ref):
    for h in range(MLA_HEADS):
        u_ref[:, h * MLA_V:(h + 1) * MLA_V] = _dot(ol_ref[h], wuv_ref[h]).astype(BF16)
    mix = _dot(u_ref[...], wo_ref[...])
    _store_row_layouts(_layer_norm(DEEPNORM_ALPHA * x_ref[...] + mix, g_ref[...], b_ref[...]), o_ref, of_ref)


def _odd_out(olat, wuv, wo, x, g, b):
    H, s, _ = olat.shape
    tm = min(256, s)
    row = lambda i: (i, 0)
    fix = lambda i: (0, 0)
    out_specs, out_shape = _row_layout_outputs(s, tm)
    return pl.pallas_call(
        _odd_out_kernel,
        grid=(s // tm,),
        in_specs=[
            pl.BlockSpec((H, tm, MLA_RANK), lambda i: (0, i, 0)),
            pl.BlockSpec(wuv.shape, lambda i: (0, 0, 0)),
            pl.BlockSpec(wo.shape, fix),
            pl.BlockSpec((tm, D_MODEL), row),
            pl.BlockSpec((1, D_MODEL), fix), pl.BlockSpec((1, D_MODEL), fix),
        ],
        out_specs=out_specs,
        out_shape=out_shape,
        scratch_shapes=[pltpu.VMEM((tm, MLA_HEADS * MLA_V), BF16)],
        compiler_params=_cparams(("parallel",)),
        name="odd_out_ln",
    )(olat, wuv, wo, x, g, b)


ROUTE_LANE0 = MOE_GROUPS


def _router_kernel(h_ref, wr_ref, br_ref, info_ref, cnt_ref, run_ref):
    i = pl.program_id(0)

    @pl.when(i == 0)
    def _():
        run_ref[...] = jnp.zeros(run_ref.shape, F32)

    h = h_ref[...]
    tm = h.shape[0]
    h_hi = h.astype(BF16)
    h_lo = (h - h_hi.astype(F32)).astype(BF16)
    w = wr_ref[...]
    w_hi = w.astype(BF16)
    w_lo = (w - w_hi.astype(F32)).astype(BF16)
    logits = _dot(h_hi, w_hi) + (_dot(h_hi, w_lo) + _dot(h_lo, w_hi)) + br_ref[...]
    lane = lax.broadcasted_iota(I32, (tm, LANES), 1)
    lane_f = lane.astype(F32)
    neg = -jnp.inf
    big = float(LANES)
    is_grp = lane < MOE_GROUPS
    gl = jnp.where(is_grp, logits, neg)
    gmax = jnp.max(gl, axis=-1, keepdims=True)
    gsel = jnp.min(jnp.where(gl == gmax, lane_f, big), axis=-1, keepdims=True)
    gsum = jnp.sum(jnp.where(is_grp, jnp.exp(logits - gmax), 0.0), axis=-1, keepdims=True)
    egrp = ((lane - ROUTE_LANE0) >> 3).astype(F32)
    valid = (lane >= ROUTE_LANE0) & (lane < ROUTE_LANE0 + MOE_EXPERTS) & (egrp == gsel)
    el = jnp.where(valid, logits, neg)
    v1 = jnp.max(el, axis=-1, keepdims=True)
    i1 = jnp.min(jnp.where(el == v1, lane_f, big), axis=-1, keepdims=True)
    el2 = jnp.where(lane_f == i1, neg, el)
    v2 = jnp.max(el2, axis=-1, keepdims=True)
    i2 = jnp.min(jnp.where(el2 == v2, lane_f, big), axis=-1, keepdims=True)
    t = jnp.exp(v2 - v1)
    p1 = 1.0 / (1.0 + t)
    p2 = t / (1.0 + t)
    ggate = 1.0 / gsum
    m1 = lane_f == i1
    m2 = lane_f == i2
    memb = jnp.where(m1 | m2, 1.0, 0.0)
    tri = (lax.broadcasted_iota(I32, (tm, tm), 0) > lax.broadcasted_iota(I32, (tm, tm), 1))
    cum = _dot(jnp.where(tri, 1.0, 0.0).astype(BF16), memb.astype(BF16)) + run_ref[...]
    rank1 = jnp.sum(jnp.where(m1, cum, 0.0), axis=-1, keepdims=True)
    rank2 = jnp.sum(jnp.where(m2, cum, 0.0), axis=-1, keepdims=True)
    run_ref[...] = run_ref[...] + jnp.sum(memb, axis=0, keepdims=True)
    info = jnp.where(lane == 0, i1 - ROUTE_LANE0, 0.0)
    info = jnp.where(lane == 1, i2 - ROUTE_LANE0, info)
    info = jnp.where(lane == 2, p1 * ggate, info)
    info = jnp.where(lane == 3, p2 * ggate, info)
    info = jnp.where(lane == 4, rank1, info)
    info = jnp.where(lane == 5, rank2, info)
    info_ref[...] = info
    cnt_ref[...] = run_ref[...]


def _router(h, wr, br):
    t = h.shape[0]
    tm = min(512, t)
    return pl.pallas_call(
        _router_kernel,
        grid=(t // tm,),
        in_specs=[
            pl.BlockSpec((tm, D_MODEL), lambda i: (i, 0)),
            pl.BlockSpec(wr.shape, lambda i: (0, 0)),
            pl.BlockSpec((1, LANES), lambda i: (0, 0)),
        ],
        out_specs=[pl.BlockSpec((tm, LANES), lambda i: (i, 0)), pl.BlockSpec((1, LANES), lambda i: (0, 0))],
        out_shape=[jax.ShapeDtypeStruct((t, LANES), F32), jax.ShapeDtypeStruct((1, LANES), F32)],
        scratch_shapes=[pltpu.VMEM((1, LANES), F32)],
        compiler_params=_cparams(("arbitrary",)),
        name="moe_router",
    )(h, wr, br)


def _plan_kernel(info_ref, ps_ref, pos_ref):
    info = info_ref[...]
    lane = lax.broadcasted_iota(I32, info.shape, 1)
    lane_f = lane.astype(F32)
    ps = ps_ref[...]
    pos1 = jnp.sum(jnp.where(lane_f == info[:, 0:1], ps, 0.0), axis=-1, keepdims=True) + info[:, 4:5]
    pos2 = jnp.sum(jnp.where(lane_f == info[:, 1:2], ps, 0.0), axis=-1, keepdims=True) + info[:, 5:6]
    pos_ref[...] = jnp.where(lane == 0, pos1, jnp.where(lane == 1, pos2, 0.0)).astype(I32)


def _plan(info, pad_start):
    t = info.shape[0]
    tm = min(1024, t)
    return pl.pallas_call(
        _plan_kernel,
        grid=(t // tm,),
        in_specs=[pl.BlockSpec((tm, LANES), lambda i: (i, 0)), pl.BlockSpec((1, LANES), lambda i: (0, 0))],
        out_specs=pl.BlockSpec((tm, LANES), lambda i: (i, 0)),
        out_shape=jax.ShapeDtypeStruct((t, LANES), I32),
        compiler_params=_cparams(("parallel",)),
        name="moe_plan",
    )(info, pad_start)


def _invert_kernel(pos_ref, rt_ref):
    def clear(r, carry):
        rt_ref[r] = 0
        return carry

    lax.fori_loop(0, rt_ref.shape[0], clear, 0, unroll=8)

    def put(n, carry):
        rt_ref[pos_ref[n]] = lax.shift_right_logical(n, 1)
        return carry

    lax.fori_loop(0, pos_ref.shape[0], put, 0, unroll=8)


def _invert(pos_flat, n_rows):
    return pl.pallas_call(
        _invert_kernel,
        in_specs=[pl.BlockSpec(memory_space=pltpu.SMEM)],
        out_specs=pl.BlockSpec(memory_space=pltpu.SMEM),
        out_shape=jax.ShapeDtypeStruct((n_rows,), I32),
        name="moe_invert",
    )(pos_flat)


def _expert_kernel(be_ref, nu_ref, rt_ref, h_ref, wg_ref, wu_ref, wd_ref, o_ref, xbuf0, xbuf1, xb, wgb, wub, wdb, sem):
    b = pl.program_id(0)
    n_used = nu_ref[0]
    R = EXPERT_ROWS

    bufs = (xbuf0, xbuf1)

    def row_copy(tok, r, slot):
        src = h_ref.at[pl.ds(pl.multiple_of(tok * ROW_TILES, ROW_TILES), ROW_TILES)]
        return pltpu.make_async_copy(src, bufs[slot].at[pl.ds(r * ROW_TILES, ROW_TILES)], sem.at[slot])

    def drain(slot):
        def body(r, carry):
            row_copy(0, 0, slot).wait()
            return carry
        lax.fori_loop(0, R, body, 0, unroll=8)

    @pl.when(b == 0)
    def _():
        def body(r, carry):
            row_copy(rt_ref[r], r, 0).start()
            return carry
        lax.fori_loop(0, R, body, 0, unroll=8)

    def block(slot):
        drain(slot)

        @pl.when((b == 0) | (be_ref[b] != be_ref[jnp.maximum(b - 1, 0)]))
        def _():
            wgb[...] = wg_ref[0, 0].astype(BF16)
            wub[...] = wu_ref[0, 0].astype(BF16)
            wdb[...] = wd_ref[0, 0].astype(BF16)

        base = (b + 1) * R
        for r in range(R):
            row_copy(rt_ref[base + r], r, 1 - slot).start()
        for j in range(ROW_TILES):
            xb[:, j * LANES:(j + 1) * LANES] = bufs[slot][pl.ds(j, R, stride=ROW_TILES), :].astype(BF16)
        x = xb[...]
        hid = _silu(_dot(x, wgb[...])) * _dot(x, wub[...])
        y = _dot(hid.astype(BF16), wdb[...])
        for j in range(ROW_TILES):
            o_ref[pl.ds(j, R, stride=ROW_TILES), :] = y[:, j * LANES:(j + 1) * LANES]

    for parity in (0, 1):
        pl.when((b < n_used) & (b % 2 == parity))(functools.partial(block, parity))

    @pl.when(b >= n_used)
    def _():
        o_ref[...] = jnp.zeros(o_ref.shape, F32)

    for parity in (0, 1):
        pl.when((b == n_used) & (b % 2 == parity))(functools.partial(drain, parity))


def _experts(blk_e, n_used, row_token, h, wg, wu, wd, layer):
    r = row_token.shape[0]
    nb = r // EXPERT_ROWS
    wmap = lambda b, be, nu, rt: (layer, be[b], 0, 0)
    return pl.pallas_call(
        _expert_kernel,
        grid_spec=pltpu.PrefetchScalarGridSpec(
            num_scalar_prefetch=3,
            grid=(nb,),
            in_specs=[
                pl.BlockSpec(memory_space=pl.ANY),
                pl.BlockSpec((1, 1, D_MODEL, MOE_FF), wmap),
                pl.BlockSpec((1, 1, D_MODEL, MOE_FF), wmap),
                pl.BlockSpec((1, 1, MOE_FF, D_MODEL), wmap),
            ],
            out_specs=pl.BlockSpec((EXPERT_ROWS * ROW_TILES, LANES), lambda b, be, nu, rt: (b, 0)),
            scratch_shapes=[
                pltpu.VMEM((EXPERT_ROWS * ROW_TILES, LANES), F32),
                pltpu.VMEM((EXPERT_ROWS * ROW_TILES, LANES), F32),
                pltpu.VMEM((EXPERT_ROWS, D_MODEL), BF16),
                pltpu.VMEM((D_MODEL, MOE_FF), BF16),
                pltpu.VMEM((D_MODEL, MOE_FF), BF16),
                pltpu.VMEM((MOE_FF, D_MODEL), BF16),
                pltpu.SemaphoreType.DMA((2,)),
            ],
        ),
        out_shape=jax.ShapeDtypeStruct((r * ROW_TILES, LANES), F32),
        compiler_params=_cparams(("arbitrary",)),
        name="moe_experts",
    )(blk_e, n_used, row_token, h, wg, wu, wd)


COMBINE_TOKENS = 256


def _combine_ple_kernel(pos_ref, y_ref, info_ref, x_ref, g_ref, b_ref, p_ref, wg_ref, bg_ref, wp_ref, o_ref,
                        buf0, buf1, ycat, sem):
    i = pl.program_id(0)
    n = pl.num_programs(0)
    TB = x_ref.shape[0]
    bufs = (buf0, buf1)

    def row_copy(src_row, k, tt, slot):
        src = y_ref.at[pl.ds(pl.multiple_of(src_row * ROW_TILES, ROW_TILES), ROW_TILES)]
        return pltpu.make_async_copy(src, bufs[slot].at[k, pl.ds(tt * ROW_TILES, ROW_TILES)], sem.at[slot])

    def drain(slot):
        def body(tt, carry):
            row_copy(0, 0, 0, slot).wait()
            row_copy(0, 1, 0, slot).wait()
            return carry
        lax.fori_loop(0, TB, body, 0, unroll=8)

    @pl.when(i == 0)
    def _():
        def body(tt, carry):
            row_copy(pos_ref[2 * tt], 0, tt, 0).start()
            row_copy(pos_ref[2 * tt + 1], 1, tt, 0).start()
            return carry
        lax.fori_loop(0, TB, body, 0, unroll=8)

    def block(slot):
        drain(slot)
        base = jnp.minimum(i + 1, n - 1) * (2 * TB)
        for tt in range(TB):
            row_copy(pos_ref[base + 2 * tt], 0, tt, 1 - slot).start()
            row_copy(pos_ref[base + 2 * tt + 1], 1, tt, 1 - slot).start()
        info = info_ref[...]
        g0, g1 = info[:, 2:3], info[:, 3:4]
        for j in range(ROW_TILES):
            rows = pl.ds(j, TB, stride=ROW_TILES)
            ycat[:, j * LANES:(j + 1) * LANES] = bufs[slot][0, rows, :] * g0 + bufs[slot][1, rows, :] * g1
        x2 = _layer_norm(DEEPNORM_ALPHA * x_ref[...] + ycat[...], g_ref[...], b_ref[...])
        gate = jax.nn.sigmoid(_dot(x2.astype(BF16), wg_ref[...]) + bg_ref[...])
        o_ref[...] = x2 + gate * _dot(p_ref[...].astype(BF16), wp_ref[...])

    for parity in (0, 1):
        pl.when(i % 2 == parity)(functools.partial(block, parity))
    for parity in (0, 1):
        pl.when((i == n - 1) & (i % 2 == parity))(functools.partial(drain, 1 - parity))


def _combine_ple(pos_flat, y_rows, info, x, g, b, p, wg, bg, wp):
    t = x.shape[0]
    TB = min(COMBINE_TOKENS, t)
    row = lambda i, pos: (i, 0)
    fix = lambda i, pos: (0, 0)
    return pl.pallas_call(
        _combine_ple_kernel,
        grid_spec=pltpu.PrefetchScalarGridSpec(
            num_scalar_prefetch=1,
            grid=(t // TB,),
            in_specs=[
                pl.BlockSpec(memory_space=pl.ANY),
                pl.BlockSpec((TB, LANES), row),
                pl.BlockSpec((TB, D_MODEL), row),
                pl.BlockSpec((1, D_MODEL), fix), pl.BlockSpec((1, D_MODEL), fix),
                pl.BlockSpec((TB, PLE_DIM), row),
                pl.BlockSpec(wg.shape, fix),
                pl.BlockSpec((1, D_MODEL), fix),
                pl.BlockSpec(wp.shape, fix),
            ],
            out_specs=pl.BlockSpec((TB, D_MODEL), row),
            scratch_shapes=[
                pltpu.VMEM((2, TB * ROW_TILES, LANES), F32),
                pltpu.VMEM((2, TB * ROW_TILES, LANES), F32),
                pltpu.VMEM((TB, D_MODEL), F32),
                pltpu.SemaphoreType.DMA((2,)),
            ],
        ),
        out_shape=jax.ShapeDtypeStruct((t, D_MODEL), F32),
        compiler_params=_cparams(("arbitrary",)),
        name="moe_combine_ln_ple",
    )(pos_flat, y_rows, info, x, g, b, p, wg, bg, wp)


def _hier_moe_ln_ple(x, x_folded, wr, br, wg, wu, wd, layer, ln_g, ln_b, p, ple_wg, ple_bg, ple_wp):
    t = x.shape[0]
    info, cnt = _router(x, wr, br)
    counts = cnt[0, ROUTE_LANE0:ROUTE_LANE0 + MOE_EXPERTS].astype(I32)
    padded = (counts + EXPERT_ROWS - 1) // EXPERT_ROWS * EXPERT_ROWS
    pad_end = jnp.cumsum(padded)
    pad_start = jnp.zeros((1, LANES), F32).at[0, :MOE_EXPERTS].set((pad_end - padded).astype(F32))
    n_rows = 2 * t + MOE_EXPERTS * EXPERT_ROWS
    n_blocks = n_rows // EXPERT_ROWS
    blk_start = jnp.arange(n_blocks, dtype=I32) * EXPERT_ROWS
    blk_e = jnp.minimum(jnp.sum(pad_end[None, :] <= blk_start[:, None], axis=1), MOE_EXPERTS - 1).astype(I32)
    n_used = (pad_end[-1:] // EXPERT_ROWS).astype(I32)
    pos = _plan(info, pad_start)[:, :2].reshape(-1)
    y_rows = _experts(blk_e, n_used, _invert(pos, n_rows), x_folded, wg, wu, wd, layer)
    return _combine_ple(pos, y_rows, info, x, ln_g, ln_b, p, ple_wg, ple_bg, ple_wp)


def _rope_tables(positions):
    inv = ROPE_THETA ** (-jnp.arange(0, ROPE_DIM, 2, dtype=F32) / ROPE_DIM)
    ang = positions.astype(F32)[:, None] * inv
    cos, sin = jnp.cos(ang), jnp.sin(ang)
    cos_t = jnp.tile(cos, (1, LANES // (ROPE_DIM // 2)))
    sin_t = jnp.tile(jnp.concatenate([-sin, sin], axis=1), (1, LANES // ROPE_DIM))
    return cos_t, sin_t


def _pad_cols(w, n):
    return jnp.pad(w, ((0, 0), (0, n - w.shape[1])))


def _cat_bf16(parts, n):
    parts = [p.astype(BF16) for p in parts]
    used = sum(p.shape[1] for p in parts)
    return jnp.concatenate(parts + [jnp.zeros((parts[0].shape[0], n - used), BF16)], axis=1)


def _even_w_in(w):
    z, xbc, dt, q, kv = jnp.split(w, [2048, 5120, 5152, 6176], axis=1)
    return _cat_bf16([z, xbc, q, kv, dt], EV_NP)


def _odd_w_in(w):
    q, ckv, krope, qi, ki, wi = jnp.split(w, [3072, 3584, 3648, 4672, 4736], axis=1)
    q = q.reshape(D_MODEL, MLA_HEADS, MLA_NOPE + MLA_ROPE)
    qn = q[:, :, :MLA_NOPE].reshape(D_MODEL, -1)
    qr = q[:, :, MLA_NOPE:].reshape(D_MODEL, -1)
    return _cat_bf16([qn, qr, qi, ckv, krope, ki, wi], OD_NP)


def _head_expand_matrix():
    e = np.zeros((LANES, SSM_INNER), np.float32)
    for h in range(SSM_HEADS):
        e[h, h * SSM_HEAD_DIM:(h + 1) * SSM_HEAD_DIM] = 1.0
    return jnp.asarray(np.tile(e, (3, 1)), BF16)


def kernel(x, p, positions, ev_w_in, ev_conv_w, ev_conv_b, ev_dt_bias, ev_a_log, ev_d_skip, ev_ssm_norm, ev_sinks, ev_w_out, od_w_in, od_kv_norm, od_w_uk, od_w_uv, od_w_out, ln1_g, ln1_b, ln2_g, ln2_b, moe_router_group, moe_router_group_b, moe_router_expert, moe_router_expert_b, moe_w_gate, moe_w_up, moe_w_down, ple_w_proj, ple_w_gate, ple_b_gate):
    batch, s, d = x.shape
    assert batch == 1 and d == D_MODEL
    xs = x[0]
    cos_t, sin_t = _rope_tables(positions[0])
    e_mat = _head_expand_matrix()
    topk = min(IDX_TOPK_MAX, s // 4)
    row = lambda v: v.reshape(1, -1)
    pad_row = lambda v: _pad_cols(v.reshape(1, -1), LANES)
    for i in range(DEPTH):
        j = i // 2
        if i % 2 == 0:
            xp = _inproj(xs, _even_w_in(ev_w_in[j]))
            y_ssm = _ssd(xp, ev_conv_w[j], row(ev_conv_b[j]), pad_row(ev_dt_bias[j]), pad_row(ev_a_log[j]),
                         row(jnp.repeat(ev_d_skip[j], SSM_HEAD_DIM)), row(ev_ssm_norm[j]), e_mat)
            y_att = _swa(xp, ev_sinks[j], cos_t, sin_t)
            w_out = ev_w_out[j].astype(BF16)
            xs, xf = _even_out(y_ssm, y_att, w_out[:SSM_INNER], w_out[SSM_INNER:], xs, row(ln1_g[i]), row(ln1_b[i]))
        else:
            xp = _inproj(xs, _odd_w_in(od_w_in[j]))
            qlat, qrope, qidx, ckvn, kr, ki, wis = _dsa_prep(
                xp, od_w_uk[j].astype(BF16), row(od_kv_norm[j]), cos_t, sin_t)
            bias = _dsa_select(qidx, wis, ki, topk)
            olat = _dsa_attn(qlat, qrope, bias, ckvn, kr)
            xs, xf = _odd_out(olat, od_w_uv[j].astype(BF16), od_w_out[j].astype(BF16), xs, row(ln1_g[i]), row(ln1_b[i]))
        wr = _pad_cols(jnp.concatenate([moe_router_group[i], moe_router_expert[i]], axis=1), LANES)
        br = pad_row(jnp.concatenate([moe_router_group_b[i], moe_router_expert_b[i]]))
        xs = _hier_moe_ln_ple(xs, xf, wr, br, moe_w_gate, moe_w_up, moe_w_down, i, row(ln2_g[i]), row(ln2_b[i]),
                              p[i, 0], ple_w_gate[i].astype(BF16), row(ple_b_gate[i]), ple_w_proj[i].astype(BF16))
    return xs[None]
```

```python
import functools

import jax
import jax.numpy as jnp
import numpy as np
from jax import lax
from jax.experimental import pallas as pl
from jax.experimental.pallas import tpu as pltpu

F32 = jnp.float32
BF16 = jnp.bfloat16
I32 = jnp.int32

D_MODEL = 2048
DEPTH = 4
ROPE_THETA = 10000.0
ROPE_DIM = 64
NORM_EPS = 1e-5
SSM_HEADS = 32
SSM_HEAD_DIM = 64
SSM_INNER = SSM_HEADS * SSM_HEAD_DIM
SSM_GROUPS = 4
SSM_STATE = 128
SSM_CONV = 4
SSM_CHUNK = 128
SWA_Q_HEADS = 16
SWA_KV_HEADS = 2
ATTN_BLOCK = 128
MLA_HEADS = 16
MLA_NOPE = 128
MLA_ROPE = ROPE_DIM
MLA_V = 128
MLA_RANK = 512
MLA_SCALE = (MLA_NOPE + MLA_ROPE) ** -0.5
IDX_HEADS = 16
IDX_DIM = ROPE_DIM
IDX_TOPK_MAX = 256
MOE_GROUPS = 4
MOE_EPG = 8
MOE_EXPERTS = MOE_GROUPS * MOE_EPG
MOE_FF = 512
PLE_DIM = 256
DEEPNORM_ALPHA = (2 * DEPTH) ** 0.25

LANES = 128
SUBLANES = 8
V7X_VMEM_BYTES = 64 * 1024 * 1024
COMPILER_RESERVE_BYTES = 8 * 1024 * 1024
VMEM_LIMIT_BYTES = V7X_VMEM_BYTES - COMPILER_RESERVE_BYTES
ROW_TILES = D_MODEL // LANES

EXPERT_ROWS = 256
DSA_SEL_Q = 128
DSA_ATT_Q = 64
DSA_KC = 512
DSA_SEL_HEADS_PER_DOT = 4
MASK_NEG = -1e30

EV_Z, EV_XS, EV_BC, EV_Q, EV_KV, EV_DT = 0, 2048, 4096, 5120, 6144, 6400
EV_NP = 6656
OD_QN, OD_QR, OD_QI, OD_CKV, OD_KK, OD_WI = 0, 2048, 3072, 4096, 4608, 4736
OD_NP = 5120


def _cparams(sem, vmem=VMEM_LIMIT_BYTES):
    return pltpu.CompilerParams(dimension_semantics=sem, vmem_limit_bytes=vmem)


def _dot(a, b):
    return jnp.dot(a, b, preferred_element_type=F32)


def _dot_nt(a, b):
    return lax.dot_general(a, b, (((1,), (1,)), ((), ())), preferred_element_type=F32)


def _split3(v):
    hi = v.astype(BF16)
    r = v - hi.astype(F32)
    mid = r.astype(BF16)
    lo = (r - mid.astype(F32)).astype(BF16)
    return hi, mid, lo


def _expand(v, e3):
    return _dot(jnp.concatenate(_split3(v), axis=1), e3)


def _silu(v):
    return v * jax.nn.sigmoid(v)


def _layer_norm(v, g, b):
    mu = jnp.mean(v, axis=-1, keepdims=True)
    vc = v - mu
    var = jnp.mean(vc * vc, axis=-1, keepdims=True)
    return vc * lax.rsqrt(var + NORM_EPS) * g + b


def _rope_tile(t, c, s):
    lane = lax.broadcasted_iota(I32, t.shape, 1)
    first_half = (lane & 32) == 0
    swapped = jnp.where(first_half, pltpu.roll(t, LANES - 32, 1), pltpu.roll(t, 32, 1))
    return t * c + swapped * s


def _inproj_kernel(x_ref, w_ref, o_ref):
    o_ref[...] = _dot(x_ref[...].astype(BF16), w_ref[...])


def _inproj(x, w):
    m, k = x.shape
    n = w.shape[1]
    tm, tn = min(1024, m), 512
    return pl.pallas_call(
        _inproj_kernel,
        grid=(m // tm, n // tn),
        in_specs=[pl.BlockSpec((tm, k), lambda i, j: (i, 0)), pl.BlockSpec((k, tn), lambda i, j: (0, j))],
        out_specs=pl.BlockSpec((tm, tn), lambda i, j: (i, j)),
        out_shape=jax.ShapeDtypeStruct((m, n), F32),
        compiler_params=_cparams(("parallel", "arbitrary")),
        name="inproj",
    )(x, w)


def _store_row_layouts(y, o_ref, of_ref):
    o_ref[...] = y
    for j in range(ROW_TILES):
        of_ref[pl.ds(j, y.shape[0], stride=ROW_TILES), :] = y[:, j * LANES:(j + 1) * LANES]


def _row_layout_outputs(m, tm):
    specs = [pl.BlockSpec((tm, D_MODEL), lambda i: (i, 0)), pl.BlockSpec((tm * ROW_TILES, LANES), lambda i: (i, 0))]
    shapes = [jax.ShapeDtypeStruct((m, D_MODEL), F32), jax.ShapeDtypeStruct((m * ROW_TILES, LANES), F32)]
    return specs, shapes


def _even_out_kernel(a1_ref, a2_ref, w1_ref, w2_ref, x_ref, g_ref, b_ref, o_ref, of_ref):
    mix = _dot(a1_ref[...], w1_ref[...]) + _dot(a2_ref[...], w2_ref[...])
    _store_row_layouts(_layer_norm(DEEPNORM_ALPHA * x_ref[...] + mix, g_ref[...], b_ref[...]), o_ref, of_ref)


def _even_out(y_ssm, y_att, w1, w2, x, g, b):
    m = x.shape[0]
    tm = min(256, m)
    row = lambda i: (i, 0)
    fix = lambda i: (0, 0)
    out_specs, out_shape = _row_layout_outputs(m, tm)
    return pl.pallas_call(
        _even_out_kernel,
        grid=(m // tm,),
        in_specs=[
            pl.BlockSpec((tm, y_ssm.shape[1]), row),
            pl.BlockSpec((tm, y_att.shape[1]), row),
            pl.BlockSpec(w1.shape, fix),
            pl.BlockSpec(w2.shape, fix),
            pl.BlockSpec((tm, D_MODEL), row),
            pl.BlockSpec((1, D_MODEL), fix),
            pl.BlockSpec((1, D_MODEL), fix),
        ],
        out_specs=out_specs,
        out_shape=out_shape,
        compiler_params=_cparams(("parallel",)),
        name="even_out_ln",
    )(y_ssm, y_att, w1, w2, x, g, b)


def _ssd_kernel(z_ref, xs_ref, bc_ref, dt_ref, cwx_ref, cbx_ref, cwb_ref, cbb_ref, dtb_ref, alog_ref,
                dsk_ref, nrm_ref, e_ref, y_ref, xs_ext, bc_ext, st_ref):
    c = pl.program_id(0)
    L = SSM_CHUNK
    halo = SUBLANES

    @pl.when(c == 0)
    def _():
        xs_ext[0:halo, :] = jnp.zeros((halo, xs_ext.shape[1]), F32)
        bc_ext[0:halo, :] = jnp.zeros((halo, bc_ext.shape[1]), F32)
        st_ref[...] = jnp.zeros(st_ref.shape, F32)

    xs_ext[halo:halo + L, :] = xs_ref[...]
    bc_ext[halo:halo + L, :] = bc_ref[...]

    def conv(ext, w_ref, b_ref):
        acc = b_ref[...]
        for j in range(SSM_CONV):
            lo = halo - (SSM_CONV - 1) + j
            acc = acc + ext[lo:lo + L, :] * w_ref[j:j + 1, :]
        return acc

    xs = _silu(conv(xs_ext, cwx_ref, cbx_ref))
    bc = _silu(conv(bc_ext, cwb_ref, cbb_ref))
    xs_ext[0:halo, :] = xs_ext[L:L + halo, :]
    bc_ext[0:halo, :] = bc_ext[L:L + halo, :]

    pre = dt_ref[...] + dtb_ref[...]
    dt = jnp.maximum(pre, 0.0) + jnp.log1p(jnp.exp(-jnp.abs(pre)))
    a = dt * (-jnp.exp(alog_ref[...]))
    row = lax.broadcasted_iota(I32, (L, LANES), 0)
    acs = a
    s = 1
    while s < L:
        acs = acs + jnp.where(row >= s, pltpu.roll(acs, s, 0), 0.0)
        s *= 2
    a_last = acs[L - 1:L, :]
    e = e_ref[...]
    dt_x = _expand(dt, e)
    dte_x = _expand(dt * jnp.exp(a_last - acs), e)
    eacs_x = _expand(jnp.exp(acs), e)
    cd_x = _expand(jnp.broadcast_to(jnp.exp(a_last), (SUBLANES, LANES)), e)[0:1, :]
    acs_t = acs.T

    xdt = (xs * dt_x).astype(BF16)
    xd = (xs * dte_x).astype(BF16)
    tri = lax.broadcasted_iota(I32, (L, L), 0) >= lax.broadcasted_iota(I32, (L, L), 1)
    first_head = lax.broadcasted_iota(I32, (L, LANES), 1) < SSM_HEAD_DIM
    n_state = SSM_STATE
    gw = SSM_INNER // SSM_GROUPS
    ys = []
    for g in range(SSM_GROUPS):
        bg = bc[:, g * n_state:(g + 1) * n_state]
        cg = bc[:, SSM_GROUPS * n_state + g * n_state:SSM_GROUPS * n_state + (g + 1) * n_state]
        bb, cb16 = bg.astype(BF16), cg.astype(BF16)
        cbm = _dot_nt(cb16, bb)
        st = st_ref[g]
        y_off = _dot(cb16, st.astype(BF16)) * eacs_x[:, g * gw:(g + 1) * gw]
        st_ref[g] = st * cd_x[:, g * gw:(g + 1) * gw] + _dot(bg.T.astype(BF16), xd[:, g * gw:(g + 1) * gw])
        parts = []
        for j in range(gw // LANES):
            lo = g * gw + j * LANES
            xp = xdt[:, lo:lo + LANES]
            out = None
            for par in (0, 1):
                h = lo // SSM_HEAD_DIM + par
                seg = acs[:, h:h + 1] - acs_t[h:h + 1, :]
                lm = (jnp.exp(jnp.where(tri, seg, -jnp.inf)) * cbm).astype(BF16)
                xm = jnp.where(first_head if par == 0 else jnp.logical_not(first_head), xp, jnp.zeros_like(xp))
                d = _dot(lm, xm)
                out = d if out is None else out + d
            parts.append(out)
        ys.append(jnp.concatenate(parts, axis=1) + y_off)
    y = jnp.concatenate(ys, axis=1) + xs * dsk_ref[...]
    y = y * _silu(z_ref[...])
    outs = []
    for g in range(SSM_GROUPS):
        yg = y[:, g * gw:(g + 1) * gw]
        ms = jnp.mean(yg * yg, axis=-1, keepdims=True)
        outs.append(yg * lax.rsqrt(ms + NORM_EPS))
    y_ref[...] = (jnp.concatenate(outs, axis=1) * nrm_ref[...]).astype(BF16)


def _ssd(xp, cw, cb, dtb, alog, dsk, nrm, e):
    s = xp.shape[0]
    L = SSM_CHUNK
    bcw = 2 * SSM_GROUPS * SSM_STATE
    cwx, cwb = cw[:, :SSM_INNER], cw[:, SSM_INNER:]
    cbx, cbb = cb[:, :SSM_INNER], cb[:, SSM_INNER:]
    fix = lambda i: (0, 0)
    return pl.pallas_call(
        _ssd_kernel,
        grid=(s // L,),
        in_specs=[
            pl.BlockSpec((L, SSM_INNER), lambda i: (i, EV_Z // SSM_INNER)),
            pl.BlockSpec((L, SSM_INNER), lambda i: (i, EV_XS // SSM_INNER)),
            pl.BlockSpec((L, bcw), lambda i: (i, EV_BC // bcw)),
            pl.BlockSpec((L, LANES), lambda i: (i, EV_DT // LANES)),
            pl.BlockSpec(cwx.shape, fix), pl.BlockSpec(cbx.shape, fix),
            pl.BlockSpec(cwb.shape, fix), pl.BlockSpec(cbb.shape, fix),
            pl.BlockSpec((1, LANES), fix), pl.BlockSpec((1, LANES), fix),
            pl.BlockSpec((1, SSM_INNER), fix), pl.BlockSpec((1, SSM_INNER), fix),
            pl.BlockSpec(e.shape, fix),
        ],
        out_specs=pl.BlockSpec((L, SSM_INNER), lambda i: (i, 0)),
        out_shape=jax.ShapeDtypeStruct((s, SSM_INNER), BF16),
        scratch_shapes=[
            pltpu.VMEM((L + 2 * SUBLANES, SSM_INNER), F32),
            pltpu.VMEM((L + 2 * SUBLANES, bcw), F32),
            pltpu.VMEM((SSM_GROUPS, SSM_STATE, SSM_INNER // SSM_GROUPS), F32),
        ],
        compiler_params=_cparams(("arbitrary",)),
        name="ssd_scan",
    )(xp, xp, xp, xp, cwx, cbx, cwb, cbb, dtb, alog, dsk, nrm, e)


def _swa_kernel(sink_ref, q_ref, kvc_ref, kvp_ref, cq_ref, sq_ref, cp_ref, sp_ref, o_ref):
    i = pl.program_id(0)
    B = ATTN_BLOCK
    lane = lax.broadcasted_iota(I32, (B, LANES), 1)
    lo_half = lane < ROPE_DIM
    cq, sq = cq_ref[...], sq_ref[...]
    kc = _rope_tile(kvc_ref[:, 0:LANES], cq, sq)
    kp = _rope_tile(kvp_ref[:, 0:LANES], cp_ref[...], sp_ref[...])
    kcat = jnp.concatenate([kp, kc], axis=0)
    kmat = (kcat.astype(BF16), pltpu.roll(kcat, ROPE_DIM, 1).astype(BF16))
    vcat = jnp.concatenate([kvp_ref[:, LANES:2 * LANES], kvc_ref[:, LANES:2 * LANES]], axis=0)
    vrol = pltpu.roll(vcat, ROPE_DIM, 1)
    lane2 = lax.broadcasted_iota(I32, (2 * B, LANES), 1) < ROPE_DIM
    vdup = (jnp.where(lane2, vcat, vrol).astype(BF16), jnp.where(lane2, vrol, vcat).astype(BF16))
    r = lax.broadcasted_iota(I32, (B, 2 * B), 0)
    col = lax.broadcasted_iota(I32, (B, 2 * B), 1)
    mask = (col > r) & (col <= r + B) & ((i > 0) | (col >= B))
    scale = ROPE_DIM ** -0.5
    hpg = SWA_Q_HEADS // SWA_KV_HEADS
    for j in range(SWA_Q_HEADS // 2):
        g = (2 * j) // hpg
        qt = _rope_tile(q_ref[:, j * LANES:(j + 1) * LANES], cq, sq)
        outs = []
        for par in (0, 1):
            h = 2 * j + par
            qm = jnp.where(lo_half if par == 0 else jnp.logical_not(lo_half), qt, 0.0).astype(BF16)
            logit = _dot_nt(qm, kmat[0] if par == g else kmat[1]) * scale
            logit = jnp.where(mask, logit, -jnp.inf)
            sink = sink_ref[h]
            m = jnp.maximum(jnp.max(logit, axis=-1, keepdims=True), sink)
            ex = jnp.exp(logit - m)
            prob = ex / (jnp.sum(ex, axis=-1, keepdims=True) + jnp.exp(sink - m))
            outs.append(_dot(prob.astype(BF16), vdup[g]))
        o_ref[:, j * LANES:(j + 1) * LANES] = jnp.where(lo_half, outs[0], outs[1]).astype(BF16)


def _swa(xp, sinks, cos_t, sin_t):
    s = xp.shape[0]
    B = ATTN_BLOCK
    qw = SWA_Q_HEADS * ROPE_DIM
    kvw = 2 * SWA_KV_HEADS * ROPE_DIM
    prev = lambda i: (jnp.maximum(i - 1, 0), 0)
    cur = lambda i: (i, 0)
    return pl.pallas_call(
        _swa_kernel,
        grid=(s // B,),
        in_specs=[
            pl.BlockSpec(memory_space=pltpu.SMEM),
            pl.BlockSpec((B, qw), lambda i: (i, EV_Q // qw)),
            pl.BlockSpec((B, kvw), lambda i: (i, EV_KV // kvw)),
            pl.BlockSpec((B, kvw), lambda i: (jnp.maximum(i - 1, 0), EV_KV // kvw)),
            pl.BlockSpec((B, LANES), cur), pl.BlockSpec((B, LANES), cur),
            pl.BlockSpec((B, LANES), prev), pl.BlockSpec((B, LANES), prev),
        ],
        out_specs=pl.BlockSpec((B, qw), cur),
        out_shape=jax.ShapeDtypeStruct((s, qw), BF16),
        compiler_params=_cparams(("parallel",)),
        name="swa_sink",
    )(sinks, xp, xp, xp, cos_t, sin_t, cos_t, sin_t)


def _dsa_prep_kernel(qn_ref, qr_ref, qi_ref, ckv_ref, kk_ref, wi_ref, wuk_ref, kvn_ref, c_ref, s_ref,
                     qlat_ref, qrope_ref, qidx_ref, ckvn_ref, kr_ref, ki_ref, wis_ref):
    c, s = c_ref[...], s_ref[...]
    lane = lax.broadcasted_iota(I32, c.shape, 1)
    lo_half = lane < ROPE_DIM
    for h in range(MLA_HEADS):
        qn = qn_ref[:, h * MLA_NOPE:(h + 1) * MLA_NOPE].astype(BF16)
        qlat_ref[h] = _dot(qn, wuk_ref[h]).astype(BF16)
    for src, dst in ((qr_ref, qrope_ref), (qi_ref, qidx_ref)):
        for j in range(MLA_HEADS // 2):
            t = _rope_tile(src[:, j * LANES:(j + 1) * LANES], c, s)
            dst[2 * j] = jnp.where(lo_half, t, 0.0).astype(BF16)
            dst[2 * j + 1] = jnp.where(lo_half, pltpu.roll(t, ROPE_DIM, 1), 0.0).astype(BF16)
    kk = _rope_tile(kk_ref[...], c, s)
    kr_ref[...] = jnp.where(lo_half, kk, 0.0).astype(BF16)
    ki_ref[...] = jnp.where(lo_half, pltpu.roll(kk, ROPE_DIM, 1), 0.0).astype(BF16)
    ckv = ckv_ref[...]
    ms = jnp.mean(ckv * ckv, axis=-1, keepdims=True)
    ckvn_ref[...] = (ckv * lax.rsqrt(ms + NORM_EPS) * kvn_ref[...]).astype(BF16)
    wis_ref[...] = wi_ref[...] * (IDX_HEADS ** -0.5 * IDX_DIM ** -0.5)


def _dsa_prep(xp, wuk, kvn, cos_t, sin_t):
    s = xp.shape[0]
    tm = min(256, s)
    H = MLA_HEADS
    fix2 = lambda i: (0, 0)
    hrow = lambda i: (0, i, 0)
    row = lambda i: (i, 0)
    return pl.pallas_call(
        _dsa_prep_kernel,
        grid=(s // tm,),
        in_specs=[
            pl.BlockSpec((tm, 2048), lambda i: (i, OD_QN // 2048)),
            pl.BlockSpec((tm, 1024), lambda i: (i, OD_QR // 1024)),
            pl.BlockSpec((tm, 1024), lambda i: (i, OD_QI // 1024)),
            pl.BlockSpec((tm, MLA_RANK), lambda i: (i, OD_CKV // MLA_RANK)),
            pl.BlockSpec((tm, LANES), lambda i: (i, OD_KK // LANES)),
            pl.BlockSpec((tm, LANES), lambda i: (i, OD_WI // LANES)),
            pl.BlockSpec(wuk.shape, lambda i: (0, 0, 0)),
            pl.BlockSpec((1, MLA_RANK), fix2),
            pl.BlockSpec((tm, LANES), row), pl.BlockSpec((tm, LANES), row),
        ],
        out_specs=[
            pl.BlockSpec((H, tm, MLA_RANK), hrow),
            pl.BlockSpec((H, tm, LANES), hrow),
            pl.BlockSpec((H, tm, LANES), hrow),
            pl.BlockSpec((tm, MLA_RANK), row),
            pl.BlockSpec((tm, LANES), row),
            pl.BlockSpec((tm, LANES), row),
            pl.BlockSpec((tm, LANES), row),
        ],
        out_shape=[
            jax.ShapeDtypeStruct((H, s, MLA_RANK), BF16),
            jax.ShapeDtypeStruct((H, s, LANES), BF16),
            jax.ShapeDtypeStruct((H, s, LANES), BF16),
            jax.ShapeDtypeStruct((s, MLA_RANK), BF16),
            jax.ShapeDtypeStruct((s, LANES), BF16),
            jax.ShapeDtypeStruct((s, LANES), BF16),
            jax.ShapeDtypeStruct((s, LANES), F32),
        ],
        compiler_params=_cparams(("parallel",)),
        name="dsa_prep",
    )(xp, xp, xp, xp, xp, xp, wuk, kvn, cos_t, sin_t)


def _dsa_select_kernel(topk, qi_ref, wi_ref, ki_ref, bias_ref, key_ref, dig_ref):
    i = pl.program_id(0)
    Q, KC = DSA_SEL_Q, DSA_KC
    H = IDX_HEADS
    n_chunks = bias_ref.shape[0]
    n_vis = ((i + 1) * Q + KC - 1) // KC
    wi = wi_ref[...]
    qpos = i * Q + lax.broadcasted_iota(I32, (KC, Q), 1)
    kloc = lax.broadcasted_iota(I32, (KC, Q), 0)
    HG = DSA_SEL_HEADS_PER_DOT
    N_ACC = 4
    DIGIT_BITS = (11, 11, 10)
    DIGIT_SHIFT = (21, 10, 0)
    TOP_BIAS = 1 << (DIGIT_BITS[0] - 1)
    GUARD = jnp.int32(-0x7FFF8000)
    FIELD_ONES = jnp.int32(0x00010001)
    HK = KC // 2

    def pack(lo, hi):
        return lo | (hi << 16) | GUARD

    def score_chunk(c, carry):
        k = ki_ref[pl.ds(pl.multiple_of(c * KC, KC), KC), :]
        acc = jnp.zeros((KC, Q), F32)
        for g in range(H // HG):
            sc = _dot_nt(k, qi_ref[g * HG:(g + 1) * HG].reshape(HG * Q, LANES))
            for hh in range(HG):
                h = g * HG + hh
                acc = acc + jnp.maximum(sc[:, hh * Q:(hh + 1) * Q], 0.0) * wi[h:h + 1, :]
        acc = jnp.where(c * KC + kloc <= qpos, acc, -jnp.inf)
        bits = pltpu.bitcast(acc, I32)
        key = bits ^ ((bits >> 31) & jnp.int32(0x7FFFFFFF))
        key_ref[c] = key
        for d in range(len(DIGIT_BITS)):
            dig = (key >> DIGIT_SHIFT[d]) + TOP_BIAS if d == 0 else (key >> DIGIT_SHIFT[d]) & ((1 << DIGIT_BITS[d]) - 1)
            dig_ref[d, c] = pack(dig[:HK], dig[HK:]) + FIELD_ONES
        return carry

    lax.fori_loop(0, n_vis, score_chunk, 0)

    def count_ge(d, cand):
        cand2 = (cand + 1) | ((cand + 1) << 16)

        def body(c, accs):
            v = dig_ref[d, c]
            accs = list(accs)
            for r in range(HK // SUBLANES):
                w = v[r * SUBLANES:(r + 1) * SUBLANES, :] - cand2
                accs[r % N_ACC] = accs[r % N_ACC] + (lax.shift_right_logical(w, 15) & FIELD_ONES)
            return tuple(accs)
        accs = lax.fori_loop(0, n_vis, body, tuple(jnp.zeros((SUBLANES, Q), I32) for _ in range(N_ACC)))
        acc = sum(accs)
        cnt = (acc & 0xFFFF) + lax.shift_right_logical(acc, 16)
        return jnp.sum(cnt.astype(F32), axis=0, keepdims=True)

    def search(d, need):
        def bit_step(b, thr):
            cand = thr | (jnp.int32(1) << (DIGIT_BITS[d] - 1 - b))
            return jnp.where(count_ge(d, cand) >= need, cand, thr)

        return lax.fori_loop(0, DIGIT_BITS[d], bit_step, jnp.zeros((1, Q), I32))

    def drop_unless_equal(d, p):
        def body(c, carry):
            cur, nxt = dig_ref[d, c], dig_ref[d + 1, c]
            lo = jnp.where((cur & 0x7FFF) == p + 1, nxt & 0x7FFF, 0)
            hi = jnp.where((lax.shift_right_logical(cur, 16) & 0x7FFF) == p + 1, lax.shift_right_logical(nxt, 16) & 0x7FFF, 0)
            dig_ref[d + 1, c] = pack(lo, hi)
            return carry

        lax.fori_loop(0, n_vis, body, 0)

    need = jnp.full((1, Q), float(topk), F32)
    thr = jnp.zeros((1, Q), I32)
    for d in range(len(DIGIT_BITS)):
        p = search(d, need)
        thr = thr + ((p - (TOP_BIAS if d == 0 else 0)) << DIGIT_SHIFT[d])
        if d + 1 < len(DIGIT_BITS):
            need = need - count_ge(d, p + 1)
            drop_unless_equal(d, p)

    def emit(c, cnt):
        sel = (key_ref[c] >= thr) & (c * KC + kloc <= qpos)
        bias_ref[c] = jnp.where(sel, 0.0, MASK_NEG).T
        return cnt + jnp.sum(jnp.where(sel, 1.0, 0.0), axis=0, keepdims=True)

    n_sel = lax.fori_loop(0, n_vis, emit, jnp.zeros((1, Q), F32))

    @pl.when(jnp.max(n_sel) > topk)
    def _():
        def count(pred):
            def body(c, cnt):
                return cnt + jnp.sum(jnp.where(pred(c), 1.0, 0.0), axis=0, keepdims=True)
            return lax.fori_loop(0, n_vis, body, jnp.zeros((1, Q), F32))

        def tied(c):
            return (key_ref[c] == thr) & (c * KC + kloc <= qpos)

        need_tied = topk - count(lambda c: key_ref[c] > thr)

        def bit_step(b, last):
            cand = last | (jnp.int32(1) << (index_bits - 1 - b))
            in_front = count(lambda c: tied(c) & (c * KC + kloc < cand))
            return jnp.where(in_front < need_tied, cand, last)

        index_bits = (n_chunks * KC - 1).bit_length()
        last = lax.fori_loop(0, index_bits, bit_step, jnp.zeros((1, Q), I32))

        def emit_ties(c, carry):
            sel = (key_ref[c] > thr) | (tied(c) & (c * KC + kloc <= last))
            bias_ref[c] = jnp.where(sel, 0.0, MASK_NEG).T
            return carry

        lax.fori_loop(0, n_vis, emit_ties, 0)

    def fill(c, carry):
        bias_ref[c] = jnp.full((Q, KC), MASK_NEG, F32)
        return carry

    lax.fori_loop(n_vis, n_chunks, fill, 0)


def _dsa_select(qidx, wis, ki, topk):
    H, s, _ = qidx.shape
    Q, KC = DSA_SEL_Q, DSA_KC
    assert Q == LANES
    nch = s // KC
    return pl.pallas_call(
        functools.partial(_dsa_select_kernel, topk),
        grid=(s // Q,),
        in_specs=[
            pl.BlockSpec((H, Q, LANES), lambda i: (0, i, 0)),
            pl.BlockSpec((LANES, Q), lambda i: (0, i)),
            pl.BlockSpec((s, LANES), lambda i: (0, 0)),
        ],
        out_specs=pl.BlockSpec((nch, Q, KC), lambda i: (0, i, 0)),
        out_shape=jax.ShapeDtypeStruct((nch, s, KC), F32),
        scratch_shapes=[pltpu.VMEM((nch, KC, Q), I32), pltpu.VMEM((3, nch, KC // 2, Q), I32)],
        compiler_params=_cparams(("parallel",)),
        name="dsa_select",
    )(qidx, wis.T, ki)


def _dsa_attn_kernel(ql_ref, qr_ref, bias_ref, ckv_ref, kr_ref, o_ref, m_ref, l_ref, acc_ref, s_ref):
    i = pl.program_id(0)
    Q, KC, H = DSA_ATT_Q, DSA_KC, MLA_HEADS
    n_vis = ((i + 1) * Q + KC - 1) // KC
    m_ref[...] = jnp.full(m_ref.shape, -jnp.inf, F32)
    l_ref[...] = jnp.zeros(l_ref.shape, F32)
    acc_ref[...] = jnp.zeros(acc_ref.shape, F32)

    def keys(c):
        off = pl.multiple_of(c * KC, KC)
        return ckv_ref[pl.ds(off, KC), :], kr_ref[pl.ds(off, KC), :]

    def raw_logits(c):
        ck, kr = keys(c)
        ql = ql_ref[...].reshape(H * Q, MLA_RANK)
        qr = qr_ref[...].reshape(H * Q, LANES)
        return _dot_nt(ql, ck) + _dot_nt(qr, kr)

    def consume(c, slot):
        logit = s_ref[slot] * MLA_SCALE
        logit = (logit.reshape(H, Q, KC) + bias_ref[c][None]).reshape(H * Q, KC)
        m_old = m_ref[...]
        m_new = jnp.maximum(m_old, jnp.max(logit, axis=-1, keepdims=True))
        alpha = jnp.exp(m_old - m_new)
        p = jnp.exp(logit - m_new)
        l_ref[...] = alpha * l_ref[...] + jnp.sum(p, axis=-1, keepdims=True)
        acc_ref[...] = alpha * acc_ref[...] + _dot(p.astype(BF16), keys(c)[0])
        m_ref[...] = m_new

    s_ref[0] = raw_logits(0)

    def pair(j, carry):
        c = 2 * j
        s_ref[1] = raw_logits(c + 1)
        consume(c, 0)
        s_ref[0] = raw_logits(c + 2)
        consume(c + 1, 1)
        return carry

    n_pairs = (n_vis - 1) // 2
    lax.fori_loop(0, n_pairs, pair, 0)
    last = 2 * n_pairs

    @pl.when(last + 1 < n_vis)
    def _():
        s_ref[1] = raw_logits(last + 1)
        consume(last, 0)
        consume(last + 1, 1)

    @pl.when(last + 1 == n_vis)
    def _():
        consume(last, 0)

    o_ref[...] = (acc_ref[...] / l_ref[...]).reshape(H, Q, MLA_RANK).astype(BF16)


def _dsa_attn(qlat, qrope, bias, ckvn, kr):
    H, s, _ = qlat.shape
    Q, KC = DSA_ATT_Q, DSA_KC
    nch = s // KC
    hrow = lambda i: (0, i, 0)
    fix = lambda i: (0, 0)
    return pl.pallas_call(
        _dsa_attn_kernel,
        grid=(s // Q,),
        in_specs=[
            pl.BlockSpec((H, Q, MLA_RANK), hrow),
            pl.BlockSpec((H, Q, LANES), hrow),
            pl.BlockSpec((nch, Q, KC), hrow),
            pl.BlockSpec((s, MLA_RANK), fix),
            pl.BlockSpec((s, LANES), fix),
        ],
        out_specs=pl.BlockSpec((H, Q, MLA_RANK), hrow),
        out_shape=jax.ShapeDtypeStruct((H, s, MLA_RANK), BF16),
        scratch_shapes=[
            pltpu.VMEM((H * Q, 1), F32),
            pltpu.VMEM((H * Q, 1), F32),
            pltpu.VMEM((H * Q, MLA_RANK), F32),
            pltpu.VMEM((2, H * Q, KC), F32),
        ],
        compiler_params=_cparams(("parallel",)),
        name="dsa_attn",
    )(qlat, qrope, bias, ckvn, kr)


def _odd_out_kernel(ol_ref, wuv_ref, wo_ref, x_ref, g_ref, b_ref, o_ref, of_ref, u_ref):
    for h in range(MLA_HEADS):
        u_ref[:, h * MLA_V:(h + 1) * MLA_V] = _dot(ol_ref[h], wuv_ref[h]).astype(BF16)
    mix = _dot(u_ref[...], wo_ref[...])
    _store_row_layouts(_layer_norm(DEEPNORM_ALPHA * x_ref[...] + mix, g_ref[...], b_ref[...]), o_ref, of_ref)


def _odd_out(olat, wuv, wo, x, g, b):
    H, s, _ = olat.shape
    tm = min(256, s)
    row = lambda i: (i, 0)
    fix = lambda i: (0, 0)
    out_specs, out_shape = _row_layout_outputs(s, tm)
    return pl.pallas_call(
        _odd_out_kernel,
        grid=(s // tm,),
        in_specs=[
            pl.BlockSpec((H, tm, MLA_RANK), lambda i: (0, i, 0)),
            pl.BlockSpec(wuv.shape, lambda i: (0, 0, 0)),
            pl.BlockSpec(wo.shape, fix),
            pl.BlockSpec((tm, D_MODEL), row),
            pl.BlockSpec((1, D_MODEL), fix), pl.BlockSpec((1, D_MODEL), fix),
        ],
        out_specs=out_specs,
        out_shape=out_shape,
        scratch_shapes=[pltpu.VMEM((tm, MLA_HEADS * MLA_V), BF16)],
        compiler_params=_cparams(("parallel",)),
        name="odd_out_ln",
    )(olat, wuv, wo, x, g, b)


ROUTE_LANE0 = MOE_GROUPS


def _router_kernel(h_ref, wr_ref, br_ref, info_ref, cnt_ref, run_ref):
    i = pl.program_id(0)

    @pl.when(i == 0)
    def _():
        run_ref[...] = jnp.zeros(run_ref.shape, F32)

    h = h_ref[...]
    tm = h.shape[0]
    h_hi = h.astype(BF16)
    h_lo = (h - h_hi.astype(F32)).astype(BF16)
    w = wr_ref[...]
    w_hi = w.astype(BF16)
    w_lo = (w - w_hi.astype(F32)).astype(BF16)
    logits = _dot(h_hi, w_hi) + (_dot(h_hi, w_lo) + _dot(h_lo, w_hi)) + br_ref[...]
    lane = lax.broadcasted_iota(I32, (tm, LANES), 1)
    lane_f = lane.astype(F32)
    neg = -jnp.inf
    big = float(LANES)
    is_grp = lane < MOE_GROUPS
    gl = jnp.where(is_grp, logits, neg)
    gmax = jnp.max(gl, axis=-1, keepdims=True)
    gsel = jnp.min(jnp.where(gl == gmax, lane_f, big), axis=-1, keepdims=True)
    gsum = jnp.sum(jnp.where(is_grp, jnp.exp(logits - gmax), 0.0), axis=-1, keepdims=True)
    egrp = ((lane - ROUTE_LANE0) >> 3).astype(F32)
    valid = (lane >= ROUTE_LANE0) & (lane < ROUTE_LANE0 + MOE_EXPERTS) & (egrp == gsel)
    el = jnp.where(valid, logits, neg)
    v1 = jnp.max(el, axis=-1, keepdims=True)
    i1 = jnp.min(jnp.where(el == v1, lane_f, big), axis=-1, keepdims=True)
    el2 = jnp.where(lane_f == i1, neg, el)
    v2 = jnp.max(el2, axis=-1, keepdims=True)
    i2 = jnp.min(jnp.where(el2 == v2, lane_f, big), axis=-1, keepdims=True)
    t = jnp.exp(v2 - v1)
    p1 = 1.0 / (1.0 + t)
    p2 = t / (1.0 + t)
    ggate = 1.0 / gsum
    m1 = lane_f == i1
    m2 = lane_f == i2
    memb = jnp.where(m1 | m2, 1.0, 0.0)
    tri = (lax.broadcasted_iota(I32, (tm, tm), 0) > lax.broadcasted_iota(I32, (tm, tm), 1))
    cum = _dot(jnp.where(tri, 1.0, 0.0).astype(BF16), memb.astype(BF16)) + run_ref[...]
    rank1 = jnp.sum(jnp.where(m1, cum, 0.0), axis=-1, keepdims=True)
    rank2 = jnp.sum(jnp.where(m2, cum, 0.0), axis=-1, keepdims=True)
    run_ref[...] = run_ref[...] + jnp.sum(memb, axis=0, keepdims=True)
    info = jnp.where(lane == 0, i1 - ROUTE_LANE0, 0.0)
    info = jnp.where(lane == 1, i2 - ROUTE_LANE0, info)
    info = jnp.where(lane == 2, p1 * ggate, info)
    info = jnp.where(lane == 3, p2 * ggate, info)
    info = jnp.where(lane == 4, rank1, info)
    info = jnp.where(lane == 5, rank2, info)
    info_ref[...] = info
    cnt_ref[...] = run_ref[...]


def _router(h, wr, br):
    t = h.shape[0]
    tm = min(512, t)
    return pl.pallas_call(
        _router_kernel,
        grid=(t // tm,),
        in_specs=[
            pl.BlockSpec((tm, D_MODEL), lambda i: (i, 0)),
            pl.BlockSpec(wr.shape, lambda i: (0, 0)),
            pl.BlockSpec((1, LANES), lambda i: (0, 0)),
        ],
        out_specs=[pl.BlockSpec((tm, LANES), lambda i: (i, 0)), pl.BlockSpec((1, LANES), lambda i: (0, 0))],
        out_shape=[jax.ShapeDtypeStruct((t, LANES), F32), jax.ShapeDtypeStruct((1, LANES), F32)],
        scratch_shapes=[pltpu.VMEM((1, LANES), F32)],
        compiler_params=_cparams(("arbitrary",)),
        name="moe_router",
    )(h, wr, br)


def _plan_kernel(info_ref, ps_ref, pos_ref):
    info = info_ref[...]
    lane = lax.broadcasted_iota(I32, info.shape, 1)
    lane_f = lane.astype(F32)
    ps = ps_ref[...]
    pos1 = jnp.sum(jnp.where(lane_f == info[:, 0:1], ps, 0.0), axis=-1, keepdims=True) + info[:, 4:5]
    pos2 = jnp.sum(jnp.where(lane_f == info[:, 1:2], ps, 0.0), axis=-1, keepdims=True) + info[:, 5:6]
    pos_ref[...] = jnp.where(lane == 0, pos1, jnp.where(lane == 1, pos2, 0.0)).astype(I32)


def _plan(info, pad_start):
    t = info.shape[0]
    tm = min(1024, t)
    return pl.pallas_call(
        _plan_kernel,
        grid=(t // tm,),
        in_specs=[pl.BlockSpec((tm, LANES), lambda i: (i, 0)), pl.BlockSpec((1, LANES), lambda i: (0, 0))],
        out_specs=pl.BlockSpec((tm, LANES), lambda i: (i, 0)),
        out_shape=jax.ShapeDtypeStruct((t, LANES), I32),
        compiler_params=_cparams(("parallel",)),
        name="moe_plan",
    )(info, pad_start)


def _invert_kernel(pos_ref, rt_ref):
    def clear(r, carry):
        rt_ref[r] = 0
        return carry

    lax.fori_loop(0, rt_ref.shape[0], clear, 0, unroll=8)

    def put(n, carry):
        rt_ref[pos_ref[n]] = lax.shift_right_logical(n, 1)
        return carry

    lax.fori_loop(0, pos_ref.shape[0], put, 0, unroll=8)


def _invert(pos_flat, n_rows):
    return pl.pallas_call(
        _invert_kernel,
        in_specs=[pl.BlockSpec(memory_space=pltpu.SMEM)],
        out_specs=pl.BlockSpec(memory_space=pltpu.SMEM),
        out_shape=jax.ShapeDtypeStruct((n_rows,), I32),
        name="moe_invert",
    )(pos_flat)


def _expert_kernel(be_ref, nu_ref, rt_ref, h_ref, wg_ref, wu_ref, wd_ref, o_ref, xbuf0, xbuf1, xb, wgb, wub, wdb, sem):
    b = pl.program_id(0)
    n_used = nu_ref[0]
    R = EXPERT_ROWS

    bufs = (xbuf0, xbuf1)

    def row_copy(tok, r, slot):
        src = h_ref.at[pl.ds(pl.multiple_of(tok * ROW_TILES, ROW_TILES), ROW_TILES)]
        return pltpu.make_async_copy(src, bufs[slot].at[pl.ds(r * ROW_TILES, ROW_TILES)], sem.at[slot])

    def drain(slot):
        def body(r, carry):
            row_copy(0, 0, slot).wait()
            return carry
        lax.fori_loop(0, R, body, 0, unroll=8)

    @pl.when(b == 0)
    def _():
        def body(r, carry):
            row_copy(rt_ref[r], r, 0).start()
            return carry
        lax.fori_loop(0, R, body, 0, unroll=8)

    def block(slot):
        drain(slot)

        @pl.when((b == 0) | (be_ref[b] != be_ref[jnp.maximum(b - 1, 0)]))
        def _():
            wgb[...] = wg_ref[0, 0].astype(BF16)
            wub[...] = wu_ref[0, 0].astype(BF16)
            wdb[...] = wd_ref[0, 0].astype(BF16)

        base = (b + 1) * R
        for r in range(R):
            row_copy(rt_ref[base + r], r, 1 - slot).start()
        for j in range(ROW_TILES):
            xb[:, j * LANES:(j + 1) * LANES] = bufs[slot][pl.ds(j, R, stride=ROW_TILES), :].astype(BF16)
        x = xb[...]
        hid = _silu(_dot(x, wgb[...])) * _dot(x, wub[...])
        y = _dot(hid.astype(BF16), wdb[...])
        for j in range(ROW_TILES):
            o_ref[pl.ds(j, R, stride=ROW_TILES), :] = y[:, j * LANES:(j + 1) * LANES]

    for parity in (0, 1):
        pl.when((b < n_used) & (b % 2 == parity))(functools.partial(block, parity))

    @pl.when(b >= n_used)
    def _():
        o_ref[...] = jnp.zeros(o_ref.shape, F32)

    for parity in (0, 1):
        pl.when((b == n_used) & (b % 2 == parity))(functools.partial(drain, parity))


def _experts(blk_e, n_used, row_token, h, wg, wu, wd, layer):
    r = row_token.shape[0]
    nb = r // EXPERT_ROWS
    wmap = lambda b, be, nu, rt: (layer, be[b], 0, 0)
    return pl.pallas_call(
        _expert_kernel,
        grid_spec=pltpu.PrefetchScalarGridSpec(
            num_scalar_prefetch=3,
            grid=(nb,),
            in_specs=[
                pl.BlockSpec(memory_space=pl.ANY),
                pl.BlockSpec((1, 1, D_MODEL, MOE_FF), wmap),
                pl.BlockSpec((1, 1, D_MODEL, MOE_FF), wmap),
                pl.BlockSpec((1, 1, MOE_FF, D_MODEL), wmap),
            ],
            out_specs=pl.BlockSpec((EXPERT_ROWS * ROW_TILES, LANES), lambda b, be, nu, rt: (b, 0)),
            scratch_shapes=[
                pltpu.VMEM((EXPERT_ROWS * ROW_TILES, LANES), F32),
                pltpu.VMEM((EXPERT_ROWS * ROW_TILES, LANES), F32),
                pltpu.VMEM((EXPERT_ROWS, D_MODEL), BF16),
                pltpu.VMEM((D_MODEL, MOE_FF), BF16),
                pltpu.VMEM((D_MODEL, MOE_FF), BF16),
                pltpu.VMEM((MOE_FF, D_MODEL), BF16),
                pltpu.SemaphoreType.DMA((2,)),
            ],
        ),
        out_shape=jax.ShapeDtypeStruct((r * ROW_TILES, LANES), F32),
        compiler_params=_cparams(("arbitrary",)),
        name="moe_experts",
    )(blk_e, n_used, row_token, h, wg, wu, wd)


COMBINE_TOKENS = 256


def _combine_ple_kernel(pos_ref, y_ref, info_ref, x_ref, g_ref, b_ref, p_ref, wg_ref, bg_ref, wp_ref, o_ref,
                        buf0, buf1, ycat, sem):
    i = pl.program_id(0)
    n = pl.num_programs(0)
    TB = x_ref.shape[0]
    bufs = (buf0, buf1)

    def row_copy(src_row, k, tt, slot):
        src = y_ref.at[pl.ds(pl.multiple_of(src_row * ROW_TILES, ROW_TILES), ROW_TILES)]
        return pltpu.make_async_copy(src, bufs[slot].at[k, pl.ds(tt * ROW_TILES, ROW_TILES)], sem.at[slot])

    def drain(slot):
        def body(tt, carry):
            row_copy(0, 0, 0, slot).wait()
            row_copy(0, 1, 0, slot).wait()
            return carry
        lax.fori_loop(0, TB, body, 0, unroll=8)

    @pl.when(i == 0)
    def _():
        def body(tt, carry):
            row_copy(pos_ref[2 * tt], 0, tt, 0).start(priority=0)
            row_copy(pos_ref[2 * tt + 1], 1, tt, 0).start(priority=1)
            return carry
        lax.fori_loop(0, TB, body, 0, unroll=8)

    def block(slot):
        drain(slot)
        base = jnp.minimum(i + 1, n - 1) * (2 * TB)
        for tt in range(TB):
            row_copy(pos_ref[base + 2 * tt], 0, tt, 1 - slot).start(priority=0)
            row_copy(pos_ref[base + 2 * tt + 1], 1, tt, 1 - slot).start(priority=1)
        info = info_ref[...]
        g0, g1 = info[:, 2:3], info[:, 3:4]
        for j in range(ROW_TILES):
            rows = pl.ds(j, TB, stride=ROW_TILES)
            ycat[:, j * LANES:(j + 1) * LANES] = bufs[slot][0, rows, :] * g0 + bufs[slot][1, rows, :] * g1
        x2 = _layer_norm(DEEPNORM_ALPHA * x_ref[...] + ycat[...], g_ref[...], b_ref[...])
        gate = jax.nn.sigmoid(_dot(x2.astype(BF16), wg_ref[...]) + bg_ref[...])
        o_ref[...] = x2 + gate * _dot(p_ref[...].astype(BF16), wp_ref[...])

    for parity in (0, 1):
        pl.when(i % 2 == parity)(functools.partial(block, parity))
    for parity in (0, 1):
        pl.when((i == n - 1) & (i % 2 == parity))(functools.partial(drain, 1 - parity))


def _combine_ple(pos_flat, y_rows, info, x, g, b, p, wg, bg, wp):
    t = x.shape[0]
    TB = min(COMBINE_TOKENS, t)
    row = lambda i, pos: (i, 0)
    fix = lambda i, pos: (0, 0)
    return pl.pallas_call(
        _combine_ple_kernel,
        grid_spec=pltpu.PrefetchScalarGridSpec(
            num_scalar_prefetch=1,
            grid=(t // TB,),
            in_specs=[
                pl.BlockSpec(memory_space=pl.ANY),
                pl.BlockSpec((TB, LANES), row),
                pl.BlockSpec((TB, D_MODEL), row),
                pl.BlockSpec((1, D_MODEL), fix), pl.BlockSpec((1, D_MODEL), fix),
                pl.BlockSpec((TB, PLE_DIM), row),
                pl.BlockSpec(wg.shape, fix),
                pl.BlockSpec((1, D_MODEL), fix),
                pl.BlockSpec(wp.shape, fix),
            ],
            out_specs=pl.BlockSpec((TB, D_MODEL), row),
            scratch_shapes=[
                pltpu.VMEM((2, TB * ROW_TILES, LANES), F32),
                pltpu.VMEM((2, TB * ROW_TILES, LANES), F32),
                pltpu.VMEM((TB, D_MODEL), F32),
                pltpu.SemaphoreType.DMA((2,)),
            ],
        ),
        out_shape=jax.ShapeDtypeStruct((t, D_MODEL), F32),
        compiler_params=_cparams(("arbitrary",)),
        name="moe_combine_ln_ple",
    )(pos_flat, y_rows, info, x, g, b, p, wg, bg, wp)


def _hier_moe_ln_ple(x, x_folded, wr, br, wg, wu, wd, layer, ln_g, ln_b, p, ple_wg, ple_bg, ple_wp):
    t = x.shape[0]
    info, cnt = _router(x, wr, br)
    counts = cnt[0, ROUTE_LANE0:ROUTE_LANE0 + MOE_EXPERTS].astype(I32)
    padded = (counts + EXPERT_ROWS - 1) // EXPERT_ROWS * EXPERT_ROWS
    pad_end = jnp.cumsum(padded)
    pad_start = jnp.zeros((1, LANES), F32).at[0, :MOE_EXPERTS].set((pad_end - padded).astype(F32))
    n_rows = 2 * t + MOE_EXPERTS * EXPERT_ROWS
    n_blocks = n_rows // EXPERT_ROWS
    blk_start = jnp.arange(n_blocks, dtype=I32) * EXPERT_ROWS
    blk_e = jnp.minimum(jnp.sum(pad_end[None, :] <= blk_start[:, None], axis=1), MOE_EXPERTS - 1).astype(I32)
    n_used = (pad_end[-1:] // EXPERT_ROWS).astype(I32)
    pos = _plan(info, pad_start)[:, :2].reshape(-1)
    y_rows = _experts(blk_e, n_used, _invert(pos, n_rows), x_folded, wg, wu, wd, layer)
    return _combine_ple(pos, y_rows, info, x, ln_g, ln_b, p, ple_wg, ple_bg, ple_wp)


def _rope_tables(positions):
    inv = ROPE_THETA ** (-jnp.arange(0, ROPE_DIM, 2, dtype=F32) / ROPE_DIM)
    ang = positions.astype(F32)[:, None] * inv
    cos, sin = jnp.cos(ang), jnp.sin(ang)
    cos_t = jnp.tile(cos, (1, LANES // (ROPE_DIM // 2)))
    sin_t = jnp.tile(jnp.concatenate([-sin, sin], axis=1), (1, LANES // ROPE_DIM))
    return cos_t, sin_t


def _pad_cols(w, n):
    return jnp.pad(w, ((0, 0), (0, n - w.shape[1])))


def _cat_bf16(parts, n):
    parts = [p.astype(BF16) for p in parts]
    used = sum(p.shape[1] for p in parts)
    return jnp.concatenate(parts + [jnp.zeros((parts[0].shape[0], n - used), BF16)], axis=1)


def _even_w_in(w):
    z, xbc, dt, q, kv = jnp.split(w, [2048, 5120, 5152, 6176], axis=1)
    return _cat_bf16([z, xbc, q, kv, dt], EV_NP)


def _odd_w_in(w):
    q, ckv, krope, qi, ki, wi = jnp.split(w, [3072, 3584, 3648, 4672, 4736], axis=1)
    q = q.reshape(D_MODEL, MLA_HEADS, MLA_NOPE + MLA_ROPE)
    qn = q[:, :, :MLA_NOPE].reshape(D_MODEL, -1)
    qr = q[:, :, MLA_NOPE:].reshape(D_MODEL, -1)
    return _cat_bf16([qn, qr, qi, ckv, krope, ki, wi], OD_NP)


def _head_expand_matrix():
    e = np.zeros((LANES, SSM_INNER), np.float32)
    for h in range(SSM_HEADS):
        e[h, h * SSM_HEAD_DIM:(h + 1) * SSM_HEAD_DIM] = 1.0
    return jnp.asarray(np.tile(e, (3, 1)), BF16)


def kernel(x, p, positions, ev_w_in, ev_conv_w, ev_conv_b, ev_dt_bias, ev_a_log, ev_d_skip, ev_ssm_norm, ev_sinks, ev_w_out, od_w_in, od_kv_norm, od_w_uk, od_w_uv, od_w_out, ln1_g, ln1_b, ln2_g, ln2_b, moe_router_group, moe_router_group_b, moe_router_expert, moe_router_expert_b, moe_w_gate, moe_w_up, moe_w_down, ple_w_proj, ple_w_gate, ple_b_gate):
    batch, s, d = x.shape
    assert batch == 1 and d == D_MODEL
    xs = x[0]
    cos_t, sin_t = _rope_tables(positions[0])
    e_mat = _head_expand_matrix()
    topk = min(IDX_TOPK_MAX, s // 4)
    row = lambda v: v.reshape(1, -1)
    pad_row = lambda v: _pad_cols(v.reshape(1, -1), LANES)
    for i in range(DEPTH):
        j = i // 2
        if i % 2 == 0:
            xp = _inproj(xs, _even_w_in(ev_w_in[j]))
            y_ssm = _ssd(xp, ev_conv_w[j], row(ev_conv_b[j]), pad_row(ev_dt_bias[j]), pad_row(ev_a_log[j]),
                         row(jnp.repeat(ev_d_skip[j], SSM_HEAD_DIM)), row(ev_ssm_norm[j]), e_mat)
            y_att = _swa(xp, ev_sinks[j], cos_t, sin_t)
            w_out = ev_w_out[j].astype(BF16)
            xs, xf = _even_out(y_ssm, y_att, w_out[:SSM_INNER], w_out[SSM_INNER:], xs, row(ln1_g[i]), row(ln1_b[i]))
        else:
            xp = _inproj(xs, _odd_w_in(od_w_in[j]))
            qlat, qrope, qidx, ckvn, kr, ki, wis = _dsa_prep(
                xp, od_w_uk[j].astype(BF16), row(od_kv_norm[j]), cos_t, sin_t)
            bias = _dsa_select(qidx, wis, ki, topk)
            olat = _dsa_attn(qlat, qrope, bias, ckvn, kr)
            xs, xf = _odd_out(olat, od_w_uv[j].astype(BF16), od_w_out[j].astype(BF16), xs, row(ln1_g[i]), row(ln1_b[i]))
        wr = _pad_cols(jnp.concatenate([moe_router_group[i], moe_router_expert[i]], axis=1), LANES)
        br = pad_row(jnp.concatenate([moe_router_group_b[i], moe_router_expert_b[i]]))
        xs = _hier_moe_ln_ple(xs, xf, wr, br, moe_w_gate, moe_w_up, moe_w_down, i, row(ln2_g[i]), row(ln2_b[i]),
                              p[i, 0], ple_w_gate[i].astype(BF16), row(ple_b_gate[i]), ple_w_proj[i].astype(BF16))
    return xs[None]
```

```python
import functools

import jax
import jax.numpy as jnp
import numpy as np
from jax import lax
from jax.experimental import pallas as pl
from jax.experimental.pallas import tpu as pltpu

F32 = jnp.float32
BF16 = jnp.bfloat16
I32 = jnp.int32

D_MODEL = 2048
DEPTH = 4
ROPE_THETA = 10000.0
ROPE_DIM = 64
NORM_EPS = 1e-5
SSM_HEADS = 32
SSM_HEAD_DIM = 64
SSM_INNER = SSM_HEADS * SSM_HEAD_DIM
SSM_GROUPS = 4
SSM_STATE = 128
SSM_CONV = 4
SSM_CHUNK = 128
SWA_Q_HEADS = 16
SWA_KV_HEADS = 2
ATTN_BLOCK = 128
MLA_HEADS = 16
MLA_NOPE = 128
MLA_ROPE = ROPE_DIM
MLA_V = 128
MLA_RANK = 512
MLA_SCALE = (MLA_NOPE + MLA_ROPE) ** -0.5
IDX_HEADS = 16
IDX_DIM = ROPE_DIM
IDX_TOPK_MAX = 256
MOE_GROUPS = 4
MOE_EPG = 8
MOE_EXPERTS = MOE_GROUPS * MOE_EPG
MOE_FF = 512
PLE_DIM = 256
DEEPNORM_ALPHA = (2 * DEPTH) ** 0.25

LANES = 128
SUBLANES = 8
V7X_VMEM_BYTES = 64 * 1024 * 1024
COMPILER_RESERVE_BYTES = 8 * 1024 * 1024
VMEM_LIMIT_BYTES = V7X_VMEM_BYTES - COMPILER_RESERVE_BYTES
ROW_TILES = D_MODEL // LANES

EXPERT_ROWS = 256
DSA_SEL_Q = 128
DSA_ATT_Q = 64
DSA_KC = 512
DSA_SEL_HEADS_PER_DOT = 4
MASK_NEG = -1e30

EV_Z, EV_XS, EV_BC, EV_Q, EV_KV, EV_DT = 0, 2048, 4096, 5120, 6144, 6400
EV_NP = 6656
OD_QN, OD_QR, OD_QI, OD_CKV, OD_KK, OD_WI = 0, 2048, 3072, 4096, 4608, 4736
OD_NP = 5120


def _cparams(sem, vmem=VMEM_LIMIT_BYTES):
    return pltpu.CompilerParams(dimension_semantics=sem, vmem_limit_bytes=vmem)


def _dot(a, b):
    return jnp.dot(a, b, preferred_element_type=F32)


def _dot_nt(a, b):
    return lax.dot_general(a, b, (((1,), (1,)), ((), ())), preferred_element_type=F32)


def _split3(v):
    hi = v.astype(BF16)
    r = v - hi.astype(F32)
    mid = r.astype(BF16)
    lo = (r - mid.astype(F32)).astype(BF16)
    return hi, mid, lo


def _expand(v, e3):
    return _dot(jnp.concatenate(_split3(v), axis=1), e3)


def _silu(v):
    return v * jax.nn.sigmoid(v)


def _layer_norm(v, g, b):
    mu = jnp.mean(v, axis=-1, keepdims=True)
    vc = v - mu
    var = jnp.mean(vc * vc, axis=-1, keepdims=True)
    return vc * lax.rsqrt(var + NORM_EPS) * g + b


def _rope_tile(t, c, s):
    lane = lax.broadcasted_iota(I32, t.shape, 1)
    first_half = (lane & 32) == 0
    swapped = jnp.where(first_half, pltpu.roll(t, LANES - 32, 1), pltpu.roll(t, 32, 1))
    return t * c + swapped * s


def _inproj_kernel(x_ref, w_ref, o_ref):
    o_ref[...] = _dot(x_ref[...].astype(BF16), w_ref[...])


def _inproj(x, w):
    m, k = x.shape
    n = w.shape[1]
    tm, tn = min(2048, m), 512
    return pl.pallas_call(
        _inproj_kernel,
        grid=(m // tm, n // tn),
        in_specs=[pl.BlockSpec((tm, k), lambda i, j: (i, 0)), pl.BlockSpec((k, tn), lambda i, j: (0, j))],
        out_specs=pl.BlockSpec((tm, tn), lambda i, j: (i, j)),
        out_shape=jax.ShapeDtypeStruct((m, n), F32),
        compiler_params=_cparams(("parallel", "arbitrary")),
        name="inproj",
    )(x, w)


def _store_row_layouts(y, o_ref, of_ref):
    o_ref[...] = y
    for j in range(ROW_TILES):
        of_ref[pl.ds(j, y.shape[0], stride=ROW_TILES), :] = y[:, j * LANES:(j + 1) * LANES]


def _row_layout_outputs(m, tm):
    specs = [pl.BlockSpec((tm, D_MODEL), lambda i: (i, 0)), pl.BlockSpec((tm * ROW_TILES, LANES), lambda i: (i, 0))]
    shapes = [jax.ShapeDtypeStruct((m, D_MODEL), F32), jax.ShapeDtypeStruct((m * ROW_TILES, LANES), F32)]
    return specs, shapes


def _even_out_kernel(a1_ref, a2_ref, w1_ref, w2_ref, x_ref, g_ref, b_ref, o_ref, of_ref):
    mix = _dot(a1_ref[...], w1_ref[...]) + _dot(a2_ref[...], w2_ref[...])
    _store_row_layouts(_layer_norm(DEEPNORM_ALPHA * x_ref[...] + mix, g_ref[...], b_ref[...]), o_ref, of_ref)


def _even_out(y_ssm, y_att, w1, w2, x, g, b):
    m = x.shape[0]
    tm = min(256, m)
    row = lambda i: (i, 0)
    fix = lambda i: (0, 0)
    out_specs, out_shape = _row_layout_outputs(m, tm)
    return pl.pallas_call(
        _even_out_kernel,
        grid=(m // tm,),
        in_specs=[
            pl.BlockSpec((tm, y_ssm.shape[1]), row),
            pl.BlockSpec((tm, y_att.shape[1]), row),
            pl.BlockSpec(w1.shape, fix),
            pl.BlockSpec(w2.shape, fix),
            pl.BlockSpec((tm, D_MODEL), row),
            pl.BlockSpec((1, D_MODEL), fix),
            pl.BlockSpec((1, D_MODEL), fix),
        ],
        out_specs=out_specs,
        out_shape=out_shape,
        compiler_params=_cparams(("parallel",)),
        name="even_out_ln",
    )(y_ssm, y_att, w1, w2, x, g, b)


def _ssd_kernel(z_ref, xs_ref, bc_ref, dt_ref, cwx_ref, cbx_ref, cwb_ref, cbb_ref, dtb_ref, alog_ref,
                dsk_ref, nrm_ref, e_ref, y_ref, xs_ext, bc_ext, st_ref):
    c = pl.program_id(0)
    L = SSM_CHUNK
    halo = SUBLANES

    @pl.when(c == 0)
    def _():
        xs_ext[0:halo, :] = jnp.zeros((halo, xs_ext.shape[1]), F32)
        bc_ext[0:halo, :] = jnp.zeros((halo, bc_ext.shape[1]), F32)
        st_ref[...] = jnp.zeros(st_ref.shape, F32)

    xs_ext[halo:halo + L, :] = xs_ref[...]
    bc_ext[halo:halo + L, :] = bc_ref[...]

    def conv(ext, w_ref, b_ref):
        acc = b_ref[...]
        for j in range(SSM_CONV):
            lo = halo - (SSM_CONV - 1) + j
            acc = acc + ext[lo:lo + L, :] * w_ref[j:j + 1, :]
        return acc

    xs = _silu(conv(xs_ext, cwx_ref, cbx_ref))
    bc = _silu(conv(bc_ext, cwb_ref, cbb_ref))
    xs_ext[0:halo, :] = xs_ext[L:L + halo, :]
    bc_ext[0:halo, :] = bc_ext[L:L + halo, :]

    pre = dt_ref[...] + dtb_ref[...]
    dt = jnp.maximum(pre, 0.0) + jnp.log1p(jnp.exp(-jnp.abs(pre)))
    a = dt * (-jnp.exp(alog_ref[...]))
    row = lax.broadcasted_iota(I32, (L, LANES), 0)
    acs = a
    s = 1
    while s < L:
        acs = acs + jnp.where(row >= s, pltpu.roll(acs, s, 0), 0.0)
        s *= 2
    a_last = acs[L - 1:L, :]
    e = e_ref[...]
    dt_x = _expand(dt, e)
    dte_x = _expand(dt * jnp.exp(a_last - acs), e)
    eacs_x = _expand(jnp.exp(acs), e)
    cd_x = _expand(jnp.broadcast_to(jnp.exp(a_last), (SUBLANES, LANES)), e)[0:1, :]
    acs_t = acs.T

    xdt = (xs * dt_x).astype(BF16)
    xd = (xs * dte_x).astype(BF16)
    tri = lax.broadcasted_iota(I32, (L, L), 0) >= lax.broadcasted_iota(I32, (L, L), 1)
    first_head = lax.broadcasted_iota(I32, (L, LANES), 1) < SSM_HEAD_DIM
    n_state = SSM_STATE
    gw = SSM_INNER // SSM_GROUPS
    ys = []
    for g in range(SSM_GROUPS):
        bg = bc[:, g * n_state:(g + 1) * n_state]
        cg = bc[:, SSM_GROUPS * n_state + g * n_state:SSM_GROUPS * n_state + (g + 1) * n_state]
        bb, cb16 = bg.astype(BF16), cg.astype(BF16)
        cbm = _dot_nt(cb16, bb)
        st = st_ref[g]
        y_off = _dot(cb16, st.astype(BF16)) * eacs_x[:, g * gw:(g + 1) * gw]
        st_ref[g] = st * cd_x[:, g * gw:(g + 1) * gw] + _dot(bg.T.astype(BF16), xd[:, g * gw:(g + 1) * gw])
        parts = []
        for j in range(gw // LANES):
            lo = g * gw + j * LANES
            xp = xdt[:, lo:lo + LANES]
            out = None
            for par in (0, 1):
                h = lo // SSM_HEAD_DIM + par
                seg = acs[:, h:h + 1] - acs_t[h:h + 1, :]
                lm = (jnp.exp(jnp.where(tri, seg, -jnp.inf)) * cbm).astype(BF16)
                xm = jnp.where(first_head if par == 0 else jnp.logical_not(first_head), xp, jnp.zeros_like(xp))
                d = _dot(lm, xm)
                out = d if out is None else out + d
            parts.append(out)
        ys.append(jnp.concatenate(parts, axis=1) + y_off)
    y = jnp.concatenate(ys, axis=1) + xs * dsk_ref[...]
    y = y * _silu(z_ref[...])
    outs = []
    for g in range(SSM_GROUPS):
        yg = y[:, g * gw:(g + 1) * gw]
        ms = jnp.mean(yg * yg, axis=-1, keepdims=True)
        outs.append(yg * lax.rsqrt(ms + NORM_EPS))
    y_ref[...] = (jnp.concatenate(outs, axis=1) * nrm_ref[...]).astype(BF16)


def _ssd(xp, cw, cb, dtb, alog, dsk, nrm, e):
    s = xp.shape[0]
    L = SSM_CHUNK
    bcw = 2 * SSM_GROUPS * SSM_STATE
    cwx, cwb = cw[:, :SSM_INNER], cw[:, SSM_INNER:]
    cbx, cbb = cb[:, :SSM_INNER], cb[:, SSM_INNER:]
    fix = lambda i: (0, 0)
    return pl.pallas_call(
        _ssd_kernel,
        grid=(s // L,),
        in_specs=[
            pl.BlockSpec((L, SSM_INNER), lambda i: (i, EV_Z // SSM_INNER)),
            pl.BlockSpec((L, SSM_INNER), lambda i: (i, EV_XS // SSM_INNER)),
            pl.BlockSpec((L, bcw), lambda i: (i, EV_BC // bcw)),
            pl.BlockSpec((L, LANES), lambda i: (i, EV_DT // LANES)),
            pl.BlockSpec(cwx.shape, fix), pl.BlockSpec(cbx.shape, fix),
            pl.BlockSpec(cwb.shape, fix), pl.BlockSpec(cbb.shape, fix),
            pl.BlockSpec((1, LANES), fix), pl.BlockSpec((1, LANES), fix),
            pl.BlockSpec((1, SSM_INNER), fix), pl.BlockSpec((1, SSM_INNER), fix),
            pl.BlockSpec(e.shape, fix),
        ],
        out_specs=pl.BlockSpec((L, SSM_INNER), lambda i: (i, 0)),
        out_shape=jax.ShapeDtypeStruct((s, SSM_INNER), BF16),
        scratch_shapes=[
            pltpu.VMEM((L + 2 * SUBLANES, SSM_INNER), F32),
            pltpu.VMEM((L + 2 * SUBLANES, bcw), F32),
            pltpu.VMEM((SSM_GROUPS, SSM_STATE, SSM_INNER // SSM_GROUPS), F32),
        ],
        compiler_params=_cparams(("arbitrary",)),
        name="ssd_scan",
    )(xp, xp, xp, xp, cwx, cbx, cwb, cbb, dtb, alog, dsk, nrm, e)


def _swa_kernel(sink_ref, q_ref, kvc_ref, kvp_ref, cq_ref, sq_ref, cp_ref, sp_ref, o_ref):
    i = pl.program_id(0)
    B = ATTN_BLOCK
    lane = lax.broadcasted_iota(I32, (B, LANES), 1)
    lo_half = lane < ROPE_DIM
    cq, sq = cq_ref[...], sq_ref[...]
    kc = _rope_tile(kvc_ref[:, 0:LANES], cq, sq)
    kp = _rope_tile(kvp_ref[:, 0:LANES], cp_ref[...], sp_ref[...])
    kcat = jnp.concatenate([kp, kc], axis=0)
    kmat = (kcat.astype(BF16), pltpu.roll(kcat, ROPE_DIM, 1).astype(BF16))
    vcat = jnp.concatenate([kvp_ref[:, LANES:2 * LANES], kvc_ref[:, LANES:2 * LANES]], axis=0)
    vrol = pltpu.roll(vcat, ROPE_DIM, 1)
    lane2 = lax.broadcasted_iota(I32, (2 * B, LANES), 1) < ROPE_DIM
    vdup = (jnp.where(lane2, vcat, vrol).astype(BF16), jnp.where(lane2, vrol, vcat).astype(BF16))
    r = lax.broadcasted_iota(I32, (B, 2 * B), 0)
    col = lax.broadcasted_iota(I32, (B, 2 * B), 1)
    mask = (col > r) & (col <= r + B) & ((i > 0) | (col >= B))
    scale = ROPE_DIM ** -0.5
    hpg = SWA_Q_HEADS // SWA_KV_HEADS
    for j in range(SWA_Q_HEADS // 2):
        g = (2 * j) // hpg
        qt = _rope_tile(q_ref[:, j * LANES:(j + 1) * LANES], cq, sq)
        outs = []
        for par in (0, 1):
            h = 2 * j + par
            qm = jnp.where(lo_half if par == 0 else jnp.logical_not(lo_half), qt, 0.0).astype(BF16)
            logit = _dot_nt(qm, kmat[0] if par == g else kmat[1]) * scale
            logit = jnp.where(mask, logit, -jnp.inf)
            sink = sink_ref[h]
            m = jnp.maximum(jnp.max(logit, axis=-1, keepdims=True), sink)
            ex = jnp.exp(logit - m)
            prob = ex / (jnp.sum(ex, axis=-1, keepdims=True) + jnp.exp(sink - m))
            outs.append(_dot(prob.astype(BF16), vdup[g]))
        o_ref[:, j * LANES:(j + 1) * LANES] = jnp.where(lo_half, outs[0], outs[1]).astype(BF16)


def _swa(xp, sinks, cos_t, sin_t):
    s = xp.shape[0]
    B = ATTN_BLOCK
    qw = SWA_Q_HEADS * ROPE_DIM
    kvw = 2 * SWA_KV_HEADS * ROPE_DIM
    prev = lambda i: (jnp.maximum(i - 1, 0), 0)
    cur = lambda i: (i, 0)
    return pl.pallas_call(
        _swa_kernel,
        grid=(s // B,),
        in_specs=[
            pl.BlockSpec(memory_space=pltpu.SMEM),
            pl.BlockSpec((B, qw), lambda i: (i, EV_Q // qw)),
            pl.BlockSpec((B, kvw), lambda i: (i, EV_KV // kvw)),
            pl.BlockSpec((B, kvw), lambda i: (jnp.maximum(i - 1, 0), EV_KV // kvw)),
            pl.BlockSpec((B, LANES), cur), pl.BlockSpec((B, LANES), cur),
            pl.BlockSpec((B, LANES), prev), pl.BlockSpec((B, LANES), prev),
        ],
        out_specs=pl.BlockSpec((B, qw), cur),
        out_shape=jax.ShapeDtypeStruct((s, qw), BF16),
        compiler_params=_cparams(("parallel",)),
        name="swa_sink",
    )(sinks, xp, xp, xp, cos_t, sin_t, cos_t, sin_t)


def _dsa_prep_kernel(qn_ref, qr_ref, qi_ref, ckv_ref, kk_ref, wi_ref, wuk_ref, kvn_ref, c_ref, s_ref,
                     qlat_ref, qrope_ref, qidx_ref, ckvn_ref, kr_ref, ki_ref, wis_ref):
    c, s = c_ref[...], s_ref[...]
    lane = lax.broadcasted_iota(I32, c.shape, 1)
    lo_half = lane < ROPE_DIM
    for h in range(MLA_HEADS):
        qn = qn_ref[:, h * MLA_NOPE:(h + 1) * MLA_NOPE].astype(BF16)
        qlat_ref[h] = _dot(qn, wuk_ref[h]).astype(BF16)
    for src, dst in ((qr_ref, qrope_ref), (qi_ref, qidx_ref)):
        for j in range(MLA_HEADS // 2):
            t = _rope_tile(src[:, j * LANES:(j + 1) * LANES], c, s)
            dst[2 * j] = jnp.where(lo_half, t, 0.0).astype(BF16)
            dst[2 * j + 1] = jnp.where(lo_half, pltpu.roll(t, ROPE_DIM, 1), 0.0).astype(BF16)
    kk = _rope_tile(kk_ref[...], c, s)
    kr_ref[...] = jnp.where(lo_half, kk, 0.0).astype(BF16)
    ki_ref[...] = jnp.where(lo_half, pltpu.roll(kk, ROPE_DIM, 1), 0.0).astype(BF16)
    ckv = ckv_ref[...]
    ms = jnp.mean(ckv * ckv, axis=-1, keepdims=True)
    ckvn_ref[...] = (ckv * lax.rsqrt(ms + NORM_EPS) * kvn_ref[...]).astype(BF16)
    wis_ref[...] = wi_ref[...] * (IDX_HEADS ** -0.5 * IDX_DIM ** -0.5)


def _dsa_prep(xp, wuk, kvn, cos_t, sin_t):
    s = xp.shape[0]
    tm = min(256, s)
    H = MLA_HEADS
    fix2 = lambda i: (0, 0)
    hrow = lambda i: (0, i, 0)
    row = lambda i: (i, 0)
    return pl.pallas_call(
        _dsa_prep_kernel,
        grid=(s // tm,),
        in_specs=[
            pl.BlockSpec((tm, 2048), lambda i: (i, OD_QN // 2048)),
            pl.BlockSpec((tm, 1024), lambda i: (i, OD_QR // 1024)),
            pl.BlockSpec((tm, 1024), lambda i: (i, OD_QI // 1024)),
            pl.BlockSpec((tm, MLA_RANK), lambda i: (i, OD_CKV // MLA_RANK)),
            pl.BlockSpec((tm, LANES), lambda i: (i, OD_KK // LANES)),
            pl.BlockSpec((tm, LANES), lambda i: (i, OD_WI // LANES)),
            pl.BlockSpec(wuk.shape, lambda i: (0, 0, 0)),
            pl.BlockSpec((1, MLA_RANK), fix2),
            pl.BlockSpec((tm, LANES), row), pl.BlockSpec((tm, LANES), row),
        ],
        out_specs=[
            pl.BlockSpec((H, tm, MLA_RANK), hrow),
            pl.BlockSpec((H, tm, LANES), hrow),
            pl.BlockSpec((H, tm, LANES), hrow),
            pl.BlockSpec((tm, MLA_RANK), row),
            pl.BlockSpec((tm, LANES), row),
            pl.BlockSpec((tm, LANES), row),
            pl.BlockSpec((tm, LANES), row),
        ],
        out_shape=[
            jax.ShapeDtypeStruct((H, s, MLA_RANK), BF16),
            jax.ShapeDtypeStruct((H, s, LANES), BF16),
            jax.ShapeDtypeStruct((H, s, LANES), BF16),
            jax.ShapeDtypeStruct((s, MLA_RANK), BF16),
            jax.ShapeDtypeStruct((s, LANES), BF16),
            jax.ShapeDtypeStruct((s, LANES), BF16),
            jax.ShapeDtypeStruct((s, LANES), F32),
        ],
        compiler_params=_cparams(("parallel",)),
        name="dsa_prep",
    )(xp, xp, xp, xp, xp, xp, wuk, kvn, cos_t, sin_t)


def _dsa_select_kernel(topk, qi_ref, wi_ref, ki_ref, bias_ref, key_ref, dig_ref):
    i = pl.program_id(0)
    Q, KC = DSA_SEL_Q, DSA_KC
    H = IDX_HEADS
    n_chunks = bias_ref.shape[0]
    n_vis = ((i + 1) * Q + KC - 1) // KC
    wi = wi_ref[...]
    qpos = i * Q + lax.broadcasted_iota(I32, (KC, Q), 1)
    kloc = lax.broadcasted_iota(I32, (KC, Q), 0)
    HG = DSA_SEL_HEADS_PER_DOT
    N_ACC = 4
    DIGIT_BITS = (11, 11, 10)
    DIGIT_SHIFT = (21, 10, 0)
    TOP_BIAS = 1 << (DIGIT_BITS[0] - 1)
    GUARD = jnp.int32(-0x7FFF8000)
    FIELD_ONES = jnp.int32(0x00010001)
    HK = KC // 2

    def pack(lo, hi):
        return lo | (hi << 16) | GUARD

    def score_chunk(c, carry):
        k = ki_ref[pl.ds(pl.multiple_of(c * KC, KC), KC), :]
        acc = jnp.zeros((KC, Q), F32)
        for g in range(H // HG):
            sc = _dot_nt(k, qi_ref[g * HG:(g + 1) * HG].reshape(HG * Q, LANES))
            for hh in range(HG):
                h = g * HG + hh
                acc = acc + jnp.maximum(sc[:, hh * Q:(hh + 1) * Q], 0.0) * wi[h:h + 1, :]
        acc = jnp.where(c * KC + kloc <= qpos, acc, -jnp.inf)
        bits = pltpu.bitcast(acc, I32)
        key = bits ^ ((bits >> 31) & jnp.int32(0x7FFFFFFF))
        key_ref[c] = key
        for d in range(len(DIGIT_BITS)):
            dig = (key >> DIGIT_SHIFT[d]) + TOP_BIAS if d == 0 else (key >> DIGIT_SHIFT[d]) & ((1 << DIGIT_BITS[d]) - 1)
            dig_ref[d, c] = pack(dig[:HK], dig[HK:]) + FIELD_ONES
        return carry

    lax.fori_loop(0, n_vis, score_chunk, 0)

    def count_ge(d, cand):
        cand2 = (cand + 1) | ((cand + 1) << 16)

        def body(c, accs):
            v = dig_ref[d, c]
            accs = list(accs)
            for r in range(HK // SUBLANES):
                w = v[r * SUBLANES:(r + 1) * SUBLANES, :] - cand2
                accs[r % N_ACC] = accs[r % N_ACC] + (lax.shift_right_logical(w, 15) & FIELD_ONES)
            return tuple(accs)
        accs = lax.fori_loop(0, n_vis, body, tuple(jnp.zeros((SUBLANES, Q), I32) for _ in range(N_ACC)))
        acc = sum(accs)
        cnt = (acc & 0xFFFF) + lax.shift_right_logical(acc, 16)
        return jnp.sum(cnt.astype(F32), axis=0, keepdims=True)

    def search(d, need):
        def bit_step(b, thr):
            cand = thr | (jnp.int32(1) << (DIGIT_BITS[d] - 1 - b))
            return jnp.where(count_ge(d, cand) >= need, cand, thr)

        return lax.fori_loop(0, DIGIT_BITS[d], bit_step, jnp.zeros((1, Q), I32))

    def drop_unless_equal(d, p):
        def body(c, carry):
            cur, nxt = dig_ref[d, c], dig_ref[d + 1, c]
            lo = jnp.where((cur & 0x7FFF) == p + 1, nxt & 0x7FFF, 0)
            hi = jnp.where((lax.shift_right_logical(cur, 16) & 0x7FFF) == p + 1, lax.shift_right_logical(nxt, 16) & 0x7FFF, 0)
            dig_ref[d + 1, c] = pack(lo, hi)
            return carry

        lax.fori_loop(0, n_vis, body, 0)

    need = jnp.full((1, Q), float(topk), F32)
    thr = jnp.zeros((1, Q), I32)
    for d in range(len(DIGIT_BITS)):
        p = search(d, need)
        thr = thr + ((p - (TOP_BIAS if d == 0 else 0)) << DIGIT_SHIFT[d])
        if d + 1 < len(DIGIT_BITS):
            need = need - count_ge(d, p + 1)
            drop_unless_equal(d, p)

    def emit(c, cnt):
        sel = (key_ref[c] >= thr) & (c * KC + kloc <= qpos)
        bias_ref[c] = jnp.where(sel, 0.0, MASK_NEG).T
        return cnt + jnp.sum(jnp.where(sel, 1.0, 0.0), axis=0, keepdims=True)

    n_sel = lax.fori_loop(0, n_vis, emit, jnp.zeros((1, Q), F32))

    @pl.when(jnp.max(n_sel) > topk)
    def _():
        def count(pred):
            def body(c, cnt):
                return cnt + jnp.sum(jnp.where(pred(c), 1.0, 0.0), axis=0, keepdims=True)
            return lax.fori_loop(0, n_vis, body, jnp.zeros((1, Q), F32))

        def tied(c):
            return (key_ref[c] == thr) & (c * KC + kloc <= qpos)

        need_tied = topk - count(lambda c: key_ref[c] > thr)

        def bit_step(b, last):
            cand = last | (jnp.int32(1) << (index_bits - 1 - b))
            in_front = count(lambda c: tied(c) & (c * KC + kloc < cand))
            return jnp.where(in_front < need_tied, cand, last)

        index_bits = (n_chunks * KC - 1).bit_length()
        last = lax.fori_loop(0, index_bits, bit_step, jnp.zeros((1, Q), I32))

        def emit_ties(c, carry):
            sel = (key_ref[c] > thr) | (tied(c) & (c * KC + kloc <= last))
            bias_ref[c] = jnp.where(sel, 0.0, MASK_NEG).T
            return carry

        lax.fori_loop(0, n_vis, emit_ties, 0)

    def fill(c, carry):
        bias_ref[c] = jnp.full((Q, KC), MASK_NEG, F32)
        return carry

    lax.fori_loop(n_vis, n_chunks, fill, 0)


def _dsa_select(qidx, wis, ki, topk):
    H, s, _ = qidx.shape
    Q, KC = DSA_SEL_Q, DSA_KC
    assert Q == LANES
    nch = s // KC
    return pl.pallas_call(
        functools.partial(_dsa_select_kernel, topk),
        grid=(s // Q,),
        in_specs=[
            pl.BlockSpec((H, Q, LANES), lambda i: (0, i, 0)),
            pl.BlockSpec((LANES, Q), lambda i: (0, i)),
            pl.BlockSpec((s, LANES), lambda i: (0, 0)),
        ],
        out_specs=pl.BlockSpec((nch, Q, KC), lambda i: (0, i, 0)),
        out_shape=jax.ShapeDtypeStruct((nch, s, KC), F32),
        scratch_shapes=[pltpu.VMEM((nch, KC, Q), I32), pltpu.VMEM((3, nch, KC // 2, Q), I32)],
        compiler_params=_cparams(("parallel",)),
        name="dsa_select",
    )(qidx, wis.T, ki)


def _dsa_attn_kernel(ql_ref, qr_ref, bias_ref, ckv_ref, kr_ref, o_ref, m_ref, l_ref, acc_ref, s_ref):
    i = pl.program_id(0)
    Q, KC, H = DSA_ATT_Q, DSA_KC, MLA_HEADS
    n_vis = ((i + 1) * Q + KC - 1) // KC
    m_ref[...] = jnp.full(m_ref.shape, -jnp.inf, F32)
    l_ref[...] = jnp.zeros(l_ref.shape, F32)
    acc_ref[...] = jnp.zeros(acc_ref.shape, F32)

    def keys(c):
        off = pl.multiple_of(c * KC, KC)
        return ckv_ref[pl.ds(off, KC), :], kr_ref[pl.ds(off, KC), :]

    def raw_logits(c):
        ck, kr = keys(c)
        ql = ql_ref[...].reshape(H * Q, MLA_RANK)
        qr = qr_ref[...].reshape(H * Q, LANES)
        return _dot_nt(ql, ck) + _dot_nt(qr, kr)

    def consume(c, slot):
        logit = s_ref[slot] * MLA_SCALE
        logit = (logit.reshape(H, Q, KC) + bias_ref[c][None]).reshape(H * Q, KC)
        m_old = m_ref[...]
        m_new = jnp.maximum(m_old, jnp.max(logit, axis=-1, keepdims=True))
        alpha = jnp.exp(m_old - m_new)
        p = jnp.exp(logit - m_new)
        l_ref[...] = alpha * l_ref[...] + jnp.sum(p, axis=-1, keepdims=True)
        acc_ref[...] = alpha * acc_ref[...] + _dot(p.astype(BF16), keys(c)[0])
        m_ref[...] = m_new

    s_ref[0] = raw_logits(0)

    def pair(j, carry):
        c = 2 * j
        s_ref[1] = raw_logits(c + 1)
        consume(c, 0)
        s_ref[0] = raw_logits(c + 2)
        consume(c + 1, 1)
        return carry

    n_pairs = (n_vis - 1) // 2
    lax.fori_loop(0, n_pairs, pair, 0)
    last = 2 * n_pairs

    @pl.when(last + 1 < n_vis)
    def _():
        s_ref[1] = raw_logits(last + 1)
        consume(last, 0)
        consume(last + 1, 1)

    @pl.when(last + 1 == n_vis)
    def _():
        consume(last, 0)

    o_ref[...] = (acc_ref[...] / l_ref[...]).reshape(H, Q, MLA_RANK).astype(BF16)


def _dsa_attn(qlat, qrope, bias, ckvn, kr):
    H, s, _ = qlat.shape
    Q, KC = DSA_ATT_Q, DSA_KC
    nch = s // KC
    hrow = lambda i: (0, i, 0)
    fix = lambda i: (0, 0)
    return pl.pallas_call(
        _dsa_attn_kernel,
        grid=(s // Q,),
        in_specs=[
            pl.BlockSpec((H, Q, MLA_RANK), hrow),
            pl.BlockSpec((H, Q, LANES), hrow),
            pl.BlockSpec((nch, Q, KC), hrow),
            pl.BlockSpec((s, MLA_RANK), fix),
            pl.BlockSpec((s, LANES), fix),
        ],
        out_specs=pl.BlockSpec((H, Q, MLA_RANK), hrow),
        out_shape=jax.ShapeDtypeStruct((H, s, MLA_RANK), BF16),
        scratch_shapes=[
            pltpu.VMEM((H * Q, 1), F32),
            pltpu.VMEM((H * Q, 1), F32),
            pltpu.VMEM((H * Q, MLA_RANK), F32),
            pltpu.VMEM((2, H * Q, KC), F32),
        ],
        compiler_params=_cparams(("parallel",)),
        name="dsa_attn",
    )(qlat, qrope, bias, ckvn, kr)


def _odd_out_kernel(ol_ref, wuv_ref, wo_ref, x_ref, g_ref, b_ref, o_ref, of_ref, u_ref):
    for h in range(MLA_HEADS):
        u_ref[:, h * MLA_V:(h + 1) * MLA_V] = _dot(ol_ref[h], wuv_ref[h]).astype(BF16)
    mix = _dot(u_ref[...], wo_ref[...])
    _store_row_layouts(_layer_norm(DEEPNORM_ALPHA * x_ref[...] + mix, g_ref[...], b_ref[...]), o_ref, of_ref)


def _odd_out(olat, wuv, wo, x, g, b):
    H, s, _ = olat.shape
    tm = min(256, s)
    row = lambda i: (i, 0)
    fix = lambda i: (0, 0)
    out_specs, out_shape = _row_layout_outputs(s, tm)
    return pl.pallas_call(
        _odd_out_kernel,
        grid=(s // tm,),
        in_specs=[
            pl.BlockSpec((H, tm, MLA_RANK), lambda i: (0, i, 0)),
            pl.BlockSpec(wuv.shape, lambda i: (0, 0, 0)),
            pl.BlockSpec(wo.shape, fix),
            pl.BlockSpec((tm, D_MODEL), row),
            pl.BlockSpec((1, D_MODEL), fix), pl.BlockSpec((1, D_MODEL), fix),
        ],
        out_specs=out_specs,
        out_shape=out_shape,
        scratch_shapes=[pltpu.VMEM((tm, MLA_HEADS * MLA_V), BF16)],
        compiler_params=_cparams(("parallel",)),
        name="odd_out_ln",
    )(olat, wuv, wo, x, g, b)


ROUTE_LANE0 = MOE_GROUPS


def _router_kernel(h_ref, wr_ref, br_ref, info_ref, cnt_ref, run_ref):
    i = pl.program_id(0)

    @pl.when(i == 0)
    def _():
        run_ref[...] = jnp.zeros(run_ref.shape, F32)

    h = h_ref[...]
    tm = h.shape[0]
    h_hi = h.astype(BF16)
    h_lo = (h - h_hi.astype(F32)).astype(BF16)
    w = wr_ref[...]
    w_hi = w.astype(BF16)
    w_lo = (w - w_hi.astype(F32)).astype(BF16)
    logits = _dot(h_hi, w_hi) + (_dot(h_hi, w_lo) + _dot(h_lo, w_hi)) + br_ref[...]
    lane = lax.broadcasted_iota(I32, (tm, LANES), 1)
    lane_f = lane.astype(F32)
    neg = -jnp.inf
    big = float(LANES)
    is_grp = lane < MOE_GROUPS
    gl = jnp.where(is_grp, logits, neg)
    gmax = jnp.max(gl, axis=-1, keepdims=True)
    gsel = jnp.min(jnp.where(gl == gmax, lane_f, big), axis=-1, keepdims=True)
    gsum = jnp.sum(jnp.where(is_grp, jnp.exp(logits - gmax), 0.0), axis=-1, keepdims=True)
    egrp = ((lane - ROUTE_LANE0) >> 3).astype(F32)
    valid = (lane >= ROUTE_LANE0) & (lane < ROUTE_LANE0 + MOE_EXPERTS) & (egrp == gsel)
    el = jnp.where(valid, logits, neg)
    v1 = jnp.max(el, axis=-1, keepdims=True)
    i1 = jnp.min(jnp.where(el == v1, lane_f, big), axis=-1, keepdims=True)
    el2 = jnp.where(lane_f == i1, neg, el)
    v2 = jnp.max(el2, axis=-1, keepdims=True)
    i2 = jnp.min(jnp.where(el2 == v2, lane_f, big), axis=-1, keepdims=True)
    t = jnp.exp(v2 - v1)
    p1 = 1.0 / (1.0 + t)
    p2 = t / (1.0 + t)
    ggate = 1.0 / gsum
    m1 = lane_f == i1
    m2 = lane_f == i2
    memb = jnp.where(m1 | m2, 1.0, 0.0)
    tri = (lax.broadcasted_iota(I32, (tm, tm), 0) > lax.broadcasted_iota(I32, (tm, tm), 1))
    cum = _dot(jnp.where(tri, 1.0, 0.0).astype(BF16), memb.astype(BF16)) + run_ref[...]
    rank1 = jnp.sum(jnp.where(m1, cum, 0.0), axis=-1, keepdims=True)
    rank2 = jnp.sum(jnp.where(m2, cum, 0.0), axis=-1, keepdims=True)
    run_ref[...] = run_ref[...] + jnp.sum(memb, axis=0, keepdims=True)
    info = jnp.where(lane == 0, i1 - ROUTE_LANE0, 0.0)
    info = jnp.where(lane == 1, i2 - ROUTE_LANE0, info)
    info = jnp.where(lane == 2, p1 * ggate, info)
    info = jnp.where(lane == 3, p2 * ggate, info)
    info = jnp.where(lane == 4, rank1, info)
    info = jnp.where(lane == 5, rank2, info)
    info_ref[...] = info
    cnt_ref[...] = run_ref[...]


def _router(h, wr, br):
    t = h.shape[0]
    tm = min(512, t)
    return pl.pallas_call(
        _router_kernel,
        grid=(t // tm,),
        in_specs=[
            pl.BlockSpec((tm, D_MODEL), lambda i: (i, 0)),
            pl.BlockSpec(wr.shape, lambda i: (0, 0)),
            pl.BlockSpec((1, LANES), lambda i: (0, 0)),
        ],
        out_specs=[pl.BlockSpec((tm, LANES), lambda i: (i, 0)), pl.BlockSpec((1, LANES), lambda i: (0, 0))],
        out_shape=[jax.ShapeDtypeStruct((t, LANES), F32), jax.ShapeDtypeStruct((1, LANES), F32)],
        scratch_shapes=[pltpu.VMEM((1, LANES), F32)],
        compiler_params=_cparams(("arbitrary",)),
        name="moe_router",
    )(h, wr, br)


def _plan_kernel(info_ref, ps_ref, pos_ref):
    info = info_ref[...]
    lane = lax.broadcasted_iota(I32, info.shape, 1)
    lane_f = lane.astype(F32)
    ps = ps_ref[...]
    pos1 = jnp.sum(jnp.where(lane_f == info[:, 0:1], ps, 0.0), axis=-1, keepdims=True) + info[:, 4:5]
    pos2 = jnp.sum(jnp.where(lane_f == info[:, 1:2], ps, 0.0), axis=-1, keepdims=True) + info[:, 5:6]
    pos_ref[...] = jnp.where(lane == 0, pos1, jnp.where(lane == 1, pos2, 0.0)).astype(I32)


def _plan(info, pad_start):
    t = info.shape[0]
    tm = min(1024, t)
    return pl.pallas_call(
        _plan_kernel,
        grid=(t // tm,),
        in_specs=[pl.BlockSpec((tm, LANES), lambda i: (i, 0)), pl.BlockSpec((1, LANES), lambda i: (0, 0))],
        out_specs=pl.BlockSpec((tm, LANES), lambda i: (i, 0)),
        out_shape=jax.ShapeDtypeStruct((t, LANES), I32),
        compiler_params=_cparams(("parallel",)),
        name="moe_plan",
    )(info, pad_start)


def _invert_kernel(pos_ref, rt_ref):
    def clear(r, carry):
        rt_ref[r] = 0
        return carry

    lax.fori_loop(0, rt_ref.shape[0], clear, 0, unroll=8)

    def put(n, carry):
        rt_ref[pos_ref[n]] = lax.shift_right_logical(n, 1)
        return carry

    lax.fori_loop(0, pos_ref.shape[0], put, 0, unroll=8)


def _invert(pos_flat, n_rows):
    return pl.pallas_call(
        _invert_kernel,
        in_specs=[pl.BlockSpec(memory_space=pltpu.SMEM)],
        out_specs=pl.BlockSpec(memory_space=pltpu.SMEM),
        out_shape=jax.ShapeDtypeStruct((n_rows,), I32),
        name="moe_invert",
    )(pos_flat)


def _expert_kernel(be_ref, nu_ref, rt_ref, h_ref, wg_ref, wu_ref, wd_ref, o_ref, xbuf0, xbuf1, xb, wgb, wub, wdb, sem):
    b = pl.program_id(0)
    n_used = nu_ref[0]
    R = EXPERT_ROWS

    bufs = (xbuf0, xbuf1)

    def row_copy(tok, r, slot):
        src = h_ref.at[pl.ds(pl.multiple_of(tok * ROW_TILES, ROW_TILES), ROW_TILES)]
        return pltpu.make_async_copy(src, bufs[slot].at[pl.ds(r * ROW_TILES, ROW_TILES)], sem.at[slot])

    def drain(slot):
        def body(r, carry):
            row_copy(0, 0, slot).wait()
            return carry
        lax.fori_loop(0, R, body, 0, unroll=8)

    @pl.when(b == 0)
    def _():
        def body(r, carry):
            row_copy(rt_ref[r], r, 0).start()
            return carry
        lax.fori_loop(0, R, body, 0, unroll=8)

    def block(slot):
        drain(slot)

        @pl.when((b == 0) | (be_ref[b] != be_ref[jnp.maximum(b - 1, 0)]))
        def _():
            wgb[...] = wg_ref[0, 0].astype(BF16)
            wub[...] = wu_ref[0, 0].astype(BF16)
            wdb[...] = wd_ref[0, 0].astype(BF16)

        base = (b + 1) * R
        for r in range(R):
            row_copy(rt_ref[base + r], r, 1 - slot).start()
        for j in range(ROW_TILES):
            xb[:, j * LANES:(j + 1) * LANES] = bufs[slot][pl.ds(j, R, stride=ROW_TILES), :].astype(BF16)
        x = xb[...]
        hid = _silu(_dot(x, wgb[...])) * _dot(x, wub[...])
        y = _dot(hid.astype(BF16), wdb[...])
        for j in range(ROW_TILES):
            o_ref[pl.ds(j, R, stride=ROW_TILES), :] = y[:, j * LANES:(j + 1) * LANES]

    for parity in (0, 1):
        pl.when((b < n_used) & (b % 2 == parity))(functools.partial(block, parity))

    @pl.when(b >= n_used)
    def _():
        o_ref[...] = jnp.zeros(o_ref.shape, F32)

    for parity in (0, 1):
        pl.when((b == n_used) & (b % 2 == parity))(functools.partial(drain, parity))


def _experts(blk_e, n_used, row_token, h, wg, wu, wd, layer):
    r = row_token.shape[0]
    nb = r // EXPERT_ROWS
    wmap = lambda b, be, nu, rt: (layer, be[b], 0, 0)
    return pl.pallas_call(
        _expert_kernel,
        grid_spec=pltpu.PrefetchScalarGridSpec(
            num_scalar_prefetch=3,
            grid=(nb,),
            in_specs=[
                pl.BlockSpec(memory_space=pl.ANY),
                pl.BlockSpec((1, 1, D_MODEL, MOE_FF), wmap),
                pl.BlockSpec((1, 1, D_MODEL, MOE_FF), wmap),
                pl.BlockSpec((1, 1, MOE_FF, D_MODEL), wmap),
            ],
            out_specs=pl.BlockSpec((EXPERT_ROWS * ROW_TILES, LANES), lambda b, be, nu, rt: (b, 0)),
            scratch_shapes=[
                pltpu.VMEM((EXPERT_ROWS * ROW_TILES, LANES), F32),
                pltpu.VMEM((EXPERT_ROWS * ROW_TILES, LANES), F32),
                pltpu.VMEM((EXPERT_ROWS, D_MODEL), BF16),
                pltpu.VMEM((D_MODEL, MOE_FF), BF16),
                pltpu.VMEM((D_MODEL, MOE_FF), BF16),
                pltpu.VMEM((MOE_FF, D_MODEL), BF16),
                pltpu.SemaphoreType.DMA((2,)),
            ],
        ),
        out_shape=jax.ShapeDtypeStruct((r * ROW_TILES, LANES), F32),
        compiler_params=_cparams(("arbitrary",)),
        name="moe_experts",
    )(blk_e, n_used, row_token, h, wg, wu, wd)


COMBINE_TOKENS = 256


def _combine_ple_kernel(pos_ref, y_ref, info_ref, x_ref, g_ref, b_ref, p_ref, wg_ref, bg_ref, wp_ref, o_ref,
                        buf0, buf1, ycat, sem):
    i = pl.program_id(0)
    n = pl.num_programs(0)
    TB = x_ref.shape[0]
    bufs = (buf0, buf1)

    def row_copy(src_row, k, tt, slot):
        src = y_ref.at[pl.ds(pl.multiple_of(src_row * ROW_TILES, ROW_TILES), ROW_TILES)]
        return pltpu.make_async_copy(src, bufs[slot].at[k, pl.ds(tt * ROW_TILES, ROW_TILES)], sem.at[slot])

    def drain(slot):
        def body(tt, carry):
            row_copy(0, 0, 0, slot).wait()
            row_copy(0, 1, 0, slot).wait()
            return carry
        lax.fori_loop(0, TB, body, 0, unroll=8)

    @pl.when(i == 0)
    def _():
        def body(tt, carry):
            row_copy(pos_ref[2 * tt], 0, tt, 0).start(priority=0)
            row_copy(pos_ref[2 * tt + 1], 1, tt, 0).start(priority=1)
            return carry
        lax.fori_loop(0, TB, body, 0, unroll=8)

    def block(slot):
        drain(slot)
        base = jnp.minimum(i + 1, n - 1) * (2 * TB)
        for tt in range(TB):
            row_copy(pos_ref[base + 2 * tt], 0, tt, 1 - slot).start(priority=0)
            row_copy(pos_ref[base + 2 * tt + 1], 1, tt, 1 - slot).start(priority=1)
        info = info_ref[...]
        g0, g1 = info[:, 2:3], info[:, 3:4]
        for j in range(ROW_TILES):
            rows = pl.ds(j, TB, stride=ROW_TILES)
            ycat[:, j * LANES:(j + 1) * LANES] = bufs[slot][0, rows, :] * g0 + bufs[slot][1, rows, :] * g1
        x2 = _layer_norm(DEEPNORM_ALPHA * x_ref[...] + ycat[...], g_ref[...], b_ref[...])
        gate = jax.nn.sigmoid(_dot(x2.astype(BF16), wg_ref[...]) + bg_ref[...])
        o_ref[...] = x2 + gate * _dot(p_ref[...].astype(BF16), wp_ref[...])

    for parity in (0, 1):
        pl.when(i % 2 == parity)(functools.partial(block, parity))
    for parity in (0, 1):
        pl.when((i == n - 1) & (i % 2 == parity))(functools.partial(drain, 1 - parity))


def _combine_ple(pos_flat, y_rows, info, x, g, b, p, wg, bg, wp):
    t = x.shape[0]
    TB = min(COMBINE_TOKENS, t)
    row = lambda i, pos: (i, 0)
    fix = lambda i, pos: (0, 0)
    return pl.pallas_call(
        _combine_ple_kernel,
        grid_spec=pltpu.PrefetchScalarGridSpec(
            num_scalar_prefetch=1,
            grid=(t // TB,),
            in_specs=[
                pl.BlockSpec(memory_space=pl.ANY),
                pl.BlockSpec((TB, LANES), row),
                pl.BlockSpec((TB, D_MODEL), row),
                pl.BlockSpec((1, D_MODEL), fix), pl.BlockSpec((1, D_MODEL), fix),
                pl.BlockSpec((TB, PLE_DIM), row),
                pl.BlockSpec(wg.shape, fix),
                pl.BlockSpec((1, D_MODEL), fix),
                pl.BlockSpec(wp.shape, fix),
            ],
            out_specs=pl.BlockSpec((TB, D_MODEL), row),
            scratch_shapes=[
                pltpu.VMEM((2, TB * ROW_TILES, LANES), F32),
                pltpu.VMEM((2, TB * ROW_TILES, LANES), F32),
                pltpu.VMEM((TB, D_MODEL), F32),
                pltpu.SemaphoreType.DMA((2,)),
            ],
        ),
        out_shape=jax.ShapeDtypeStruct((t, D_MODEL), F32),
        compiler_params=_cparams(("arbitrary",)),
        name="moe_combine_ln_ple",
    )(pos_flat, y_rows, info, x, g, b, p, wg, bg, wp)


def _hier_moe_ln_ple(x, x_folded, wr, br, wg, wu, wd, layer, ln_g, ln_b, p, ple_wg, ple_bg, ple_wp):
    t = x.shape[0]
    info, cnt = _router(x, wr, br)
    counts = cnt[0, ROUTE_LANE0:ROUTE_LANE0 + MOE_EXPERTS].astype(I32)
    padded = (counts + EXPERT_ROWS - 1) // EXPERT_ROWS * EXPERT_ROWS
    pad_end = jnp.cumsum(padded)
    pad_start = jnp.zeros((1, LANES), F32).at[0, :MOE_EXPERTS].set((pad_end - padded).astype(F32))
    n_rows = 2 * t + MOE_EXPERTS * EXPERT_ROWS
    n_blocks = n_rows // EXPERT_ROWS
    blk_start = jnp.arange(n_blocks, dtype=I32) * EXPERT_ROWS
    blk_e = jnp.minimum(jnp.sum(pad_end[None, :] <= blk_start[:, None], axis=1), MOE_EXPERTS - 1).astype(I32)
    n_used = (pad_end[-1:] // EXPERT_ROWS).astype(I32)
    pos = _plan(info, pad_start)[:, :2].reshape(-1)
    y_rows = _experts(blk_e, n_used, _invert(pos, n_rows), x_folded, wg, wu, wd, layer)
    return _combine_ple(pos, y_rows, info, x, ln_g, ln_b, p, ple_wg, ple_bg, ple_wp)


def _rope_tables(positions):
    inv = ROPE_THETA ** (-jnp.arange(0, ROPE_DIM, 2, dtype=F32) / ROPE_DIM)
    ang = positions.astype(F32)[:, None] * inv
    cos, sin = jnp.cos(ang), jnp.sin(ang)
    cos_t = jnp.tile(cos, (1, LANES // (ROPE_DIM // 2)))
    sin_t = jnp.tile(jnp.concatenate([-sin, sin], axis=1), (1, LANES // ROPE_DIM))
    return cos_t, sin_t


def _pad_cols(w, n):
    return jnp.pad(w, ((0, 0), (0, n - w.shape[1])))


def _cat_bf16(parts, n):
    parts = [p.astype(BF16) for p in parts]
    used = sum(p.shape[1] for p in parts)
    return jnp.concatenate(parts + [jnp.zeros((parts[0].shape[0], n - used), BF16)], axis=1)


def _even_w_in(w):
    z, xbc, dt, q, kv = jnp.split(w, [2048, 5120, 5152, 6176], axis=1)
    return _cat_bf16([z, xbc, q, kv, dt], EV_NP)


def _odd_w_in(w):
    q, ckv, krope, qi, ki, wi = jnp.split(w, [3072, 3584, 3648, 4672, 4736], axis=1)
    q = q.reshape(D_MODEL, MLA_HEADS, MLA_NOPE + MLA_ROPE)
    qn = q[:, :, :MLA_NOPE].reshape(D_MODEL, -1)
    qr = q[:, :, MLA_NOPE:].reshape(D_MODEL, -1)
    return _cat_bf16([qn, qr, qi, ckv, krope, ki, wi], OD_NP)


def _head_expand_matrix():
    e = np.zeros((LANES, SSM_INNER), np.float32)
    for h in range(SSM_HEADS):
        e[h, h * SSM_HEAD_DIM:(h + 1) * SSM_HEAD_DIM] = 1.0
    return jnp.asarray(np.tile(e, (3, 1)), BF16)


def kernel(x, p, positions, ev_w_in, ev_conv_w, ev_conv_b, ev_dt_bias, ev_a_log, ev_d_skip, ev_ssm_norm, ev_sinks, ev_w_out, od_w_in, od_kv_norm, od_w_uk, od_w_uv, od_w_out, ln1_g, ln1_b, ln2_g, ln2_b, moe_router_group, moe_router_group_b, moe_router_expert, moe_router_expert_b, moe_w_gate, moe_w_up, moe_w_down, ple_w_proj, ple_w_gate, ple_b_gate):
    batch, s, d = x.shape
    assert batch == 1 and d == D_MODEL
    xs = x[0]
    cos_t, sin_t = _rope_tables(positions[0])
    e_mat = _head_expand_matrix()
    topk = min(IDX_TOPK_MAX, s // 4)
    row = lambda v: v.reshape(1, -1)
    pad_row = lambda v: _pad_cols(v.reshape(1, -1), LANES)
    for i in range(DEPTH):
        j = i // 2
        if i % 2 == 0:
            xp = _inproj(xs, _even_w_in(ev_w_in[j]))
            y_ssm = _ssd(xp, ev_conv_w[j], row(ev_conv_b[j]), pad_row(ev_dt_bias[j]), pad_row(ev_a_log[j]),
                         row(jnp.repeat(ev_d_skip[j], SSM_HEAD_DIM)), row(ev_ssm_norm[j]), e_mat)
            y_att = _swa(xp, ev_sinks[j], cos_t, sin_t)
            w_out = ev_w_out[j].astype(BF16)
            xs, xf = _even_out(y_ssm, y_att, w_out[:SSM_INNER], w_out[SSM_INNER:], xs, row(ln1_g[i]), row(ln1_b[i]))
        else:
            xp = _inproj(xs, _odd_w_in(od_w_in[j]))
            qlat, qrope, qidx, ckvn, kr, ki, wis = _dsa_prep(
                xp, od_w_uk[j].astype(BF16), row(od_kv_norm[j]), cos_t, sin_t)
            bias = _dsa_select(qidx, wis, ki, topk)
            olat = _dsa_attn(qlat, qrope, bias, ckvn, kr)
            xs, xf = _odd_out(olat, od_w_uv[j].astype(BF16), od_w_out[j].astype(BF16), xs, row(ln1_g[i]), row(ln1_b[i]))
        wr = _pad_cols(jnp.concatenate([moe_router_group[i], moe_router_expert[i]], axis=1), LANES)
        br = pad_row(jnp.concatenate([moe_router_group_b[i], moe_router_expert_b[i]]))
        xs = _hier_moe_ln_ple(xs, xf, wr, br, moe_w_gate, moe_w_up, moe_w_down, i, row(ln2_g[i]), row(ln2_b[i]),
                              p[i, 0], ple_w_gate[i].astype(BF16), row(ple_b_gate[i]), ple_w_proj[i].astype(BF16))
    return xs[None]
```

```python
import functools

import jax
import jax.numpy as jnp
import numpy as np
from jax import lax
from jax.experimental import pallas as pl
from jax.experimental.pallas import tpu as pltpu

F32 = jnp.float32
BF16 = jnp.bfloat16
I32 = jnp.int32

D_MODEL = 2048
DEPTH = 4
ROPE_THETA = 10000.0
ROPE_DIM = 64
NORM_EPS = 1e-5
SSM_HEADS = 32
SSM_HEAD_DIM = 64
SSM_INNER = SSM_HEADS * SSM_HEAD_DIM
SSM_GROUPS = 4
SSM_STATE = 128
SSM_CONV = 4
SSM_CHUNK = 128
SWA_Q_HEADS = 16
SWA_KV_HEADS = 2
ATTN_BLOCK = 128
MLA_HEADS = 16
MLA_NOPE = 128
MLA_ROPE = ROPE_DIM
MLA_V = 128
MLA_RANK = 512
MLA_SCALE = (MLA_NOPE + MLA_ROPE) ** -0.5
IDX_HEADS = 16
IDX_DIM = ROPE_DIM
IDX_TOPK_MAX = 256
MOE_GROUPS = 4
MOE_EPG = 8
MOE_EXPERTS = MOE_GROUPS * MOE_EPG
MOE_FF = 512
PLE_DIM = 256
DEEPNORM_ALPHA = (2 * DEPTH) ** 0.25

LANES = 128
SUBLANES = 8
V7X_VMEM_BYTES = 64 * 1024 * 1024
COMPILER_RESERVE_BYTES = 8 * 1024 * 1024
VMEM_LIMIT_BYTES = V7X_VMEM_BYTES - COMPILER_RESERVE_BYTES
ROW_TILES = D_MODEL // LANES

EXPERT_ROWS = 256
DSA_SEL_Q = 128
DSA_ATT_Q = 64
DSA_KC = 512
DSA_ATT_ROW_GROUPS = 4
DSA_SEL_HEADS_PER_DOT = 4
MASK_NEG = -1e30

EV_Z, EV_XS, EV_BC, EV_Q, EV_KV, EV_DT = 0, 2048, 4096, 5120, 6144, 6400
EV_NP = 6656
OD_QN, OD_QR, OD_QI, OD_CKV, OD_KK, OD_WI = 0, 2048, 3072, 4096, 4608, 4736
OD_NP = 5120


def _cparams(sem, vmem=VMEM_LIMIT_BYTES):
    return pltpu.CompilerParams(dimension_semantics=sem, vmem_limit_bytes=vmem)


def _dot(a, b):
    return jnp.dot(a, b, preferred_element_type=F32)


def _dot_nt(a, b):
    return lax.dot_general(a, b, (((1,), (1,)), ((), ())), preferred_element_type=F32)


def _split3(v):
    hi = v.astype(BF16)
    r = v - hi.astype(F32)
    mid = r.astype(BF16)
    lo = (r - mid.astype(F32)).astype(BF16)
    return hi, mid, lo


def _expand(v, e3):
    return _dot(jnp.concatenate(_split3(v), axis=1), e3)


def _silu(v):
    return v * jax.nn.sigmoid(v)


def _layer_norm(v, g, b):
    mu = jnp.mean(v, axis=-1, keepdims=True)
    vc = v - mu
    var = jnp.mean(vc * vc, axis=-1, keepdims=True)
    return vc * lax.rsqrt(var + NORM_EPS) * g + b


def _rope_tile(t, c, s):
    lane = lax.broadcasted_iota(I32, t.shape, 1)
    first_half = (lane & 32) == 0
    swapped = jnp.where(first_half, pltpu.roll(t, LANES - 32, 1), pltpu.roll(t, 32, 1))
    return t * c + swapped * s


def _inproj_kernel(x_ref, w_ref, o_ref):
    o_ref[...] = _dot(x_ref[...].astype(BF16), w_ref[...])


def _inproj(x, w):
    m, k = x.shape
    n = w.shape[1]
    tm, tn = min(2048, m), 512
    return pl.pallas_call(
        _inproj_kernel,
        grid=(m // tm, n // tn),
        in_specs=[pl.BlockSpec((tm, k), lambda i, j: (i, 0)), pl.BlockSpec((k, tn), lambda i, j: (0, j))],
        out_specs=pl.BlockSpec((tm, tn), lambda i, j: (i, j)),
        out_shape=jax.ShapeDtypeStruct((m, n), F32),
        compiler_params=_cparams(("parallel", "arbitrary")),
        name="inproj",
    )(x, w)


def _store_row_layouts(y, o_ref, of_ref):
    o_ref[...] = y
    for j in range(ROW_TILES):
        of_ref[pl.ds(j, y.shape[0], stride=ROW_TILES), :] = y[:, j * LANES:(j + 1) * LANES]


def _row_layout_outputs(m, tm):
    specs = [pl.BlockSpec((tm, D_MODEL), lambda i: (i, 0)), pl.BlockSpec((tm * ROW_TILES, LANES), lambda i: (i, 0))]
    shapes = [jax.ShapeDtypeStruct((m, D_MODEL), F32), jax.ShapeDtypeStruct((m * ROW_TILES, LANES), F32)]
    return specs, shapes


def _even_out_kernel(a1_ref, a2_ref, w1_ref, w2_ref, x_ref, g_ref, b_ref, o_ref, of_ref):
    mix = _dot(a1_ref[...], w1_ref[...]) + _dot(a2_ref[...], w2_ref[...])
    _store_row_layouts(_layer_norm(DEEPNORM_ALPHA * x_ref[...] + mix, g_ref[...], b_ref[...]), o_ref, of_ref)


def _even_out(y_ssm, y_att, w1, w2, x, g, b):
    m = x.shape[0]
    tm = min(256, m)
    row = lambda i: (i, 0)
    fix = lambda i: (0, 0)
    out_specs, out_shape = _row_layout_outputs(m, tm)
    return pl.pallas_call(
        _even_out_kernel,
        grid=(m // tm,),
        in_specs=[
            pl.BlockSpec((tm, y_ssm.shape[1]), row),
            pl.BlockSpec((tm, y_att.shape[1]), row),
            pl.BlockSpec(w1.shape, fix),
            pl.BlockSpec(w2.shape, fix),
            pl.BlockSpec((tm, D_MODEL), row),
            pl.BlockSpec((1, D_MODEL), fix),
            pl.BlockSpec((1, D_MODEL), fix),
        ],
        out_specs=out_specs,
        out_shape=out_shape,
        compiler_params=_cparams(("parallel",)),
        name="even_out_ln",
    )(y_ssm, y_att, w1, w2, x, g, b)


def _ssd_kernel(z_ref, xs_ref, bc_ref, dt_ref, cwx_ref, cbx_ref, cwb_ref, cbb_ref, dtb_ref, alog_ref,
                dsk_ref, nrm_ref, e_ref, y_ref, xs_ext, bc_ext, st_ref):
    c = pl.program_id(0)
    L = SSM_CHUNK
    halo = SUBLANES

    @pl.when(c == 0)
    def _():
        xs_ext[0:halo, :] = jnp.zeros((halo, xs_ext.shape[1]), F32)
        bc_ext[0:halo, :] = jnp.zeros((halo, bc_ext.shape[1]), F32)
        st_ref[...] = jnp.zeros(st_ref.shape, F32)

    xs_ext[halo:halo + L, :] = xs_ref[...]
    bc_ext[halo:halo + L, :] = bc_ref[...]

    def conv(ext, w_ref, b_ref):
        acc = b_ref[...]
        for j in range(SSM_CONV):
            lo = halo - (SSM_CONV - 1) + j
            acc = acc + ext[lo:lo + L, :] * w_ref[j:j + 1, :]
        return acc

    xs = _silu(conv(xs_ext, cwx_ref, cbx_ref))
    bc = _silu(conv(bc_ext, cwb_ref, cbb_ref))
    xs_ext[0:halo, :] = xs_ext[L:L + halo, :]
    bc_ext[0:halo, :] = bc_ext[L:L + halo, :]

    pre = dt_ref[...] + dtb_ref[...]
    dt = jnp.maximum(pre, 0.0) + jnp.log1p(jnp.exp(-jnp.abs(pre)))
    a = dt * (-jnp.exp(alog_ref[...]))
    row = lax.broadcasted_iota(I32, (L, LANES), 0)
    acs = a
    s = 1
    while s < L:
        acs = acs + jnp.where(row >= s, pltpu.roll(acs, s, 0), 0.0)
        s *= 2
    a_last = acs[L - 1:L, :]
    e = e_ref[...]
    dt_x = _expand(dt, e)
    dte_x = _expand(dt * jnp.exp(a_last - acs), e)
    eacs_x = _expand(jnp.exp(acs), e)
    cd_x = _expand(jnp.broadcast_to(jnp.exp(a_last), (SUBLANES, LANES)), e)[0:1, :]
    acs_t = acs.T

    xdt = (xs * dt_x).astype(BF16)
    xd = (xs * dte_x).astype(BF16)
    tri = lax.broadcasted_iota(I32, (L, L), 0) >= lax.broadcasted_iota(I32, (L, L), 1)
    first_head = lax.broadcasted_iota(I32, (L, LANES), 1) < SSM_HEAD_DIM
    n_state = SSM_STATE
    gw = SSM_INNER // SSM_GROUPS
    ys = []
    for g in range(SSM_GROUPS):
        bg = bc[:, g * n_state:(g + 1) * n_state]
        cg = bc[:, SSM_GROUPS * n_state + g * n_state:SSM_GROUPS * n_state + (g + 1) * n_state]
        bb, cb16 = bg.astype(BF16), cg.astype(BF16)
        cbm = _dot_nt(cb16, bb)
        st = st_ref[g]
        y_off = _dot(cb16, st.astype(BF16)) * eacs_x[:, g * gw:(g + 1) * gw]
        st_ref[g] = st * cd_x[:, g * gw:(g + 1) * gw] + _dot(bg.T.astype(BF16), xd[:, g * gw:(g + 1) * gw])
        parts = []
        for j in range(gw // LANES):
            lo = g * gw + j * LANES
            xp = xdt[:, lo:lo + LANES]
            out = None
            for par in (0, 1):
                h = lo // SSM_HEAD_DIM + par
                seg = acs[:, h:h + 1] - acs_t[h:h + 1, :]
                lm = (jnp.exp(jnp.where(tri, seg, -jnp.inf)) * cbm).astype(BF16)
                xm = jnp.where(first_head if par == 0 else jnp.logical_not(first_head), xp, jnp.zeros_like(xp))
                d = _dot(lm, xm)
                out = d if out is None else out + d
            parts.append(out)
        ys.append(jnp.concatenate(parts, axis=1) + y_off)
    y = jnp.concatenate(ys, axis=1) + xs * dsk_ref[...]
    y = y * _silu(z_ref[...])
    outs = []
    for g in range(SSM_GROUPS):
        yg = y[:, g * gw:(g + 1) * gw]
        ms = jnp.mean(yg * yg, axis=-1, keepdims=True)
        outs.append(yg * lax.rsqrt(ms + NORM_EPS))
    y_ref[...] = (jnp.concatenate(outs, axis=1) * nrm_ref[...]).astype(BF16)


def _ssd(xp, cw, cb, dtb, alog, dsk, nrm, e):
    s = xp.shape[0]
    L = SSM_CHUNK
    bcw = 2 * SSM_GROUPS * SSM_STATE
    cwx, cwb = cw[:, :SSM_INNER], cw[:, SSM_INNER:]
    cbx, cbb = cb[:, :SSM_INNER], cb[:, SSM_INNER:]
    fix = lambda i: (0, 0)
    return pl.pallas_call(
        _ssd_kernel,
        grid=(s // L,),
        in_specs=[
            pl.BlockSpec((L, SSM_INNER), lambda i: (i, EV_Z // SSM_INNER)),
            pl.BlockSpec((L, SSM_INNER), lambda i: (i, EV_XS // SSM_INNER)),
            pl.BlockSpec((L, bcw), lambda i: (i, EV_BC // bcw)),
            pl.BlockSpec((L, LANES), lambda i: (i, EV_DT // LANES)),
            pl.BlockSpec(cwx.shape, fix), pl.BlockSpec(cbx.shape, fix),
            pl.BlockSpec(cwb.shape, fix), pl.BlockSpec(cbb.shape, fix),
            pl.BlockSpec((1, LANES), fix), pl.BlockSpec((1, LANES), fix),
            pl.BlockSpec((1, SSM_INNER), fix), pl.BlockSpec((1, SSM_INNER), fix),
            pl.BlockSpec(e.shape, fix),
        ],
        out_specs=pl.BlockSpec((L, SSM_INNER), lambda i: (i, 0)),
        out_shape=jax.ShapeDtypeStruct((s, SSM_INNER), BF16),
        scratch_shapes=[
            pltpu.VMEM((L + 2 * SUBLANES, SSM_INNER), F32),
            pltpu.VMEM((L + 2 * SUBLANES, bcw), F32),
            pltpu.VMEM((SSM_GROUPS, SSM_STATE, SSM_INNER // SSM_GROUPS), F32),
        ],
        compiler_params=_cparams(("arbitrary",)),
        name="ssd_scan",
    )(xp, xp, xp, xp, cwx, cbx, cwb, cbb, dtb, alog, dsk, nrm, e)


def _swa_kernel(sink_ref, q_ref, kvc_ref, kvp_ref, cq_ref, sq_ref, cp_ref, sp_ref, o_ref):
    i = pl.program_id(0)
    B = ATTN_BLOCK
    lane = lax.broadcasted_iota(I32, (B, LANES), 1)
    lo_half = lane < ROPE_DIM
    cq, sq = cq_ref[...], sq_ref[...]
    kc = _rope_tile(kvc_ref[:, 0:LANES], cq, sq)
    kp = _rope_tile(kvp_ref[:, 0:LANES], cp_ref[...], sp_ref[...])
    kcat = jnp.concatenate([kp, kc], axis=0)
    kmat = (kcat.astype(BF16), pltpu.roll(kcat, ROPE_DIM, 1).astype(BF16))
    vcat = jnp.concatenate([kvp_ref[:, LANES:2 * LANES], kvc_ref[:, LANES:2 * LANES]], axis=0)
    vrol = pltpu.roll(vcat, ROPE_DIM, 1)
    lane2 = lax.broadcasted_iota(I32, (2 * B, LANES), 1) < ROPE_DIM
    vdup = (jnp.where(lane2, vcat, vrol).astype(BF16), jnp.where(lane2, vrol, vcat).astype(BF16))
    r = lax.broadcasted_iota(I32, (B, 2 * B), 0)
    col = lax.broadcasted_iota(I32, (B, 2 * B), 1)
    mask = (col > r) & (col <= r + B) & ((i > 0) | (col >= B))
    scale = ROPE_DIM ** -0.5
    hpg = SWA_Q_HEADS // SWA_KV_HEADS
    for j in range(SWA_Q_HEADS // 2):
        g = (2 * j) // hpg
        qt = _rope_tile(q_ref[:, j * LANES:(j + 1) * LANES], cq, sq)
        outs = []
        for par in (0, 1):
            h = 2 * j + par
            qm = jnp.where(lo_half if par == 0 else jnp.logical_not(lo_half), qt, 0.0).astype(BF16)
            logit = _dot_nt(qm, kmat[0] if par == g else kmat[1]) * scale
            logit = jnp.where(mask, logit, -jnp.inf)
            sink = sink_ref[h]
            m = jnp.maximum(jnp.max(logit, axis=-1, keepdims=True), sink)
            ex = jnp.exp(logit - m)
            prob = ex / (jnp.sum(ex, axis=-1, keepdims=True) + jnp.exp(sink - m))
            outs.append(_dot(prob.astype(BF16), vdup[g]))
        o_ref[:, j * LANES:(j + 1) * LANES] = jnp.where(lo_half, outs[0], outs[1]).astype(BF16)


def _swa(xp, sinks, cos_t, sin_t):
    s = xp.shape[0]
    B = ATTN_BLOCK
    qw = SWA_Q_HEADS * ROPE_DIM
    kvw = 2 * SWA_KV_HEADS * ROPE_DIM
    prev = lambda i: (jnp.maximum(i - 1, 0), 0)
    cur = lambda i: (i, 0)
    return pl.pallas_call(
        _swa_kernel,
        grid=(s // B,),
        in_specs=[
            pl.BlockSpec(memory_space=pltpu.SMEM),
            pl.BlockSpec((B, qw), lambda i: (i, EV_Q // qw)),
            pl.BlockSpec((B, kvw), lambda i: (i, EV_KV // kvw)),
            pl.BlockSpec((B, kvw), lambda i: (jnp.maximum(i - 1, 0), EV_KV // kvw)),
            pl.BlockSpec((B, LANES), cur), pl.BlockSpec((B, LANES), cur),
            pl.BlockSpec((B, LANES), prev), pl.BlockSpec((B, LANES), prev),
        ],
        out_specs=pl.BlockSpec((B, qw), cur),
        out_shape=jax.ShapeDtypeStruct((s, qw), BF16),
        compiler_params=_cparams(("parallel",)),
        name="swa_sink",
    )(sinks, xp, xp, xp, cos_t, sin_t, cos_t, sin_t)


def _dsa_prep_kernel(qn_ref, qr_ref, qi_ref, ckv_ref, kk_ref, wi_ref, wuk_ref, kvn_ref, c_ref, s_ref,
                     qlat_ref, qrope_ref, qidx_ref, ckvn_ref, kr_ref, ki_ref, wis_ref):
    c, s = c_ref[...], s_ref[...]
    lane = lax.broadcasted_iota(I32, c.shape, 1)
    lo_half = lane < ROPE_DIM
    for h in range(MLA_HEADS):
        qn = qn_ref[:, h * MLA_NOPE:(h + 1) * MLA_NOPE].astype(BF16)
        qlat_ref[h] = _dot(qn, wuk_ref[h]).astype(BF16)
    for src, dst in ((qr_ref, qrope_ref), (qi_ref, qidx_ref)):
        for j in range(MLA_HEADS // 2):
            t = _rope_tile(src[:, j * LANES:(j + 1) * LANES], c, s)
            dst[2 * j] = jnp.where(lo_half, t, 0.0).astype(BF16)
            dst[2 * j + 1] = jnp.where(lo_half, pltpu.roll(t, ROPE_DIM, 1), 0.0).astype(BF16)
    kk = _rope_tile(kk_ref[...], c, s)
    kr_ref[...] = jnp.where(lo_half, kk, 0.0).astype(BF16)
    ki_ref[...] = jnp.where(lo_half, pltpu.roll(kk, ROPE_DIM, 1), 0.0).astype(BF16)
    ckv = ckv_ref[...]
    ms = jnp.mean(ckv * ckv, axis=-1, keepdims=True)
    ckvn_ref[...] = (ckv * lax.rsqrt(ms + NORM_EPS) * kvn_ref[...]).astype(BF16)
    wis_ref[...] = wi_ref[...] * (IDX_HEADS ** -0.5 * IDX_DIM ** -0.5)


def _dsa_prep(xp, wuk, kvn, cos_t, sin_t):
    s = xp.shape[0]
    tm = min(256, s)
    H = MLA_HEADS
    fix2 = lambda i: (0, 0)
    hrow = lambda i: (0, i, 0)
    row = lambda i: (i, 0)
    return pl.pallas_call(
        _dsa_prep_kernel,
        grid=(s // tm,),
        in_specs=[
            pl.BlockSpec((tm, 2048), lambda i: (i, OD_QN // 2048)),
            pl.BlockSpec((tm, 1024), lambda i: (i, OD_QR // 1024)),
            pl.BlockSpec((tm, 1024), lambda i: (i, OD_QI // 1024)),
            pl.BlockSpec((tm, MLA_RANK), lambda i: (i, OD_CKV // MLA_RANK)),
            pl.BlockSpec((tm, LANES), lambda i: (i, OD_KK // LANES)),
            pl.BlockSpec((tm, LANES), lambda i: (i, OD_WI // LANES)),
            pl.BlockSpec(wuk.shape, lambda i: (0, 0, 0)),
            pl.BlockSpec((1, MLA_RANK), fix2),
            pl.BlockSpec((tm, LANES), row), pl.BlockSpec((tm, LANES), row),
        ],
        out_specs=[
            pl.BlockSpec((H, tm, MLA_RANK), hrow),
            pl.BlockSpec((H, tm, LANES), hrow),
            pl.BlockSpec((H, tm, LANES), hrow),
            pl.BlockSpec((tm, MLA_RANK), row),
            pl.BlockSpec((tm, LANES), row),
            pl.BlockSpec((tm, LANES), row),
            pl.BlockSpec((tm, LANES), row),
        ],
        out_shape=[
            jax.ShapeDtypeStruct((H, s, MLA_RANK), BF16),
            jax.ShapeDtypeStruct((H, s, LANES), BF16),
            jax.ShapeDtypeStruct((H, s, LANES), BF16),
            jax.ShapeDtypeStruct((s, MLA_RANK), BF16),
            jax.ShapeDtypeStruct((s, LANES), BF16),
            jax.ShapeDtypeStruct((s, LANES), BF16),
            jax.ShapeDtypeStruct((s, LANES), F32),
        ],
        compiler_params=_cparams(("parallel",)),
        name="dsa_prep",
    )(xp, xp, xp, xp, xp, xp, wuk, kvn, cos_t, sin_t)


def _dsa_select_kernel(topk, qi_ref, wi_ref, ki_ref, bias_ref, key_ref, dig_ref):
    i = pl.program_id(0)
    Q, KC = DSA_SEL_Q, DSA_KC
    H = IDX_HEADS
    n_chunks = bias_ref.shape[0]
    n_vis = ((i + 1) * Q + KC - 1) // KC
    wi = wi_ref[...]
    qpos = i * Q + lax.broadcasted_iota(I32, (KC, Q), 1)
    kloc = lax.broadcasted_iota(I32, (KC, Q), 0)
    HG = DSA_SEL_HEADS_PER_DOT
    N_ACC = 4
    DIGIT_BITS = (11, 11, 10)
    DIGIT_SHIFT = (21, 10, 0)
    TOP_BIAS = 1 << (DIGIT_BITS[0] - 1)
    GUARD = jnp.int32(-0x7FFF8000)
    FIELD_ONES = jnp.int32(0x00010001)
    HK = KC // 2

    def pack(lo, hi):
        return lo | (hi << 16) | GUARD

    def score_chunk(c, carry):
        k = ki_ref[pl.ds(pl.multiple_of(c * KC, KC), KC), :]
        acc = jnp.zeros((KC, Q), F32)
        for g in range(H // HG):
            sc = _dot_nt(k, qi_ref[g * HG:(g + 1) * HG].reshape(HG * Q, LANES))
            for hh in range(HG):
                h = g * HG + hh
                acc = acc + jnp.maximum(sc[:, hh * Q:(hh + 1) * Q], 0.0) * wi[h:h + 1, :]
        acc = jnp.where(c * KC + kloc <= qpos, acc, -jnp.inf)
        bits = pltpu.bitcast(acc, I32)
        key = bits ^ ((bits >> 31) & jnp.int32(0x7FFFFFFF))
        key_ref[c] = key
        for d in range(len(DIGIT_BITS)):
            dig = (key >> DIGIT_SHIFT[d]) + TOP_BIAS if d == 0 else (key >> DIGIT_SHIFT[d]) & ((1 << DIGIT_BITS[d]) - 1)
            dig_ref[d, c] = pack(dig[:HK], dig[HK:]) + FIELD_ONES
        return carry

    lax.fori_loop(0, n_vis, score_chunk, 0)

    def count_ge(d, cand):
        cand2 = (cand + 1) | ((cand + 1) << 16)

        def body(c, accs):
            v = dig_ref[d, c]
            accs = list(accs)
            for r in range(HK // SUBLANES):
                w = v[r * SUBLANES:(r + 1) * SUBLANES, :] - cand2
                accs[r % N_ACC] = accs[r % N_ACC] + (lax.shift_right_logical(w, 15) & FIELD_ONES)
            return tuple(accs)
        accs = lax.fori_loop(0, n_vis, body, tuple(jnp.zeros((SUBLANES, Q), I32) for _ in range(N_ACC)))
        acc = sum(accs)
        cnt = (acc & 0xFFFF) + lax.shift_right_logical(acc, 16)
        return jnp.sum(cnt.astype(F32), axis=0, keepdims=True)

    def search(d, need):
        def bit_step(b, thr):
            cand = thr | (jnp.int32(1) << (DIGIT_BITS[d] - 1 - b))
            return jnp.where(count_ge(d, cand) >= need, cand, thr)

        return lax.fori_loop(0, DIGIT_BITS[d], bit_step, jnp.zeros((1, Q), I32))

    def drop_unless_equal(d, p):
        def body(c, carry):
            cur, nxt = dig_ref[d, c], dig_ref[d + 1, c]
            lo = jnp.where((cur & 0x7FFF) == p + 1, nxt & 0x7FFF, 0)
            hi = jnp.where((lax.shift_right_logical(cur, 16) & 0x7FFF) == p + 1, lax.shift_right_logical(nxt, 16) & 0x7FFF, 0)
            dig_ref[d + 1, c] = pack(lo, hi)
            return carry

        lax.fori_loop(0, n_vis, body, 0)

    need = jnp.full((1, Q), float(topk), F32)
    thr = jnp.zeros((1, Q), I32)
    for d in range(len(DIGIT_BITS)):
        p = search(d, need)
        thr = thr + ((p - (TOP_BIAS if d == 0 else 0)) << DIGIT_SHIFT[d])
        if d + 1 < len(DIGIT_BITS):
            need = need - count_ge(d, p + 1)
            drop_unless_equal(d, p)

    def emit(c, cnt):
        sel = (key_ref[c] >= thr) & (c * KC + kloc <= qpos)
        bias_ref[c] = jnp.where(sel, 0.0, MASK_NEG).T
        return cnt + jnp.sum(jnp.where(sel, 1.0, 0.0), axis=0, keepdims=True)

    n_sel = lax.fori_loop(0, n_vis, emit, jnp.zeros((1, Q), F32))

    @pl.when(jnp.max(n_sel) > topk)
    def _():
        def count(pred):
            def body(c, cnt):
                return cnt + jnp.sum(jnp.where(pred(c), 1.0, 0.0), axis=0, keepdims=True)
            return lax.fori_loop(0, n_vis, body, jnp.zeros((1, Q), F32))

        def tied(c):
            return (key_ref[c] == thr) & (c * KC + kloc <= qpos)

        need_tied = topk - count(lambda c: key_ref[c] > thr)

        def bit_step(b, last):
            cand = last | (jnp.int32(1) << (index_bits - 1 - b))
            in_front = count(lambda c: tied(c) & (c * KC + kloc < cand))
            return jnp.where(in_front < need_tied, cand, last)

        index_bits = (n_chunks * KC - 1).bit_length()
        last = lax.fori_loop(0, index_bits, bit_step, jnp.zeros((1, Q), I32))

        def emit_ties(c, carry):
            sel = (key_ref[c] > thr) | (tied(c) & (c * KC + kloc <= last))
            bias_ref[c] = jnp.where(sel, 0.0, MASK_NEG).T
            return carry

        lax.fori_loop(0, n_vis, emit_ties, 0)

    def fill(c, carry):
        bias_ref[c] = jnp.full((Q, KC), MASK_NEG, F32)
        return carry

    lax.fori_loop(n_vis, n_chunks, fill, 0)


def _dsa_select(qidx, wis, ki, topk):
    H, s, _ = qidx.shape
    Q, KC = DSA_SEL_Q, DSA_KC
    assert Q == LANES
    nch = s // KC
    return pl.pallas_call(
        functools.partial(_dsa_select_kernel, topk),
        grid=(s // Q,),
        in_specs=[
            pl.BlockSpec((H, Q, LANES), lambda i: (0, i, 0)),
            pl.BlockSpec((LANES, Q), lambda i: (0, i)),
            pl.BlockSpec((s, LANES), lambda i: (0, 0)),
        ],
        out_specs=pl.BlockSpec((nch, Q, KC), lambda i: (0, i, 0)),
        out_shape=jax.ShapeDtypeStruct((nch, s, KC), F32),
        scratch_shapes=[pltpu.VMEM((nch, KC, Q), I32), pltpu.VMEM((3, nch, KC // 2, Q), I32)],
        compiler_params=_cparams(("parallel",)),
        name="dsa_select",
    )(qidx, wis.T, ki)


def _dsa_attn_kernel(ql_ref, qr_ref, bias_ref, ckv_ref, kr_ref, o_ref, m_ref, l_ref, acc_ref, s_ref, p_ref):
    i = pl.program_id(0)
    Q, KC, H = DSA_ATT_Q, DSA_KC, MLA_HEADS
    n_vis = ((i + 1) * Q + KC - 1) // KC
    m_ref[...] = jnp.full(m_ref.shape, -jnp.inf, F32)
    l_ref[...] = jnp.zeros(l_ref.shape, F32)
    acc_ref[...] = jnp.zeros(acc_ref.shape, F32)

    def keys(c):
        off = pl.multiple_of(c * KC, KC)
        return ckv_ref[pl.ds(off, KC), :], kr_ref[pl.ds(off, KC), :]

    def raw_logits(c):
        ck, kr = keys(c)
        ql = ql_ref[...].reshape(H * Q, MLA_RANK)
        qr = qr_ref[...].reshape(H * Q, LANES)
        return _dot_nt(ql, ck) + _dot_nt(qr, kr)

    def consume(c, slot):
        bias = bias_ref[c][None]
        ck = keys(c)[0]
        RG = H * Q // DSA_ATT_ROW_GROUPS
        for g in range(DSA_ATT_ROW_GROUPS):
            rows = pl.ds(g * RG, RG)
            logit = s_ref[slot, rows, :] * MLA_SCALE
            logit = (logit.reshape(RG // Q, Q, KC) + bias).reshape(RG, KC)
            m_old = m_ref[rows, :]
            m_new = jnp.maximum(m_old, jnp.max(logit, axis=-1, keepdims=True))
            alpha = jnp.exp(m_old - m_new)
            p = jnp.exp(logit - m_new)
            l_ref[rows, :] = alpha * l_ref[rows, :] + jnp.sum(p, axis=-1, keepdims=True)
            p_ref[rows, :] = p.astype(BF16)
            acc_ref[rows, :] = alpha * acc_ref[rows, :]
            m_ref[rows, :] = m_new
        acc_ref[...] += _dot(p_ref[...], ck)

    s_ref[0] = raw_logits(0)

    def pair(j, carry):
        c = 2 * j
        s_ref[1] = raw_logits(c + 1)
        consume(c, 0)
        s_ref[0] = raw_logits(c + 2)
        consume(c + 1, 1)
        return carry

    n_pairs = (n_vis - 1) // 2
    lax.fori_loop(0, n_pairs, pair, 0)
    last = 2 * n_pairs

    @pl.when(last + 1 < n_vis)
    def _():
        s_ref[1] = raw_logits(last + 1)
        consume(last, 0)
        consume(last + 1, 1)

    @pl.when(last + 1 == n_vis)
    def _():
        consume(last, 0)

    o_ref[...] = (acc_ref[...] / l_ref[...]).reshape(H, Q, MLA_RANK).astype(BF16)


def _dsa_attn(qlat, qrope, bias, ckvn, kr):
    H, s, _ = qlat.shape
    Q, KC = DSA_ATT_Q, DSA_KC
    nch = s // KC
    hrow = lambda i: (0, i, 0)
    fix = lambda i: (0, 0)
    return pl.pallas_call(
        _dsa_attn_kernel,
        grid=(s // Q,),
        in_specs=[
            pl.BlockSpec((H, Q, MLA_RANK), hrow),
            pl.BlockSpec((H, Q, LANES), hrow),
            pl.BlockSpec((nch, Q, KC), hrow),
            pl.BlockSpec((s, MLA_RANK), fix),
            pl.BlockSpec((s, LANES), fix),
        ],
        out_specs=pl.BlockSpec((H, Q, MLA_RANK), hrow),
        out_shape=jax.ShapeDtypeStruct((H, s, MLA_RANK), BF16),
        scratch_shapes=[
            pltpu.VMEM((H * Q, 1), F32),
            pltpu.VMEM((H * Q, 1), F32),
            pltpu.VMEM((H * Q, MLA_RANK), F32),
            pltpu.VMEM((2, H * Q, KC), F32),
            pltpu.VMEM((H * Q, KC), BF16),
        ],
        compiler_params=_cparams(("parallel",)),
        name="dsa_attn",
    )(qlat, qrope, bias, ckvn, kr)


def _odd_out_kernel(ol_ref, wuv_ref, wo_ref, x_ref, g_ref, b_ref, o_ref, of_ref, u_ref):
    for h in range(MLA_HEADS):
        u_ref[:, h * MLA_V:(h + 1) * MLA_V] = _dot(ol_ref[h], wuv_ref[h]).astype(BF16)
    mix = _dot(u_ref[...], wo_ref[...])
    _store_row_layouts(_layer_norm(DEEPNORM_ALPHA * x_ref[...] + mix, g_ref[...], b_ref[...]), o_ref, of_ref)


def _odd_out(olat, wuv, wo, x, g, b):
    H, s, _ = olat.shape
    tm = min(256, s)
    row = lambda i: (i, 0)
    fix = lambda i: (0, 0)
    out_specs, out_shape = _row_layout_outputs(s, tm)
    return pl.pallas_call(
        _odd_out_kernel,
        grid=(s // tm,),
        in_specs=[
            pl.BlockSpec((H, tm, MLA_RANK), lambda i: (0, i, 0)),
            pl.BlockSpec(wuv.shape, lambda i: (0, 0, 0)),
            pl.BlockSpec(wo.shape, fix),
            pl.BlockSpec((tm, D_MODEL), row),
            pl.BlockSpec((1, D_MODEL), fix), pl.BlockSpec((1, D_MODEL), fix),
        ],
        out_specs=out_specs,
        out_shape=out_shape,
        scratch_shapes=[pltpu.VMEM((tm, MLA_HEADS * MLA_V), BF16)],
        compiler_params=_cparams(("parallel",)),
        name="odd_out_ln",
    )(olat, wuv, wo, x, g, b)


ROUTE_LANE0 = MOE_GROUPS


def _router_kernel(h_ref, wr_ref, br_ref, info_ref, cnt_ref, run_ref):
    i = pl.program_id(0)

    @pl.when(i == 0)
    def _():
        run_ref[...] = jnp.zeros(run_ref.shape, F32)

    h = h_ref[...]
    tm = h.shape[0]
    h_hi = h.astype(BF16)
    h_lo = (h - h_hi.astype(F32)).astype(BF16)
    w = wr_ref[...]
    w_hi = w.astype(BF16)
    w_lo = (w - w_hi.astype(F32)).astype(BF16)
    logits = _dot(h_hi, w_hi) + (_dot(h_hi, w_lo) + _dot(h_lo, w_hi)) + br_ref[...]
    lane = lax.broadcasted_iota(I32, (tm, LANES), 1)
    lane_f = lane.astype(F32)
    neg = -jnp.inf
    big = float(LANES)
    is_grp = lane < MOE_GROUPS
    gl = jnp.where(is_grp, logits, neg)
    gmax = jnp.max(gl, axis=-1, keepdims=True)
    gsel = jnp.min(jnp.where(gl == gmax, lane_f, big), axis=-1, keepdims=True)
    gsum = jnp.sum(jnp.where(is_grp, jnp.exp(logits - gmax), 0.0), axis=-1, keepdims=True)
    egrp = ((lane - ROUTE_LANE0) >> 3).astype(F32)
    valid = (lane >= ROUTE_LANE0) & (lane < ROUTE_LANE0 + MOE_EXPERTS) & (egrp == gsel)
    el = jnp.where(valid, logits, neg)
    v1 = jnp.max(el, axis=-1, keepdims=True)
    i1 = jnp.min(jnp.where(el == v1, lane_f, big), axis=-1, keepdims=True)
    el2 = jnp.where(lane_f == i1, neg, el)
    v2 = jnp.max(el2, axis=-1, keepdims=True)
    i2 = jnp.min(jnp.where(el2 == v2, lane_f, big), axis=-1, keepdims=True)
    t = jnp.exp(v2 - v1)
    p1 = 1.0 / (1.0 + t)
    p2 = t / (1.0 + t)
    ggate = 1.0 / gsum
    m1 = lane_f == i1
    m2 = lane_f == i2
    memb = jnp.where(m1 | m2, 1.0, 0.0)
    tri = (lax.broadcasted_iota(I32, (tm, tm), 0) > lax.broadcasted_iota(I32, (tm, tm), 1))
    cum = _dot(jnp.where(tri, 1.0, 0.0).astype(BF16), memb.astype(BF16)) + run_ref[...]
    rank1 = jnp.sum(jnp.where(m1, cum, 0.0), axis=-1, keepdims=True)
    rank2 = jnp.sum(jnp.where(m2, cum, 0.0), axis=-1, keepdims=True)
    run_ref[...] = run_ref[...] + jnp.sum(memb, axis=0, keepdims=True)
    info = jnp.where(lane == 0, i1 - ROUTE_LANE0, 0.0)
    info = jnp.where(lane == 1, i2 - ROUTE_LANE0, info)
    info = jnp.where(lane == 2, p1 * ggate, info)
    info = jnp.where(lane == 3, p2 * ggate, info)
    info = jnp.where(lane == 4, rank1, info)
    info = jnp.where(lane == 5, rank2, info)
    info_ref[...] = info
    cnt_ref[...] = run_ref[...]


def _router(h, wr, br):
    t = h.shape[0]
    tm = min(512, t)
    return pl.pallas_call(
        _router_kernel,
        grid=(t // tm,),
        in_specs=[
            pl.BlockSpec((tm, D_MODEL), lambda i: (i, 0)),
            pl.BlockSpec(wr.shape, lambda i: (0, 0)),
            pl.BlockSpec((1, LANES), lambda i: (0, 0)),
        ],
        out_specs=[pl.BlockSpec((tm, LANES), lambda i: (i, 0)), pl.BlockSpec((1, LANES), lambda i: (0, 0))],
        out_shape=[jax.ShapeDtypeStruct((t, LANES), F32), jax.ShapeDtypeStruct((1, LANES), F32)],
        scratch_shapes=[pltpu.VMEM((1, LANES), F32)],
        compiler_params=_cparams(("arbitrary",)),
        name="moe_router",
    )(h, wr, br)


def _plan_kernel(info_ref, ps_ref, pos_ref):
    info = info_ref[...]
    lane = lax.broadcasted_iota(I32, info.shape, 1)
    lane_f = lane.astype(F32)
    ps = ps_ref[...]
    pos1 = jnp.sum(jnp.where(lane_f == info[:, 0:1], ps, 0.0), axis=-1, keepdims=True) + info[:, 4:5]
    pos2 = jnp.sum(jnp.where(lane_f == info[:, 1:2], ps, 0.0), axis=-1, keepdims=True) + info[:, 5:6]
    pos_ref[...] = jnp.where(lane == 0, pos1, jnp.where(lane == 1, pos2, 0.0)).astype(I32)


def _plan(info, pad_start):
    t = info.shape[0]
    tm = min(1024, t)
    return pl.pallas_call(
        _plan_kernel,
        grid=(t // tm,),
        in_specs=[pl.BlockSpec((tm, LANES), lambda i: (i, 0)), pl.BlockSpec((1, LANES), lambda i: (0, 0))],
        out_specs=pl.BlockSpec((tm, LANES), lambda i: (i, 0)),
        out_shape=jax.ShapeDtypeStruct((t, LANES), I32),
        compiler_params=_cparams(("parallel",)),
        name="moe_plan",
    )(info, pad_start)


def _invert_kernel(pos_ref, rt_ref):
    def clear(r, carry):
        rt_ref[r] = 0
        return carry

    lax.fori_loop(0, rt_ref.shape[0], clear, 0, unroll=8)

    def put(n, carry):
        rt_ref[pos_ref[n]] = lax.shift_right_logical(n, 1)
        return carry

    lax.fori_loop(0, pos_ref.shape[0], put, 0, unroll=8)


def _invert(pos_flat, n_rows):
    return pl.pallas_call(
        _invert_kernel,
        in_specs=[pl.BlockSpec(memory_space=pltpu.SMEM)],
        out_specs=pl.BlockSpec(memory_space=pltpu.SMEM),
        out_shape=jax.ShapeDtypeStruct((n_rows,), I32),
        name="moe_invert",
    )(pos_flat)


def _expert_kernel(be_ref, nu_ref, rt_ref, h_ref, wg_ref, wu_ref, wd_ref, o_ref, xbuf0, xbuf1, xb, wgb, wub, wdb, sem):
    b = pl.program_id(0)
    n_used = nu_ref[0]
    R = EXPERT_ROWS

    bufs = (xbuf0, xbuf1)

    def row_copy(tok, r, slot):
        src = h_ref.at[pl.ds(pl.multiple_of(tok * ROW_TILES, ROW_TILES), ROW_TILES)]
        return pltpu.make_async_copy(src, bufs[slot].at[pl.ds(r * ROW_TILES, ROW_TILES)], sem.at[slot])

    def drain(slot):
        def body(r, carry):
            row_copy(0, 0, slot).wait()
            return carry
        lax.fori_loop(0, R, body, 0, unroll=8)

    @pl.when(b == 0)
    def _():
        def body(r, carry):
            row_copy(rt_ref[r], r, 0).start()
            return carry
        lax.fori_loop(0, R, body, 0, unroll=8)

    def block(slot):
        drain(slot)

        @pl.when((b == 0) | (be_ref[b] != be_ref[jnp.maximum(b - 1, 0)]))
        def _():
            wgb[...] = wg_ref[0, 0].astype(BF16)
            wub[...] = wu_ref[0, 0].astype(BF16)
            wdb[...] = wd_ref[0, 0].astype(BF16)

        base = (b + 1) * R
        for r in range(R):
            row_copy(rt_ref[base + r], r, 1 - slot).start()
        for j in range(ROW_TILES):
            xb[:, j * LANES:(j + 1) * LANES] = bufs[slot][pl.ds(j, R, stride=ROW_TILES), :].astype(BF16)
        x = xb[...]
        hid = _silu(_dot(x, wgb[...])) * _dot(x, wub[...])
        y = _dot(hid.astype(BF16), wdb[...])
        for j in range(ROW_TILES):
            o_ref[pl.ds(j, R, stride=ROW_TILES), :] = y[:, j * LANES:(j + 1) * LANES]

    for parity in (0, 1):
        pl.when((b < n_used) & (b % 2 == parity))(functools.partial(block, parity))

    @pl.when(b >= n_used)
    def _():
        o_ref[...] = jnp.zeros(o_ref.shape, F32)

    for parity in (0, 1):
        pl.when((b == n_used) & (b % 2 == parity))(functools.partial(drain, parity))


def _experts(blk_e, n_used, row_token, h, wg, wu, wd, layer):
    r = row_token.shape[0]
    nb = r // EXPERT_ROWS
    wmap = lambda b, be, nu, rt: (layer, be[b], 0, 0)
    return pl.pallas_call(
        _expert_kernel,
        grid_spec=pltpu.PrefetchScalarGridSpec(
            num_scalar_prefetch=3,
            grid=(nb,),
            in_specs=[
                pl.BlockSpec(memory_space=pl.ANY),
                pl.BlockSpec((1, 1, D_MODEL, MOE_FF), wmap),
                pl.BlockSpec((1, 1, D_MODEL, MOE_FF), wmap),
                pl.BlockSpec((1, 1, MOE_FF, D_MODEL), wmap),
            ],
            out_specs=pl.BlockSpec((EXPERT_ROWS * ROW_TILES, LANES), lambda b, be, nu, rt: (b, 0)),
            scratch_shapes=[
                pltpu.VMEM((EXPERT_ROWS * ROW_TILES, LANES), F32),
                pltpu.VMEM((EXPERT_ROWS * ROW_TILES, LANES), F32),
                pltpu.VMEM((EXPERT_ROWS, D_MODEL), BF16),
                pltpu.VMEM((D_MODEL, MOE_FF), BF16),
                pltpu.VMEM((D_MODEL, MOE_FF), BF16),
                pltpu.VMEM((MOE_FF, D_MODEL), BF16),
                pltpu.SemaphoreType.DMA((2,)),
            ],
        ),
        out_shape=jax.ShapeDtypeStruct((r * ROW_TILES, LANES), F32),
        compiler_params=_cparams(("arbitrary",)),
        name="moe_experts",
    )(blk_e, n_used, row_token, h, wg, wu, wd)


COMBINE_TOKENS = 256


def _combine_ple_kernel(pos_ref, y_ref, info_ref, x_ref, g_ref, b_ref, p_ref, wg_ref, bg_ref, wp_ref, o_ref,
                        buf0, buf1, ycat, sem):
    i = pl.program_id(0)
    n = pl.num_programs(0)
    TB = x_ref.shape[0]
    bufs = (buf0, buf1)

    def row_copy(src_row, k, tt, slot):
        src = y_ref.at[pl.ds(pl.multiple_of(src_row * ROW_TILES, ROW_TILES), ROW_TILES)]
        return pltpu.make_async_copy(src, bufs[slot].at[k, pl.ds(tt * ROW_TILES, ROW_TILES)], sem.at[slot])

    def drain(slot):
        def body(tt, carry):
            row_copy(0, 0, 0, slot).wait()
            row_copy(0, 1, 0, slot).wait()
            return carry
        lax.fori_loop(0, TB, body, 0, unroll=8)

    @pl.when(i == 0)
    def _():
        def body(tt, carry):
            row_copy(pos_ref[2 * tt], 0, tt, 0).start(priority=0)
            row_copy(pos_ref[2 * tt + 1], 1, tt, 0).start(priority=1)
            return carry
        lax.fori_loop(0, TB, body, 0, unroll=8)

    def block(slot):
        drain(slot)
        base = jnp.minimum(i + 1, n - 1) * (2 * TB)
        for tt in range(TB):
            row_copy(pos_ref[base + 2 * tt], 0, tt, 1 - slot).start(priority=0)
            row_copy(pos_ref[base + 2 * tt + 1], 1, tt, 1 - slot).start(priority=1)
        info = info_ref[...]
        g0, g1 = info[:, 2:3], info[:, 3:4]
        for j in range(ROW_TILES):
            rows = pl.ds(j, TB, stride=ROW_TILES)
            ycat[:, j * LANES:(j + 1) * LANES] = bufs[slot][0, rows, :] * g0 + bufs[slot][1, rows, :] * g1
        x2 = _layer_norm(DEEPNORM_ALPHA * x_ref[...] + ycat[...], g_ref[...], b_ref[...])
        gate = jax.nn.sigmoid(_dot(x2.astype(BF16), wg_ref[...]) + bg_ref[...])
        o_ref[...] = x2 + gate * _dot(p_ref[...].astype(BF16), wp_ref[...])

    for parity in (0, 1):
        pl.when(i % 2 == parity)(functools.partial(block, parity))
    for parity in (0, 1):
        pl.when((i == n - 1) & (i % 2 == parity))(functools.partial(drain, 1 - parity))


def _combine_ple(pos_flat, y_rows, info, x, g, b, p, wg, bg, wp):
    t = x.shape[0]
    TB = min(COMBINE_TOKENS, t)
    row = lambda i, pos: (i, 0)
    fix = lambda i, pos: (0, 0)
    return pl.pallas_call(
        _combine_ple_kernel,
        grid_spec=pltpu.PrefetchScalarGridSpec(
            num_scalar_prefetch=1,
            grid=(t // TB,),
            in_specs=[
                pl.BlockSpec(memory_space=pl.ANY),
                pl.BlockSpec((TB, LANES), row),
                pl.BlockSpec((TB, D_MODEL), row),
                pl.BlockSpec((1, D_MODEL), fix), pl.BlockSpec((1, D_MODEL), fix),
                pl.BlockSpec((TB, PLE_DIM), row),
                pl.BlockSpec(wg.shape, fix),
                pl.BlockSpec((1, D_MODEL), fix),
                pl.BlockSpec(wp.shape, fix),
            ],
            out_specs=pl.BlockSpec((TB, D_MODEL), row),
            scratch_shapes=[
                pltpu.VMEM((2, TB * ROW_TILES, LANES), F32),
                pltpu.VMEM((2, TB * ROW_TILES, LANES), F32),
                pltpu.VMEM((TB, D_MODEL), F32),
                pltpu.SemaphoreType.DMA((2,)),
            ],
        ),
        out_shape=jax.ShapeDtypeStruct((t, D_MODEL), F32),
        compiler_params=_cparams(("arbitrary",)),
        name="moe_combine_ln_ple",
    )(pos_flat, y_rows, info, x, g, b, p, wg, bg, wp)


def _hier_moe_ln_ple(x, x_folded, wr, br, wg, wu, wd, layer, ln_g, ln_b, p, ple_wg, ple_bg, ple_wp):
    t = x.shape[0]
    info, cnt = _router(x, wr, br)
    counts = cnt[0, ROUTE_LANE0:ROUTE_LANE0 + MOE_EXPERTS].astype(I32)
    padded = (counts + EXPERT_ROWS - 1) // EXPERT_ROWS * EXPERT_ROWS
    pad_end = jnp.cumsum(padded)
    pad_start = jnp.zeros((1, LANES), F32).at[0, :MOE_EXPERTS].set((pad_end - padded).astype(F32))
    n_rows = 2 * t + MOE_EXPERTS * EXPERT_ROWS
    n_blocks = n_rows // EXPERT_ROWS
    blk_start = jnp.arange(n_blocks, dtype=I32) * EXPERT_ROWS
    blk_e = jnp.minimum(jnp.sum(pad_end[None, :] <= blk_start[:, None], axis=1), MOE_EXPERTS - 1).astype(I32)
    n_used = (pad_end[-1:] // EXPERT_ROWS).astype(I32)
    pos = _plan(info, pad_start)[:, :2].reshape(-1)
    y_rows = _experts(blk_e, n_used, _invert(pos, n_rows), x_folded, wg, wu, wd, layer)
    return _combine_ple(pos, y_rows, info, x, ln_g, ln_b, p, ple_wg, ple_bg, ple_wp)


def _rope_tables(positions):
    inv = ROPE_THETA ** (-jnp.arange(0, ROPE_DIM, 2, dtype=F32) / ROPE_DIM)
    ang = positions.astype(F32)[:, None] * inv
    cos, sin = jnp.cos(ang), jnp.sin(ang)
    cos_t = jnp.tile(cos, (1, LANES // (ROPE_DIM // 2)))
    sin_t = jnp.tile(jnp.concatenate([-sin, sin], axis=1), (1, LANES // ROPE_DIM))
    return cos_t, sin_t


def _pad_cols(w, n):
    return jnp.pad(w, ((0, 0), (0, n - w.shape[1])))


def _cat_bf16(parts, n):
    parts = [p.astype(BF16) for p in parts]
    used = sum(p.shape[1] for p in parts)
    return jnp.concatenate(parts + [jnp.zeros((parts[0].shape[0], n - used), BF16)], axis=1)


def _even_w_in(w):
    z, xbc, dt, q, kv = jnp.split(w, [2048, 5120, 5152, 6176], axis=1)
    return _cat_bf16([z, xbc, q, kv, dt], EV_NP)


def _odd_w_in(w):
    q, ckv, krope, qi, ki, wi = jnp.split(w, [3072, 3584, 3648, 4672, 4736], axis=1)
    q = q.reshape(D_MODEL, MLA_HEADS, MLA_NOPE + MLA_ROPE)
    qn = q[:, :, :MLA_NOPE].reshape(D_MODEL, -1)
    qr = q[:, :, MLA_NOPE:].reshape(D_MODEL, -1)
    return _cat_bf16([qn, qr, qi, ckv, krope, ki, wi], OD_NP)


def _head_expand_matrix():
    e = np.zeros((LANES, SSM_INNER), np.float32)
    for h in range(SSM_HEADS):
        e[h, h * SSM_HEAD_DIM:(h + 1) * SSM_HEAD_DIM] = 1.0
    return jnp.asarray(np.tile(e, (3, 1)), BF16)


def kernel(x, p, positions, ev_w_in, ev_conv_w, ev_conv_b, ev_dt_bias, ev_a_log, ev_d_skip, ev_ssm_norm, ev_sinks, ev_w_out, od_w_in, od_kv_norm, od_w_uk, od_w_uv, od_w_out, ln1_g, ln1_b, ln2_g, ln2_b, moe_router_group, moe_router_group_b, moe_router_expert, moe_router_expert_b, moe_w_gate, moe_w_up, moe_w_down, ple_w_proj, ple_w_gate, ple_b_gate):
    batch, s, d = x.shape
    assert batch == 1 and d == D_MODEL
    xs = x[0]
    cos_t, sin_t = _rope_tables(positions[0])
    e_mat = _head_expand_matrix()
    topk = min(IDX_TOPK_MAX, s // 4)
    row = lambda v: v.reshape(1, -1)
    pad_row = lambda v: _pad_cols(v.reshape(1, -1), LANES)
    for i in range(DEPTH):
        j = i // 2
        if i % 2 == 0:
            xp = _inproj(xs, _even_w_in(ev_w_in[j]))
            y_ssm = _ssd(xp, ev_conv_w[j], row(ev_conv_b[j]), pad_row(ev_dt_bias[j]), pad_row(ev_a_log[j]),
                         row(jnp.repeat(ev_d_skip[j], SSM_HEAD_DIM)), row(ev_ssm_norm[j]), e_mat)
            y_att = _swa(xp, ev_sinks[j], cos_t, sin_t)
            w_out = ev_w_out[j].astype(BF16)
            xs, xf = _even_out(y_ssm, y_att, w_out[:SSM_INNER], w_out[SSM_INNER:], xs, row(ln1_g[i]), row(ln1_b[i]))
        else:
            xp = _inproj(xs, _odd_w_in(od_w_in[j]))
            qlat, qrope, qidx, ckvn, kr, ki, wis = _dsa_prep(
                xp, od_w_uk[j].astype(BF16), row(od_kv_norm[j]), cos_t, sin_t)
            bias = _dsa_select(qidx, wis, ki, topk)
            olat = _dsa_attn(qlat, qrope, bias, ckvn, kr)
            xs, xf = _odd_out(olat, od_w_uv[j].astype(BF16), od_w_out[j].astype(BF16), xs, row(ln1_g[i]), row(ln1_b[i]))
        wr = _pad_cols(jnp.concatenate([moe_router_group[i], moe_router_expert[i]], axis=1), LANES)
        br = pad_row(jnp.concatenate([moe_router_group_b[i], moe_router_expert_b[i]]))
        xs = _hier_moe_ln_ple(xs, xf, wr, br, moe_w_gate, moe_w_up, moe_w_down, i, row(ln2_g[i]), row(ln2_b[i]),
                              p[i, 0], ple_w_gate[i].astype(BF16), row(ple_b_gate[i]), ple_w_proj[i].astype(BF16))
    return xs[None]
```
